```python
import jax, jax.numpy as jnp
from jax import lax
import numpy as np

D_MODEL = 1024
BATCH = 16
SEQ = 2048
DEPTH = 4
DEC_BATCH = 128
DEC_SEQ = 8
PAST_LEN = 8192
PAGE_SIZE = 128

N_MLA = (DEPTH + 1) // 2
N_RWKV = DEPTH // 2
MLA_HEADS = 8
Q_RANK = 384
KV_RANK = 256
NOPE_DIM = 128
ROPE_DIM = 64
QK_DIM = NOPE_DIM + ROPE_DIM
V_DIM = 128
ROPE_THETA = 10000.0
ATTN_SCALE = QK_DIM ** -0.5
Q_BLOCK = 128
HEAD_SIZE = 64
RWKV_HEADS = D_MODEL // HEAD_SIZE
DECAY_LORA = 64
AAA_LORA = 64
MV_LORA = 32
GATE_LORA = 128
LNX_EPS = 64e-5
D_FF = 2816
RMS_EPS = 1e-6
POOL_NUM = 5
POOL_DEN = 4

kernel_name = 'hybrid_mla_rwkv7_macaron_step'


def rmsnorm(x, g, eps=RMS_EPS):
    xf = x.astype(jnp.float32)
    y = xf * lax.rsqrt(jnp.mean(xf * xf, axis=-1, keepdims=True) + eps)
    return (y * g.astype(jnp.float32)).astype(x.dtype)


def swiglu(h, w_gate, w_up, w_down):
    return (jax.nn.silu(h @ w_gate) * (h @ w_up)) @ w_down


def rope_tables(pos):
    inv = ROPE_THETA ** (-jnp.arange(0, ROPE_DIM, 2, dtype=jnp.float32) / ROPE_DIM)
    ang = pos.astype(jnp.float32)[:, None] * inv[None, :]
    return jnp.cos(ang), jnp.sin(ang)


def apply_rope(x, cos, sin):
    half = ROPE_DIM // 2
    xf = x.astype(jnp.float32)
    x1, x2 = xf[..., :half], xf[..., half:]
    return jnp.concatenate([x1 * cos - x2 * sin, x1 * sin + x2 * cos], axis=-1).astype(x.dtype)


def mla_project(h, pos, w_down, g_qlat, g_kvlat, w_uq, g_qn):
    b, l, _ = h.shape
    lat = h @ w_down
    cq = rmsnorm(lat[..., :Q_RANK], g_qlat)
    c = rmsnorm(lat[..., Q_RANK:Q_RANK + KV_RANK], g_kvlat)
    cos, sin = rope_tables(pos)
    kr = apply_rope(lat[..., Q_RANK + KV_RANK:], cos, sin)
    q = (cq @ w_uq).reshape(b, l, MLA_HEADS, QK_DIM)
    q = jnp.concatenate([q[..., :NOPE_DIM],
                         apply_rope(q[..., NOPE_DIM:], cos[:, None], sin[:, None])], axis=-1)
    return rmsnorm(q, g_qn), c, kr


def mla_keys(c, kr, w_uk, g_kn):
    k_nope = jnp.einsum('...tr,rhd->...thd', c, w_uk)
    k_rope = jnp.broadcast_to(kr[..., None, :], k_nope.shape[:-1] + (ROPE_DIM,)).astype(k_nope.dtype)
    return rmsnorm(jnp.concatenate([k_nope, k_rope], axis=-1), g_kn)


def mla_prompt_attend(q, c, kr, w_uk, w_uv, g_kn):
    b, l = q.shape[:2]
    k = mla_keys(c, kr, w_uk, g_kn)
    v = jnp.einsum('btr,rhd->bthd', c, w_uv)
    nb = l // Q_BLOCK
    qb = jnp.swapaxes(q.reshape(b, nb, Q_BLOCK, MLA_HEADS, QK_DIM), 0, 1)
    key_pos = jnp.arange(l)

    def block(args):
        qi, start = args
        s = jnp.einsum('bqhd,bkhd->bhqk', qi, k, preferred_element_type=jnp.float32) * ATTN_SCALE
        mask = (start + jnp.arange(Q_BLOCK))[:, None] >= key_pos[None, :]
        p = jax.nn.softmax(jnp.where(mask, s, -jnp.inf), axis=-1)
        return jnp.einsum('bhqk,bkhd->bqhd', p.astype(v.dtype), v)

    o = lax.map(block, (qb, jnp.arange(nb) * Q_BLOCK))
    return jnp.swapaxes(o, 0, 1).reshape(b, l, MLA_HEADS, V_DIM)


def mla_sample_attend(q, c_new, kr_new, cache_c, cache_kr, m, page_table, w_uk, w_uv, g_kn):
    s_len = q.shape[1]
    n_past = page_table.shape[1] * PAGE_SIZE
    mask = jnp.arange(n_past + s_len)[None, :] <= (n_past + jnp.arange(s_len))[:, None]

    def one_seq(args):
        pt, qs, cn, krn = args
        c_all = jnp.concatenate([cache_c[m, pt].reshape(n_past, KV_RANK).astype(cn.dtype), cn], axis=0)
        kr_all = jnp.concatenate([cache_kr[m, pt].reshape(n_past, ROPE_DIM).astype(krn.dtype), krn], axis=0)
        k = mla_keys(c_all, kr_all, w_uk, g_kn)
        s = jnp.einsum('qhd,thd->hqt', qs, k, preferred_element_type=jnp.float32) * ATTN_SCALE
        p = jax.nn.softmax(jnp.where(mask[None], s, -jnp.inf), axis=-1)
        o_lat = jnp.einsum('hqt,tr->qhr', p.astype(c_all.dtype), c_all)
        return jnp.einsum('qhr,rhd->qhd', o_lat, w_uv)

    return lax.map(one_seq, (page_table, q, c_new, kr_new))


def wkv_step(state, inp):
    r, w, k, v, a, b = inp
    sa = jnp.einsum('bhij,bhj->bhi', state, a)
    state = state * w[:, :, None, :] + sa[..., None] * b[:, :, None, :] + v[..., None] * k[:, :, None, :]
    return state, jnp.einsum('bhij,bhj->bhi', state, r)


def rwkv_time_mix(h, shift_prev, wkv0, v_first, mu, w_r, w_k, w_v, w_o, w0, w1, w2,
                  a0, a1, a2, g1, g2, k_k, k_a, r_k, lnx_w, lnx_b, vres):
    b, l, d = h.shape
    f32 = jnp.float32
    prev = jnp.concatenate([shift_prev[:, None, :].astype(h.dtype), h[:, :-1]], axis=1)
    xx = prev - h
    xr, xw, xk, xv, xa, xg = [h + xx * mu[i] for i in range(6)]
    r = xr @ w_r
    k = xk @ w_k
    v = xv @ w_v
    w_log = -jax.nn.softplus(-(w0 + jnp.tanh(xw @ w1) @ w2)) - 0.5
    decay = jnp.exp(-jnp.exp(w_log.astype(f32)))
    if vres is None:
        v_first = v
    else:
        v0, v1, v2 = vres
        v = v + (v_first - v) * jax.nn.sigmoid(v0 + (xv @ v1) @ v2)
    a = jax.nn.sigmoid(a0 + (xa @ a1) @ a2)
    g = jax.nn.sigmoid(xg @ g1) @ g2
    heads = lambda t: t.astype(f32).reshape(b, l, RWKV_HEADS, HEAD_SIZE)
    kk = heads(k * k_k)
    kk = kk / jnp.maximum(jnp.sqrt(jnp.sum(kk * kk, axis=-1, keepdims=True)), 1e-12)
    k = k * (1.0 + (a - 1.0) * k_a)
    rh, kh, vh, ah, wh = heads(r), heads(k), heads(v), heads(a), heads(decay)
    tm = lambda t: jnp.swapaxes(t, 0, 1)
    state, y = lax.scan(wkv_step, wkv0.astype(f32),
                        (tm(rh), tm(wh), tm(kh), tm(vh), tm(-kk), tm(kk * ah)))
    y = tm(y)
    mean = jnp.mean(y, axis=-1, keepdims=True)
    var = jnp.mean(jnp.square(y - mean), axis=-1, keepdims=True)
    y = ((y - mean) * lax.rsqrt(var + LNX_EPS)).reshape(b, l, d) * lnx_w.astype(f32) + lnx_b.astype(f32)
    bonus = jnp.sum(rh * kh * r_k.astype(f32).reshape(RWKV_HEADS, HEAD_SIZE), axis=-1, keepdims=True) * vh
    y = y + bonus.reshape(b, l, d)
    out = (y * g.astype(f32)).astype(h.dtype) @ w_o
    return out, state, h[:, -1], v_first


def setup_inputs(seed: int = 0) -> dict:
    key = jax.random.key(seed)
    ks = iter(jax.random.split(key, 80))
    f32 = jnp.float32

    def nrm(shape, scale):
        return jax.random.normal(next(ks), shape, f32) * scale

    def uni(shape, lo, hi):
        return jax.random.uniform(next(ks), shape, f32, minval=lo, maxval=hi)

    def gain(shape):
        return 1.0 + nrm(shape, 0.05)

    n_pages = PAST_LEN // PAGE_SIZE
    n_pool = (DEC_BATCH * n_pages * POOL_NUM) // POOL_DEN
    perm = jax.random.permutation(next(ks), n_pool)
    page_table = perm[:DEC_BATCH * n_pages].reshape(DEC_BATCH, n_pages).astype(jnp.int32)
    D = D_MODEL
    HQ = MLA_HEADS * QK_DIM
    HV = MLA_HEADS * V_DIM
    return {
        'x_prompt': nrm((BATCH, SEQ, D), 1.0),
        'x_sample': nrm((DEC_BATCH, DEC_SEQ, D), 1.0),
        'cache_kv_latent': nrm((N_MLA, n_pool, PAGE_SIZE, KV_RANK), 1.0),
        'cache_k_rope': nrm((N_MLA, n_pool, PAGE_SIZE, ROPE_DIM), 1.0),
        'state_wkv': nrm((N_RWKV, DEC_BATCH, RWKV_HEADS, HEAD_SIZE, HEAD_SIZE), 0.3),
        'state_shift': nrm((N_RWKV, DEC_BATCH, D), 1.0),
        'page_table': page_table,
        'ffn_norm': gain((DEPTH, 2, D)),
        'ffn_w_gate': nrm((DEPTH, 2, D, D_FF), D ** -0.5),
        'ffn_w_up': nrm((DEPTH, 2, D, D_FF), D ** -0.5),
        'ffn_w_down': nrm((DEPTH, 2, D_FF, D), D_FF ** -0.5),
        'mix_norm': gain((DEPTH, D)),
        'mla_w_down': nrm((N_MLA, D, Q_RANK + KV_RANK + ROPE_DIM), D ** -0.5),
        'mla_g_q_lat': gain((N_MLA, Q_RANK)),
        'mla_g_kv_lat': gain((N_MLA, KV_RANK)),
        'mla_w_uq': nrm((N_MLA, Q_RANK, HQ), Q_RANK ** -0.5),
        'mla_w_uk': nrm((N_MLA, KV_RANK, MLA_HEADS, NOPE_DIM), KV_RANK ** -0.5),
        'mla_w_uv': nrm((N_MLA, KV_RANK, MLA_HEADS, V_DIM), KV_RANK ** -0.5),
        'mla_g_qn': gain((N_MLA, QK_DIM)),
        'mla_g_kn': gain((N_MLA, QK_DIM)),
        'mla_w_o': nrm((N_MLA, HV, D), HV ** -0.5),
        'rwkv_mu': uni((N_RWKV, 6, D), 0.0, 1.0),
        'rwkv_w_r': nrm((N_RWKV, D, D), D ** -0.5),
        'rwkv_w_k': nrm((N_RWKV, D, D), D ** -0.5),
        'rwkv_w_v': nrm((N_RWKV, D, D), D ** -0.5),
        'rwkv_w_o': nrm((N_RWKV, D, D), D ** -0.5),
        'rwkv_w0': uni((N_RWKV, D), -6.5, -1.5),
        'rwkv_w1': nrm((N_RWKV, D, DECAY_LORA), D ** -0.5),
        'rwkv_w2': nrm((N_RWKV, DECAY_LORA, D), 0.1 * DECAY_LORA ** -0.5),
        'rwkv_a0': nrm((N_RWKV, D), 0.1),
        'rwkv_a1': nrm((N_RWKV, D, AAA_LORA), D ** -0.5),
        'rwkv_a2': nrm((N_RWKV, AAA_LORA, D), 0.1 * AAA_LORA ** -0.5),
        'rwkv_v0': nrm((N_RWKV - 1, D), 0.5),
        'rwkv_v1': nrm((N_RWKV - 1, D, MV_LORA), D ** -0.5),
        'rwkv_v2': nrm((N_RWKV - 1, MV_LORA, D), 0.1 * MV_LORA ** -0.5),
        'rwkv_g1': nrm((N_RWKV, D, GATE_LORA), D ** -0.5),
        'rwkv_g2': nrm((N_RWKV, GATE_LORA, D), GATE_LORA ** -0.5),
        'rwkv_k_k': 0.85 + nrm((N_RWKV, D), 0.05),
        'rwkv_k_a': gain((N_RWKV, D)),
        'rwkv_r_k': nrm((N_RWKV, D), 0.1),
        'rwkv_lnx_w': gain((N_RWKV, D)),
        'rwkv_lnx_b': nrm((N_RWKV, D), 0.02),
    }


def reference(x_prompt, x_sample, cache_kv_latent, cache_k_rope, state_wkv, state_shift, page_table,
              ffn_norm, ffn_w_gate, ffn_w_up, ffn_w_down, mix_norm,
              mla_w_down, mla_g_q_lat, mla_g_kv_lat, mla_w_uq, mla_w_uk, mla_w_uv, mla_g_qn, mla_g_kn, mla_w_o,
              rwkv_mu, rwkv_w_r, rwkv_w_k, rwkv_w_v, rwkv_w_o, rwkv_w0, rwkv_w1, rwkv_w2,
              rwkv_a0, rwkv_a1, rwkv_a2, rwkv_v0, rwkv_v1, rwkv_v2, rwkv_g1, rwkv_g2,
              rwkv_k_k, rwkv_k_a, rwkv_r_k, rwkv_lnx_w, rwkv_lnx_b):

    def trunk(x, pos, attend, shift0, wkv0):
        b, l, _ = x.shape
        lat_rows, rope_rows, wkv_out, shift_out = [], [], [], []
        v_first = None
        for i in range(DEPTH):
            h = rmsnorm(x, ffn_norm[i, 0])
            x = x + 0.5 * swiglu(h, ffn_w_gate[i, 0], ffn_w_up[i, 0], ffn_w_down[i, 0])
            h = rmsnorm(x, mix_norm[i])
            if i % 2 == 0:
                m = i // 2
                q, c, kr = mla_project(h, pos, mla_w_down[m], mla_g_q_lat[m], mla_g_kv_lat[m],
                                       mla_w_uq[m], mla_g_qn[m])
                o = attend(m, q, c, kr)
                x = x + o.reshape(b, l, MLA_HEADS * V_DIM) @ mla_w_o[m]
                lat_rows.append(c)
                rope_rows.append(kr)
            else:
                j = i // 2
                vres = None if j == 0 else (rwkv_v0[j - 1], rwkv_v1[j - 1], rwkv_v2[j - 1])
                o, st, sh, v_first = rwkv_time_mix(
                    h, shift0[j], wkv0[j], v_first, rwkv_mu[j], rwkv_w_r[j], rwkv_w_k[j], rwkv_w_v[j],
                    rwkv_w_o[j], rwkv_w0[j], rwkv_w1[j], rwkv_w2[j], rwkv_a0[j], rwkv_a1[j], rwkv_a2[j],
                    rwkv_g1[j], rwkv_g2[j], rwkv_k_k[j], rwkv_k_a[j], rwkv_r_k[j],
                    rwkv_lnx_w[j], rwkv_lnx_b[j], vres)
                x = x + o
                wkv_out.append(st)
                shift_out.append(sh)
            h = rmsnorm(x, ffn_norm[i, 1])
            x = x + 0.5 * swiglu(h, ffn_w_gate[i, 1], ffn_w_up[i, 1], ffn_w_down[i, 1])
        return x, jnp.stack(lat_rows), jnp.stack(rope_rows), jnp.stack(wkv_out), jnp.stack(shift_out)

    prompt_attend = lambda m, q, c, kr: mla_prompt_attend(q, c, kr, mla_w_uk[m], mla_w_uv[m], mla_g_kn[m])
    sample_attend = lambda m, q, c, kr: mla_sample_attend(q, c, kr, cache_kv_latent, cache_k_rope, m,
                                                          page_table, mla_w_uk[m], mla_w_uv[m], mla_g_kn[m])

    bp, lp = x_prompt.shape[:2]
    shift0_p = jnp.zeros((N_RWKV, bp, D_MODEL), x_prompt.dtype)
    wkv0_p = jnp.zeros((N_RWKV, bp, RWKV_HEADS, HEAD_SIZE, HEAD_SIZE), jnp.float32)
    y_prompt, p_lat, p_rope, p_wkv, p_shift = trunk(x_prompt, jnp.arange(lp), prompt_attend, shift0_p, wkv0_p)

    n_past = page_table.shape[1] * PAGE_SIZE
    pos_s = n_past + jnp.arange(x_sample.shape[1])
    y_sample, s_lat, s_rope, s_wkv, s_shift = trunk(x_sample, pos_s, sample_attend, state_shift, state_wkv)

    return (y_prompt, y_sample, p_lat, p_rope, p_wkv, p_shift, s_lat, s_rope, s_wkv, s_shift)
```

```python
import functools

import jax
import jax.numpy as jnp
from jax import lax
from jax.experimental import pallas as pl
from jax.experimental.pallas import tpu as pltpu

F32 = jnp.float32
BF16 = jnp.bfloat16

D_MODEL = 1024
D_FF = 2816
RMS_EPS = 1e-6
MLA_HEADS = 8
Q_RANK = 384
KV_RANK = 256
NOPE_DIM = 128
ROPE_DIM = 64
QK_DIM = NOPE_DIM + ROPE_DIM
QK_PAD = 256
V_DIM = 128
ROPE_THETA = 10000.0
ATTN_SCALE = QK_DIM ** -0.5
PAGE_SIZE = 128
HEAD_SIZE = 64
RWKV_HEADS = D_MODEL // HEAD_SIZE
HEAD_PAIRS = RWKV_HEADS // 2
PAIR_W = 2 * HEAD_SIZE
LNX_EPS = 64e-5
WKV_CHUNK = 64

VMEM_LIMIT = 48 * 1024 * 1024


def _cparams(sem):
    return pltpu.CompilerParams(dimension_semantics=sem, vmem_limit_bytes=VMEM_LIMIT)


def _pick(n, cands):
    for c in cands:
        if n % c == 0:
            return c
    raise ValueError(f"no tile in {cands} divides {n}")


def _rms(x, g):
    return x * lax.rsqrt(jnp.mean(x * x, axis=-1, keepdims=True) + RMS_EPS) * g


def _dot(a, b):
    return jnp.dot(a, b, preferred_element_type=F32)


def _dot_nt(a, b):
    return lax.dot_general(a, b, (((1,), (1,)), ((), ())), preferred_element_type=F32)


def _dot_tn(a, b):
    return lax.dot_general(a, b, (((0,), (0,)), ((), ())), preferred_element_type=F32)


def _ffn_kernel(x_ref, g_ref, wg_ref, wu_ref, wd_ref, o_ref, hb_ref):
    j = pl.program_id(1)

    @pl.when(j == 0)
    def _():
        x = x_ref[...]
        hb_ref[...] = _rms(x, g_ref[...]).astype(BF16)
        o_ref[...] = x

    hb = hb_ref[...]
    a = _dot(hb, wg_ref[...])
    u = _dot(hb, wu_ref[...])
    act = (0.5 * a * jax.nn.sigmoid(a) * u).astype(BF16)
    o_ref[...] += _dot(act, wd_ref[...])


def _ffn(x, g, wg, wu, wd):
    n = x.shape[0]
    tm = _pick(n, (512, 256, 128, 64, 32, 16, 8))
    tf = D_FF // 2
    return pl.pallas_call(
        _ffn_kernel,
        grid=(n // tm, D_FF // tf),
        in_specs=[
            pl.BlockSpec((tm, D_MODEL), lambda i, j: (i, 0)),
            pl.BlockSpec((1, D_MODEL), lambda i, j: (0, 0)),
            pl.BlockSpec((D_MODEL, tf), lambda i, j: (0, j)),
            pl.BlockSpec((D_MODEL, tf), lambda i, j: (0, j)),
            pl.BlockSpec((tf, D_MODEL), lambda i, j: (j, 0)),
        ],
        out_specs=pl.BlockSpec((tm, D_MODEL), lambda i, j: (i, 0)),
        out_shape=jax.ShapeDtypeStruct((n, D_MODEL), F32),
        scratch_shapes=[pltpu.VMEM((tm, D_MODEL), BF16)],
        compiler_params=_cparams(("parallel", "arbitrary")),
        name="ffn",
    )(x, g.reshape(1, D_MODEL), wg, wu, wd)


def _proj_res_kernel(x_ref, ap_ref, as_ref, w_ref, o_ref, *, npt):
    i = pl.program_id(0)

    @pl.when(i < npt)
    def _():
        o_ref[...] = x_ref[...] + _dot(ap_ref[...].astype(BF16), w_ref[...])

    @pl.when(i >= npt)
    def _():
        o_ref[...] = x_ref[...] + _dot(as_ref[...].astype(BF16), w_ref[...])


def _proj_res_gated_kernel(x_ref, ap_ref, as_ref, gp_ref, gs_ref, w_ref, o_ref, *, npt):
    i = pl.program_id(0)

    @pl.when(i < npt)
    def _():
        o_ref[...] = x_ref[...] + _dot((ap_ref[...] * gp_ref[...]).astype(BF16), w_ref[...])

    @pl.when(i >= npt)
    def _():
        o_ref[...] = x_ref[...] + _dot((as_ref[...] * gs_ref[...]).astype(BF16), w_ref[...])


def _proj_res(x, a_p, a_s, w, gate_p=None, gate_s=None):
    n, n_p, n_s = x.shape[0], a_p.shape[0], a_s.shape[0]
    assert n == n_p + n_s
    tm = _pick(n_s, (512, 256, 128, 64, 32, 16, 8))
    assert n_p % tm == 0
    npt = n_p // tm
    kin = a_p.shape[1]
    xspec = pl.BlockSpec((tm, D_MODEL), lambda i: (i, 0))
    pspec = pl.BlockSpec((tm, kin), lambda i: (jnp.minimum(i, npt - 1), 0))
    sspec = pl.BlockSpec((tm, kin), lambda i: (jnp.maximum(i - npt, 0), 0))
    wspec = pl.BlockSpec((kin, D_MODEL), lambda i: (0, 0))
    if gate_p is None:
        body, specs, args = _proj_res_kernel, [xspec, pspec, sspec, wspec], (x, a_p, a_s, w)
    else:
        body = _proj_res_gated_kernel
        specs = [xspec, pspec, sspec, pspec, sspec, wspec]
        args = (x, a_p, a_s, gate_p, gate_s, w)
    return pl.pallas_call(
        functools.partial(body, npt=npt),
        grid=(n // tm,),
        in_specs=specs,
        out_specs=xspec,
        out_shape=jax.ShapeDtypeStruct(x.shape, F32),
        compiler_params=_cparams(("parallel",)),
        name="proj_res",
    )(*args)


def _mla_proj_kernel(x_ref, g_ref, wd_ref, gq_ref, gkv_ref, wqa_ref, wqb_ref, wuk_ref, wuv_ref,
                     gqn_ref, gkn_ref, ct_ref, st_ref,
                     q_ref, c_ref, kr_ref, k_ref, v_ref):
    hb = _rms(x_ref[...], g_ref[...]).astype(BF16)
    lat = _dot(hb, wd_ref[...])
    cq = _rms(lat[:, :Q_RANK], gq_ref[...]).astype(BF16)
    c = _rms(lat[:, Q_RANK:Q_RANK + KV_RANK], gkv_ref[...])
    ct = ct_ref[...]
    st = st_ref[...]
    o = Q_RANK + KV_RANK
    kr = lat[:, o:o + 128] * ct + lat[:, o + 128:o + 256] * st
    c_ref[...] = c
    kr_ref[...] = kr[:, :ROPE_DIM]

    qa = _dot(cq, wqa_ref[...])
    qb = _dot(cq, wqb_ref[...])
    gqn = gqn_ref[...]
    gkn = gkn_ref[...]
    cb = c.astype(BF16)
    kn = _dot(cb, wuk_ref[...])
    vv = _dot(cb, wuv_ref[...])
    ssr = jnp.sum(kr * kr, axis=-1, keepdims=True)
    for h in range(MLA_HEADS):
        nope = qa[:, h * QK_PAD:h * QK_PAD + 128]
        rp = qa[:, h * QK_PAD + 128:(h + 1) * QK_PAD] * ct + qb[:, h * 128:(h + 1) * 128] * st
        ss = jnp.sum(nope * nope, axis=-1, keepdims=True) + jnp.sum(rp * rp, axis=-1, keepdims=True)
        rs = lax.rsqrt(ss * (1.0 / QK_DIM) + RMS_EPS)
        q_ref[h, :, 0:128] = (nope * rs * gqn[:, 0:128]).astype(BF16)
        q_ref[h, :, 128:256] = (rp * rs * gqn[:, 128:256]).astype(BF16)
        knh = kn[:, h * 128:(h + 1) * 128]
        rk = lax.rsqrt((jnp.sum(knh * knh, axis=-1, keepdims=True) + ssr) * (1.0 / QK_DIM) + RMS_EPS)
        k_ref[h, :, 0:128] = (knh * rk * gkn[:, 0:128]).astype(BF16)
        k_ref[h, :, 128:256] = (kr * rk * gkn[:, 128:256]).astype(BF16)
        v_ref[h] = vv[:, h * 128:(h + 1) * 128].astype(BF16)


def _rot_cols(w):
    half = ROPE_DIM // 2
    return jnp.concatenate([-w[..., half:], w[..., :half]], axis=-1)


def _pad_lanes(w, n):
    return jnp.pad(w, [(0, 0)] * (w.ndim - 1) + [(0, n - w.shape[-1])])


def _mla_weights(m, mla_w_down, mla_g_q_lat, mla_g_kv_lat, mla_w_uq, mla_w_uk, mla_w_uv, mla_g_qn, mla_g_kn):
    wd = mla_w_down[m]
    o = Q_RANK + KV_RANK
    wkr = wd[:, o:]
    wd_ext = jnp.concatenate([wd[:, :o], _pad_lanes(wkr, 128), _pad_lanes(_rot_cols(wkr), 128)], axis=1)
    wq = mla_w_uq[m].reshape(Q_RANK, MLA_HEADS, QK_DIM)
    wqa = _pad_lanes(wq, QK_PAD).reshape(Q_RANK, MLA_HEADS * QK_PAD)
    wqb = _pad_lanes(_rot_cols(wq[..., NOPE_DIM:]), 128).reshape(Q_RANK, MLA_HEADS * 128)
    return dict(
        wd=wd_ext.astype(BF16), gq=mla_g_q_lat[m].reshape(1, Q_RANK), gkv=mla_g_kv_lat[m].reshape(1, KV_RANK),
        wqa=wqa.astype(BF16), wqb=wqb.astype(BF16),
        wuk=mla_w_uk[m].reshape(KV_RANK, MLA_HEADS * NOPE_DIM).astype(BF16),
        wuv=mla_w_uv[m].reshape(KV_RANK, MLA_HEADS * V_DIM).astype(BF16),
        gqn=_pad_lanes(mla_g_qn[m], QK_PAD).reshape(1, QK_PAD),
        gkn=_pad_lanes(mla_g_kn[m], QK_PAD).reshape(1, QK_PAD),
    )


def _rope_tables(pos):
    inv = ROPE_THETA ** (-jnp.arange(0, ROPE_DIM, 2, dtype=F32) / ROPE_DIM)
    ang = pos.astype(F32)[:, None] * inv[None, :]
    cos, sin = jnp.cos(ang), jnp.sin(ang)
    return (_pad_lanes(jnp.concatenate([cos, cos], axis=-1), 128),
            _pad_lanes(jnp.concatenate([sin, sin], axis=-1), 128))


def _mla_proj(x, g, w, ctab, stab, tm, tab_index):
    n = x.shape[0]
    full = lambda shape: pl.BlockSpec(shape, lambda i: (0,) * len(shape))
    tspec = pl.BlockSpec((tm, 128), lambda i: (tab_index(i), 0))
    return pl.pallas_call(
        _mla_proj_kernel,
        grid=(n // tm,),
        in_specs=[
            pl.BlockSpec((tm, D_MODEL), lambda i: (i, 0)),
            full((1, D_MODEL)), full(w["wd"].shape), full((1, Q_RANK)), full((1, KV_RANK)),
            full(w["wqa"].shape), full(w["wqb"].shape), full(w["wuk"].shape), full(w["wuv"].shape),
            full((1, QK_PAD)), full((1, QK_PAD)), tspec, tspec,
        ],
        out_specs=[
            pl.BlockSpec((MLA_HEADS, tm, QK_PAD), lambda i: (0, i, 0)),
            pl.BlockSpec((tm, KV_RANK), lambda i: (i, 0)),
            pl.BlockSpec((tm, ROPE_DIM), lambda i: (i, 0)),
            pl.BlockSpec((MLA_HEADS, tm, QK_PAD), lambda i: (0, i, 0)),
            pl.BlockSpec((MLA_HEADS, tm, V_DIM), lambda i: (0, i, 0)),
        ],
        out_shape=[
            jax.ShapeDtypeStruct((MLA_HEADS, n, QK_PAD), BF16),
            jax.ShapeDtypeStruct((n, KV_RANK), F32),
            jax.ShapeDtypeStruct((n, ROPE_DIM), F32),
            jax.ShapeDtypeStruct((MLA_HEADS, n, QK_PAD), BF16),
            jax.ShapeDtypeStruct((MLA_HEADS, n, V_DIM), BF16),
        ],
        compiler_params=_cparams(("parallel",)),
        name="mla_proj",
    )(x, g.reshape(1, D_MODEL), w["wd"], w["gq"], w["gkv"], w["wqa"], w["wqb"], w["wuk"], w["wuv"],
      w["gqn"], w["gkn"], ctab, stab)


def _prompt_attn_kernel(q_ref, k_ref, v_ref, o_ref, *, tq):
    l = q_ref.shape[1]
    nq = l // tq
    row = lax.broadcasted_iota(jnp.int32, (tq, tq), 0)
    col = lax.broadcasted_iota(jnp.int32, (tq, tq), 1)
    diag_mask = row >= col
    for qi in range(nq):
        q = q_ref[0, qi * tq:(qi + 1) * tq, :]
        m = l_sum = acc = None
        for ki in range(qi + 1):
            k = k_ref[0, ki * tq:(ki + 1) * tq, :]
            v = v_ref[0, ki * tq:(ki + 1) * tq, :]
            s = _dot_nt(q, k) * ATTN_SCALE
            if ki == qi:
                s = jnp.where(diag_mask, s, -jnp.inf)
            m_blk = jnp.max(s, axis=-1, keepdims=True)
            if ki == 0:
                m = m_blk
                p = jnp.exp(s - m)
                l_sum = jnp.sum(p, axis=-1, keepdims=True)
                acc = _dot(p.astype(BF16), v)
            else:
                m_new = jnp.maximum(m, m_blk)
                alpha = jnp.exp(m - m_new)
                p = jnp.exp(s - m_new)
                l_sum = alpha * l_sum + jnp.sum(p, axis=-1, keepdims=True)
                acc = alpha * acc + _dot(p.astype(BF16), v)
                m = m_new
        o_ref[qi * tq:(qi + 1) * tq, :] = (acc / l_sum).astype(o_ref.dtype)


def _prompt_attn(q, k, v, nb, l):
    tq = _pick(l, (512, 256, 128))
    return pl.pallas_call(
        functools.partial(_prompt_attn_kernel, tq=tq),
        grid=(nb, MLA_HEADS),
        in_specs=[
            pl.BlockSpec((1, l, QK_PAD), lambda b, h: (h, b, 0)),
            pl.BlockSpec((1, l, QK_PAD), lambda b, h: (h, b, 0)),
            pl.BlockSpec((1, l, V_DIM), lambda b, h: (h, b, 0)),
        ],
        out_specs=pl.BlockSpec((l, V_DIM), lambda b, h: (b, h)),
        out_shape=jax.ShapeDtypeStruct((nb * l, MLA_HEADS * V_DIM), BF16),
        compiler_params=_cparams(("parallel", "parallel")),
        name="prompt_attn",
    )(q, k, v)


def _sample_attn_kernel(pt_ref, q_ref, cn_ref, krn_ref, wukt_ref, wuk_ref, wuv_ref, gkn_ref, *rest, pg, tc):
    cpages = rest[:pg]
    kpages = rest[pg:2 * pg]
    o_ref = rest[2 * pg]
    qt_ref, qr_ref, m_ref, l_ref, acc_ref, cb_ref, krf_ref = rest[2 * pg + 1:]
    del pt_ref
    g = pl.program_id(1)
    ng = pl.num_programs(1)
    nq = q_ref.shape[1] // MLA_HEADS
    rows = q_ref.shape[1]

    @pl.when(g == 0)
    def _():
        gkn = gkn_ref[...]
        for h in range(MLA_HEADS):
            qh = q_ref[0, h * nq:(h + 1) * nq, :].astype(F32)
            qg = (qh[:, :NOPE_DIM] * gkn[:, :NOPE_DIM]).astype(BF16)
            qt_ref[h * nq:(h + 1) * nq, :] = _dot_nt(qg, wuk_ref[:, h * NOPE_DIM:(h + 1) * NOPE_DIM])
            qr_ref[h * nq:(h + 1) * nq, :] = qh[:, NOPE_DIM:QK_DIM] * gkn[:, NOPE_DIM:QK_DIM]
        m_ref[...] = jnp.full(m_ref.shape, -jnp.inf, F32)
        l_ref[...] = jnp.zeros(l_ref.shape, F32)
        acc_ref[...] = jnp.zeros(acc_ref.shape, F32)

    ones = jnp.ones((8, ROPE_DIM), BF16)

    def process(cbc, krc, mask):
        t = cbc.shape[0]
        kt = _dot_nt(wukt_ref[...], cbc)
        ssr = _dot_nt(ones, (krc * krc).astype(BF16))[0:1]
        rs = []
        for h in range(MLA_HEADS):
            kh = kt[h * NOPE_DIM:(h + 1) * NOPE_DIM]
            ssq = jnp.sum(kh * kh, axis=0, keepdims=True)
            rs_h = lax.rsqrt((ssq + ssr) * (1.0 / QK_DIM) + RMS_EPS)
            rs.append(jnp.broadcast_to(rs_h, (nq, t)))
        rs = jnp.concatenate(rs, axis=0)
        s = _dot_nt(qt_ref[...].astype(BF16), cbc) + _dot_nt(qr_ref[...].astype(BF16), krc.astype(BF16))
        s = s * ATTN_SCALE * rs
        if mask is not None:
            s = jnp.where(mask, s, -jnp.inf)
        m_old = m_ref[...]
        m_new = jnp.maximum(m_old, jnp.max(s, axis=-1, keepdims=True))
        alpha = jnp.exp(m_old - m_new)
        p = jnp.exp(s - m_new)
        l_ref[...] = alpha * l_ref[...] + jnp.sum(p, axis=-1, keepdims=True)
        acc_ref[...] = alpha * acc_ref[...] + _dot(p.astype(BF16), cbc)
        m_ref[...] = m_new

    for j in range(pg):
        cb_ref[j * PAGE_SIZE:(j + 1) * PAGE_SIZE, :] = cpages[j][0, 0].astype(BF16)
        krf_ref[j * PAGE_SIZE:(j + 1) * PAGE_SIZE, :] = kpages[j][0, 0]
    for ci in range(pg * PAGE_SIZE // tc):
        process(cb_ref[ci * tc:(ci + 1) * tc, :], krf_ref[ci * tc:(ci + 1) * tc, :], None)

    @pl.when(g == ng - 1)
    def _():
        pad = PAGE_SIZE - nq
        cn = jnp.concatenate([cn_ref[...], jnp.zeros((pad, KV_RANK), F32)], axis=0).astype(BF16)
        krn = jnp.concatenate([krn_ref[...], jnp.zeros((pad, ROPE_DIM), F32)], axis=0)
        qpos = lax.broadcasted_iota(jnp.int32, (rows, PAGE_SIZE), 0) % nq
        tok = lax.broadcasted_iota(jnp.int32, (rows, PAGE_SIZE), 1)
        process(cn, krn, tok <= qpos)
        o_lat = (acc_ref[...] / l_ref[...]).astype(BF16)
        for h in range(MLA_HEADS):
            o_ref[:, h * V_DIM:(h + 1) * V_DIM] = _dot(o_lat[h * nq:(h + 1) * nq], wuv_ref[:, h * V_DIM:(h + 1) * V_DIM])


def _sample_attn(q_s, c_all, kr_all, row0, cache_c, cache_kr, m, page_table, wukt, wuk, wuv, gkn):
    ns, n_pages = page_table.shape
    rows = q_s.shape[1]
    nq = rows // MLA_HEADS
    assert row0 % nq == 0 and nq % 8 == 0
    pg = min(16, n_pages)
    assert n_pages % pg == 0 and pg % 2 == 0
    tc = 2 * PAGE_SIZE
    blk0 = row0 // nq
    full = lambda shape: pl.BlockSpec(shape, lambda b, g, pt: (0,) * len(shape))

    def page_spec(j, width):
        return pl.BlockSpec((1, 1, PAGE_SIZE, width), lambda b, g, pt: (m, pt[b * n_pages + g * pg + j], 0, 0))

    grid_spec = pltpu.PrefetchScalarGridSpec(
        num_scalar_prefetch=1,
        grid=(ns, n_pages // pg),
        in_specs=[
            pl.BlockSpec((1, rows, QK_PAD), lambda b, g, pt: (b, 0, 0)),
            pl.BlockSpec((nq, KV_RANK), lambda b, g, pt: (blk0 + b, 0)),
            pl.BlockSpec((nq, ROPE_DIM), lambda b, g, pt: (blk0 + b, 0)),
            full(wukt.shape), full(wuk.shape), full(wuv.shape), full((1, QK_PAD)),
        ] + [page_spec(j, KV_RANK) for j in range(pg)] + [page_spec(j, ROPE_DIM) for j in range(pg)],
        out_specs=pl.BlockSpec((nq, MLA_HEADS * V_DIM), lambda b, g, pt: (b, 0)),
        scratch_shapes=[
            pltpu.VMEM((rows, KV_RANK), F32), pltpu.VMEM((rows, ROPE_DIM), F32),
            pltpu.VMEM((rows, 1), F32), pltpu.VMEM((rows, 1), F32), pltpu.VMEM((rows, KV_RANK), F32),
            pltpu.VMEM((pg * PAGE_SIZE, KV_RANK), BF16), pltpu.VMEM((pg * PAGE_SIZE, ROPE_DIM), F32),
        ],
    )
    return pl.pallas_call(
        functools.partial(_sample_attn_kernel, pg=pg, tc=tc),
        grid_spec=grid_spec,
        out_shape=jax.ShapeDtypeStruct((ns * nq, MLA_HEADS * V_DIM), F32),
        compiler_params=_cparams(("arbitrary", "arbitrary")),
        name="sample_attn",
    )(page_table.reshape(-1), q_s, c_all, kr_all, wukt, wuk, wuv, gkn,
      *([cache_c] * pg), *([cache_kr] * pg))


def _softplus(y):
    return jnp.maximum(y, 0.0) + jnp.log(1.0 + jnp.exp(-jnp.abs(y)))


def _rwkv_proj_kernel(*refs, sample, has_vres, tiles_per_seq, nq):
    it = iter(refs)
    x_ref, g_ref, mu_ref = next(it), next(it), next(it)
    shift_ref = next(it) if sample else None
    wr, wk, wv, w0, w1, w2, a0, a1, a2, g1, g2 = (next(it) for _ in range(11))
    if has_vres:
        v0, v1, v2, vf_ref = (next(it) for _ in range(4))
    r_ref, k_ref, v_ref, lw_ref, alr_ref, gate_ref, hl_ref = (next(it) for _ in range(7))
    carry_ref = next(it)
    tm = x_ref.shape[0]
    i = pl.program_id(0)

    h = _rms(x_ref[...], g_ref[...])
    rolled = pltpu.roll(h, 1, axis=0)
    row = lax.broadcasted_iota(jnp.int32, (tm, 1), 0)
    if sample:
        prev = jnp.where(row % nq == 0, shift_ref[...], rolled)
        hl_ref[...] = h
    else:
        @pl.when(i == 0)
        def _():
            carry_ref[...] = jnp.zeros(carry_ref.shape, F32)

        first = jnp.where(i % tiles_per_seq == 0, 0.0, carry_ref[0:1, :])
        prev = jnp.where(row == 0, first, rolled)
        carry_ref[0:1, :] = h[tm - 1:tm, :]
        hl_ref[...] = h[tm - 8:tm, :]
    xx = prev - h
    mu = mu_ref[...]
    mix = lambda n: (h + xx * mu[n:n + 1, :]).astype(BF16)
    xr, xw, xk, xv, xa, xg = (mix(n) for n in range(6))
    r = _dot(xr, wr[...])
    k = _dot(xk, wk[...])
    v = _dot(xv, wv[...])
    wl = w0[...] + _dot(jnp.tanh(_dot(xw, w1[...])).astype(BF16), w2[...])
    lw = -jnp.exp(-_softplus(-wl) - 0.5)
    alr = jax.nn.sigmoid(a0[...] + _dot(_dot(xa, a1[...]).astype(BF16), a2[...]))
    gate_ref[...] = _dot(jax.nn.sigmoid(_dot(xg, g1[...])).astype(BF16), g2[...])
    if has_vres:
        vgate = jax.nn.sigmoid(v0[...] + _dot(_dot(xv, v1[...]).astype(BF16), v2[...]))
    for p in range(HEAD_PAIRS):
        sl = slice(p * PAIR_W, (p + 1) * PAIR_W)
        vp = v[:, sl]
        if has_vres:
            vp = vp + (vf_ref[p] - vp) * vgate[:, sl]
        r_ref[p] = r[:, sl]
        k_ref[p] = k[:, sl]
        v_ref[p] = vp
        lw_ref[p] = lw[:, sl]
        alr_ref[p] = alr[:, sl]


def _rwkv_weights(j, rwkv_mu, rwkv_w_r, rwkv_w_k, rwkv_w_v, rwkv_w0, rwkv_w1, rwkv_w2, rwkv_a0, rwkv_a1,
                  rwkv_a2, rwkv_v0, rwkv_v1, rwkv_v2, rwkv_g1, rwkv_g2):
    row = lambda a: a.reshape(1, D_MODEL)
    w = [rwkv_w_r[j].astype(BF16), rwkv_w_k[j].astype(BF16), rwkv_w_v[j].astype(BF16),
         row(rwkv_w0[j]), rwkv_w1[j].astype(BF16), rwkv_w2[j].astype(BF16),
         row(rwkv_a0[j]), rwkv_a1[j].astype(BF16), rwkv_a2[j].astype(BF16),
         rwkv_g1[j].astype(BF16), rwkv_g2[j].astype(BF16)]
    if j > 0:
        w += [row(rwkv_v0[j - 1]), rwkv_v1[j - 1].astype(BF16), rwkv_v2[j - 1].astype(BF16)]
    return rwkv_mu[j], w


def _rwkv_proj(x, row0, n, g, mu, weights, v_first, shift_exp, seq_len):
    sample = shift_exp is not None
    has_vres = v_first is not None
    tm = _pick(n if sample else seq_len, (256, 128, 64, 32, 16, 8))
    assert row0 % tm == 0 and n % tm == 0
    off = row0 // tm
    if sample:
        assert tm % seq_len == 0
        tiles_per_seq = 1
    else:
        assert seq_len % tm == 0
        tiles_per_seq = seq_len // tm
    full = lambda a: pl.BlockSpec(a.shape, lambda i: (0,) * a.ndim)
    pair_spec = pl.BlockSpec((HEAD_PAIRS, tm, PAIR_W), lambda i: (0, i, 0))
    row_spec = pl.BlockSpec((tm, D_MODEL), lambda i: (i, 0))
    g2d = g.reshape(1, D_MODEL)
    args = [x, g2d, mu]
    specs = [pl.BlockSpec((tm, D_MODEL), lambda i: (i + off, 0)), full(g2d), full(mu)]
    if sample:
        args.append(shift_exp)
        specs.append(row_spec)
    args += weights
    specs += [full(a) for a in weights]
    if has_vres:
        args.append(v_first)
        specs.append(pair_spec)
    pair_shape = jax.ShapeDtypeStruct((HEAD_PAIRS, n, PAIR_W), F32)
    if sample:
        hl_spec, hl_shape = row_spec, jax.ShapeDtypeStruct((n, D_MODEL), F32)
    else:
        hl_spec = pl.BlockSpec((8, D_MODEL), lambda i: (i, 0))
        hl_shape = jax.ShapeDtypeStruct((n // tm * 8, D_MODEL), F32)
    return pl.pallas_call(
        functools.partial(_rwkv_proj_kernel, sample=sample, has_vres=has_vres,
                          tiles_per_seq=tiles_per_seq, nq=seq_len),
        grid=(n // tm,),
        in_specs=specs,
        out_specs=[pair_spec] * 5 + [row_spec, hl_spec],
        out_shape=[pair_shape] * 5 + [jax.ShapeDtypeStruct((n, D_MODEL), F32), hl_shape],
        scratch_shapes=[pltpu.VMEM((8, D_MODEL), F32)],
        compiler_params=_cparams(("arbitrary",)),
        name="rwkv_proj",
    )(*args)


def _wkv_masks(c):
    c2 = 2 * c
    ri = lax.broadcasted_iota(jnp.int32, (c2, c2), 0)
    cj = lax.broadcasted_iota(jnp.int32, (c2, c2), 1)
    same = (ri // c) == (cj // c)
    t, s = ri % c, cj % c
    levels = []
    step = 1
    while step < c:
        levels.append(same & ((t ^ s) < 2 * step) & ((t & step) != 0) & ((s & step) == 0))
        step *= 2
    ti = lax.broadcasted_iota(jnp.int32, (c, c), 0)
    tj = lax.broadcasted_iota(jnp.int32, (c, c), 1)
    lane = lax.broadcasted_iota(jnp.int32, (1, PAIR_W), 1)
    row = lax.broadcasted_iota(jnp.int32, (c2, 1), 0)
    ei = lax.broadcasted_iota(jnp.int32, (PAIR_W, PAIR_W), 0)
    ej = lax.broadcasted_iota(jnp.int32, (PAIR_W, PAIR_W), 1)
    return dict(
        strict=same & (t > s), incl=same & (t >= s), levels=levels,
        tri=jnp.where(ti >= tj, 1.0, 0.0).astype(BF16),
        m0=lane < HEAD_SIZE, own=(row < c) == (lane < HEAD_SIZE), eye=ei == ej,
    )


def _wkv_chunk(r, k, v, lw, alr, kkp, kap, rkp, lnw, lnb, hbd, msk):
    c = r.shape[0]
    c2 = 2 * c
    m0 = msk["m0"]
    b16 = lambda a: a.astype(BF16)

    def stack(a):
        return jnp.concatenate([jnp.where(m0, a, 0.0), jnp.where(m0, 0.0, a)], axis=0)

    def tile2(a):
        return jnp.concatenate([a, a], axis=0)

    hi = b16(lw)
    r1 = lw - hi.astype(F32)
    mid = b16(r1)
    lo = b16(r1 - mid.astype(F32))
    tri = msk["tri"]
    cum = _dot(tri, hi) + _dot(tri, mid) + _dot(tri, lo)
    cum_last = cum[c - 1:c, :]
    e_in, e_ex = tile2(jnp.exp(cum)), tile2(jnp.exp(cum - lw))
    e_inv, e_end = tile2(jnp.exp(-cum)), tile2(jnp.exp(cum_last - cum))

    kk = stack(k * kkp)
    kk = kk / jnp.maximum(jnp.sqrt(jnp.sum(kk * kk, axis=-1, keepdims=True)), 1e-12)
    kmod = stack(k * (1.0 + (alr - 1.0) * kap))
    r_s, v_s = stack(r), stack(v)
    bb = kk * stack(alr)
    at, rt = -kk * e_ex, r_s * e_in
    bt, kt = bb * e_inv, kmod * e_inv
    bh, kh = bb * e_end, kmod * e_end
    v16 = b16(v_s)

    gram = _dot_nt(b16(jnp.concatenate([at, rt], axis=0)), b16(jnp.concatenate([bt, kt], axis=0)))
    ab = jnp.where(msk["strict"], gram[:c2, :c2], 0.0)
    ak = jnp.where(msk["strict"], gram[:c2, c2:], 0.0)
    rb = jnp.where(msk["incl"], gram[c2:, :c2], 0.0)
    rk = jnp.where(msk["incl"], gram[c2:, c2:], 0.0)

    levels = msk["levels"]
    e = jnp.where(levels[0], ab, 0.0)
    for lvl in levels[1:]:
        low = jnp.where(lvl, ab, 0.0)
        x = low + _dot(b16(e), b16(low))
        e = e + x + _dot(b16(x), b16(e))

    w2 = jnp.concatenate([_dot(b16(ak), v16), at], axis=1)
    u2 = w2 + _dot(b16(e), b16(w2))
    y2 = _dot(b16(rb), b16(u2)) + jnp.concatenate([_dot(b16(rk), v16), rt], axis=1)
    mj = _dot_tn(b16(bh), b16(u2))
    jmat = mj[:, :PAIR_W] + _dot_tn(b16(kh), v16)
    gcol = jnp.sum(jnp.where(msk["eye"], jnp.exp(cum_last), 0.0), axis=1, keepdims=True)
    h16 = b16(hbd)
    y_st = y2[:, :PAIR_W] + _dot(b16(y2[:, PAIR_W:]), h16)
    h_new = gcol * hbd + _dot(b16(mj[:, PAIR_W:]), h16) + jmat

    own = msk["own"]
    inv_n = 1.0 / HEAD_SIZE
    mean = jnp.sum(y_st, axis=-1, keepdims=True) * inv_n
    d = jnp.where(own, y_st - mean, 0.0)
    var = jnp.sum(d * d, axis=-1, keepdims=True) * inv_n
    bonus = jnp.sum(r_s * kmod * rkp, axis=-1, keepdims=True) * v_s
    z_st = d * lax.rsqrt(var + LNX_EPS) * lnw + jnp.where(own, lnb, 0.0) + bonus
    return z_st[:c] + z_st[c:], h_new


def _wkv_kernel(r_ref, k_ref, v_ref, lw_ref, a_ref, kk_ref, ka_ref, rk_ref, lnw_ref, lnb_ref, h0_ref,
                z_ref, hout_ref, h_ref, *, c, cpb):
    cb = pl.program_id(2)

    @pl.when(cb == 0)
    def _():
        h_ref[...] = h0_ref[0]

    msk = _wkv_masks(c)
    hbd = h_ref[...]
    for ci in range(cpb):
        sl = slice(ci * c, (ci + 1) * c)
        z, hbd = _wkv_chunk(r_ref[0, sl, :], k_ref[0, sl, :], v_ref[0, sl, :], lw_ref[0, sl, :], a_ref[0, sl, :],
                            kk_ref[0], ka_ref[0], rk_ref[0], lnw_ref[0], lnb_ref[0], hbd, msk)
        z_ref[sl, :] = z
    h_ref[...] = hbd

    @pl.when(cb == pl.num_programs(2) - 1)
    def _():
        hout_ref[0] = hbd


def _pair_rows(a):
    return a.reshape(HEAD_PAIRS, 1, PAIR_W)


def _wkv(r, k, v, lw, alr, params, h0, n_seq, seq_len):
    c = WKV_CHUNK
    assert seq_len % c == 0
    cpb = _pick(seq_len // c, (4, 2, 1))
    tb = c * cpb
    nblk = seq_len // tb
    dspec = pl.BlockSpec((1, tb, PAIR_W), lambda s, p, b: (p, s * nblk + b, 0))
    pspec = pl.BlockSpec((1, 1, PAIR_W), lambda s, p, b: (p, 0, 0))
    hspec = pl.BlockSpec((1, PAIR_W, PAIR_W), lambda s, p, b: (s * HEAD_PAIRS + p, 0, 0))
    return pl.pallas_call(
        functools.partial(_wkv_kernel, c=c, cpb=cpb),
        grid=(n_seq, HEAD_PAIRS, nblk),
        in_specs=[dspec] * 5 + [pspec] * 5 + [hspec],
        out_specs=[pl.BlockSpec((tb, PAIR_W), lambda s, p, b: (s * nblk + b, p)), hspec],
        out_shape=[jax.ShapeDtypeStruct((n_seq * seq_len, D_MODEL), F32),
                   jax.ShapeDtypeStruct(h0.shape, F32)],
        scratch_shapes=[pltpu.VMEM((PAIR_W, PAIR_W), F32)],
        compiler_params=_cparams(("parallel", "parallel", "arbitrary")),
        name="wkv",
    )(r, k, v, lw, alr, *[_pair_rows(p) for p in params], h0)


def _state_to_blockdiag(s):
    n = s.shape[0]
    ht = jnp.swapaxes(s, -1, -2).reshape(n, HEAD_PAIRS, 2, HEAD_SIZE, HEAD_SIZE)
    z = jnp.zeros_like(ht[:, :, 0])
    top = jnp.concatenate([ht[:, :, 0], z], axis=-1)
    bot = jnp.concatenate([z, ht[:, :, 1]], axis=-1)
    return jnp.concatenate([top, bot], axis=-2).reshape(n * HEAD_PAIRS, PAIR_W, PAIR_W)


def _blockdiag_to_state(hbd, n):
    hb = hbd.reshape(n, HEAD_PAIRS, 2, HEAD_SIZE, 2, HEAD_SIZE)
    ht = jnp.stack([hb[:, :, 0, :, 0, :], hb[:, :, 1, :, 1, :]], axis=2)
    return jnp.swapaxes(ht.reshape(n, RWKV_HEADS, HEAD_SIZE, HEAD_SIZE), -1, -2)


def _pad_steps(a, n_seq, seq_len, padded):
    a = a.reshape(HEAD_PAIRS, n_seq, seq_len, PAIR_W)
    a = jnp.pad(a, ((0, 0), (0, 0), (0, padded - seq_len), (0, 0)))
    return a.reshape(HEAD_PAIRS, n_seq * padded, PAIR_W)


def kernel(x_prompt, x_sample, cache_kv_latent, cache_k_rope, state_wkv, state_shift, page_table,
           ffn_norm, ffn_w_gate, ffn_w_up, ffn_w_down, mix_norm,
           mla_w_down, mla_g_q_lat, mla_g_kv_lat, mla_w_uq, mla_w_uk, mla_w_uv, mla_g_qn, mla_g_kn, mla_w_o,
           rwkv_mu, rwkv_w_r, rwkv_w_k, rwkv_w_v, rwkv_w_o, rwkv_w0, rwkv_w1, rwkv_w2,
           rwkv_a0, rwkv_a1, rwkv_a2, rwkv_v0, rwkv_v1, rwkv_v2, rwkv_g1, rwkv_g2,
           rwkv_k_k, rwkv_k_a, rwkv_r_k, rwkv_lnx_w, rwkv_lnx_b):
    nb, l, _ = x_prompt.shape
    ns, nq, _ = x_sample.shape
    n_p, n_s = nb * l, ns * nq
    depth = ffn_norm.shape[0]
    past = page_table.shape[1] * PAGE_SIZE
    x = jnp.concatenate([x_prompt.reshape(n_p, D_MODEL), x_sample.reshape(n_s, D_MODEL)], axis=0)

    tm = _pick(n_s, (512, 256, 128, 64, 32, 16, 8))
    assert l % tm == 0 and tm % nq == 0
    cos_p, sin_p = _rope_tables(jnp.arange(l))
    cos_s, sin_s = _rope_tables(past + jnp.arange(nq))
    ctab = jnp.concatenate([cos_p, jnp.tile(cos_s, (tm // nq, 1))], axis=0)
    stab = jnp.concatenate([sin_p, jnp.tile(sin_s, (tm // nq, 1))], axis=0)
    n_ptiles, tiles_per_seq = n_p // tm, l // tm
    tab_index = lambda i: jnp.where(i < n_ptiles, i % tiles_per_seq, tiles_per_seq)

    lat_rows, rope_rows, p_wkv, p_shift, s_wkv, s_shift = [], [], [], [], [], []
    vf_p = vf_s = None
    for i in range(depth):
        x = _ffn(x, ffn_norm[i, 0], ffn_w_gate[i, 0].astype(BF16), ffn_w_up[i, 0].astype(BF16),
                 ffn_w_down[i, 0].astype(BF16))
        if i % 2 == 0:
            m = i // 2
            w = _mla_weights(m, mla_w_down, mla_g_q_lat, mla_g_kv_lat, mla_w_uq, mla_w_uk, mla_w_uv,
                             mla_g_qn, mla_g_kn)
            q, c, kr, k, v = _mla_proj(x, mix_norm[i], w, ctab, stab, tm, tab_index)
            o_p = _prompt_attn(q, k, v, nb, l)
            q_s = jnp.transpose(q[:, n_p:, :].reshape(MLA_HEADS, ns, nq, QK_PAD), (1, 0, 2, 3))
            q_s = q_s.reshape(ns, MLA_HEADS * nq, QK_PAD)
            o_s = _sample_attn(q_s, c, kr, n_p, cache_kv_latent, cache_k_rope, m, page_table,
                               jnp.transpose(w["wuk"]), w["wuk"], w["wuv"], w["gkn"])
            x = _proj_res(x, o_p, o_s, mla_w_o[m].astype(BF16))
            lat_rows.append(c)
            rope_rows.append(kr)
        else:
            j = i // 2
            mu, w = _rwkv_weights(j, rwkv_mu, rwkv_w_r, rwkv_w_k, rwkv_w_v, rwkv_w0, rwkv_w1, rwkv_w2,
                                  rwkv_a0, rwkv_a1, rwkv_a2, rwkv_v0, rwkv_v1, rwkv_v2, rwkv_g1, rwkv_g2)
            params = [rwkv_k_k[j], rwkv_k_a[j], rwkv_r_k[j], rwkv_lnx_w[j], rwkv_lnx_b[j]]
            r, k, v, lw, alr, gate_p, hl = _rwkv_proj(x, 0, n_p, mix_norm[i], mu, w, vf_p, None, l)
            if j == 0:
                vf_p = v
            z_p, h_p = _wkv(r, k, v, lw, alr, params, jnp.zeros((nb * HEAD_PAIRS, PAIR_W, PAIR_W), F32), nb, l)
            p_wkv.append(_blockdiag_to_state(h_p, nb))
            p_shift.append(hl.reshape(nb, -1, 8, D_MODEL)[:, -1, 7])
            shift_exp = jnp.zeros((ns, nq, D_MODEL), F32).at[:, 0, :].set(state_shift[j]).reshape(n_s, D_MODEL)
            r, k, v, lw, alr, gate_s, h_s = _rwkv_proj(x, n_p, n_s, mix_norm[i], mu, w, vf_s, shift_exp, nq)
            if j == 0:
                vf_s = v
            pad = lambda a: _pad_steps(a, ns, nq, WKV_CHUNK)
            z_s, hs = _wkv(pad(r), pad(k), pad(v), pad(lw), pad(alr), params,
                           _state_to_blockdiag(state_wkv[j]), ns, WKV_CHUNK)
            z_s = z_s.reshape(ns, WKV_CHUNK, D_MODEL)[:, :nq].reshape(n_s, D_MODEL)
            s_wkv.append(_blockdiag_to_state(hs, ns))
            s_shift.append(h_s.reshape(ns, nq, D_MODEL)[:, -1])
            x = _proj_res(x, z_p, z_s, rwkv_w_o[j].astype(BF16), gate_p, gate_s)
        x = _ffn(x, ffn_norm[i, 1], ffn_w_gate[i, 1].astype(BF16), ffn_w_up[i, 1].astype(BF16),
                 ffn_w_down[i, 1].astype(BF16))

    lat = jnp.stack(lat_rows)
    rope = jnp.stack(rope_rows)
    return (x[:n_p].reshape(nb, l, D_MODEL), x[n_p:].reshape(ns, nq, D_MODEL),
            lat[:, :n_p].reshape(-1, nb, l, KV_RANK), rope[:, :n_p].reshape(-1, nb, l, ROPE_DIM),
            jnp.stack(p_wkv), jnp.stack(p_shift),
            lat[:, n_p:].reshape(-1, ns, nq, KV_RANK), rope[:, n_p:].reshape(-1, ns, nq, ROPE_DIM),
            jnp.stack(s_wkv), jnp.stack(s_shift))
```

```python
import functools

import jax
import jax.numpy as jnp
from jax import lax
from jax.experimental import pallas as pl
from jax.experimental.pallas import tpu as pltpu

F32 = jnp.float32
BF16 = jnp.bfloat16

D_MODEL = 1024
D_FF = 2816
RMS_EPS = 1e-6
MLA_HEADS = 8
Q_RANK = 384
KV_RANK = 256
NOPE_DIM = 128
ROPE_DIM = 64
QK_DIM = NOPE_DIM + ROPE_DIM
QK_PAD = 256
V_DIM = 128
ROPE_THETA = 10000.0
ATTN_SCALE = QK_DIM ** -0.5
PAGE_SIZE = 128
HEAD_SIZE = 64
RWKV_HEADS = D_MODEL // HEAD_SIZE
HEAD_PAIRS = RWKV_HEADS // 2
PAIR_W = 2 * HEAD_SIZE
LNX_EPS = 64e-5
WKV_CHUNK = 64

VMEM_LIMIT = 48 * 1024 * 1024


def _cparams(sem):
    return pltpu.CompilerParams(dimension_semantics=sem, vmem_limit_bytes=VMEM_LIMIT)


def _pick(n, cands):
    for c in cands:
        if n % c == 0:
            return c
    raise ValueError(f"no tile in {cands} divides {n}")


def _rms(x, g):
    return x * lax.rsqrt(jnp.mean(x * x, axis=-1, keepdims=True) + RMS_EPS) * g


def _dot(a, b):
    return jnp.dot(a, b, preferred_element_type=F32)


def _dot_nt(a, b):
    return lax.dot_general(a, b, (((1,), (1,)), ((), ())), preferred_element_type=F32)


def _dot_tn(a, b):
    return lax.dot_general(a, b, (((0,), (0,)), ((), ())), preferred_element_type=F32)


def _ffn_kernel(x_ref, g_ref, wg_ref, wu_ref, wd_ref, o_ref, hb_ref):
    j = pl.program_id(1)

    @pl.when(j == 0)
    def _():
        x = x_ref[...]
        hb_ref[...] = _rms(x, g_ref[...]).astype(BF16)
        o_ref[...] = x

    hb = hb_ref[...]
    a = _dot(hb, wg_ref[...])
    u = _dot(hb, wu_ref[...])
    act = (0.5 * a * jax.nn.sigmoid(a) * u).astype(BF16)
    o_ref[...] += _dot(act, wd_ref[...])


def _ffn(x, g, wg, wu, wd):
    n = x.shape[0]
    tm = _pick(n, (512, 256, 128, 64, 32, 16, 8))
    tf = D_FF // 2
    return pl.pallas_call(
        _ffn_kernel,
        grid=(n // tm, D_FF // tf),
        in_specs=[
            pl.BlockSpec((tm, D_MODEL), lambda i, j: (i, 0)),
            pl.BlockSpec((1, D_MODEL), lambda i, j: (0, 0)),
            pl.BlockSpec((D_MODEL, tf), lambda i, j: (0, j)),
            pl.BlockSpec((D_MODEL, tf), lambda i, j: (0, j)),
            pl.BlockSpec((tf, D_MODEL), lambda i, j: (j, 0)),
        ],
        out_specs=pl.BlockSpec((tm, D_MODEL), lambda i, j: (i, 0)),
        out_shape=jax.ShapeDtypeStruct((n, D_MODEL), F32),
        scratch_shapes=[pltpu.VMEM((tm, D_MODEL), BF16)],
        compiler_params=_cparams(("parallel", "arbitrary")),
        name="ffn",
    )(x, g.reshape(1, D_MODEL), wg, wu, wd)


def _proj_res_kernel(x_ref, ap_ref, as_ref, w_ref, o_ref, *, npt):
    i = pl.program_id(0)

    @pl.when(i < npt)
    def _():
        o_ref[...] = x_ref[...] + _dot(ap_ref[...].astype(BF16), w_ref[...])

    @pl.when(i >= npt)
    def _():
        o_ref[...] = x_ref[...] + _dot(as_ref[...].astype(BF16), w_ref[...])


def _proj_res_gated_kernel(x_ref, ap_ref, as_ref, gp_ref, gs_ref, w_ref, o_ref, *, npt):
    i = pl.program_id(0)

    @pl.when(i < npt)
    def _():
        o_ref[...] = x_ref[...] + _dot((ap_ref[...] * gp_ref[...]).astype(BF16), w_ref[...])

    @pl.when(i >= npt)
    def _():
        o_ref[...] = x_ref[...] + _dot((as_ref[...] * gs_ref[...]).astype(BF16), w_ref[...])


def _proj_res(x, a_p, a_s, w, gate_p=None, gate_s=None):
    n, n_p, n_s = x.shape[0], a_p.shape[0], a_s.shape[0]
    assert n == n_p + n_s
    tm = _pick(n_s, (512, 256, 128, 64, 32, 16, 8))
    assert n_p % tm == 0
    npt = n_p // tm
    kin = a_p.shape[1]
    xspec = pl.BlockSpec((tm, D_MODEL), lambda i: (i, 0))
    pspec = pl.BlockSpec((tm, kin), lambda i: (jnp.minimum(i, npt - 1), 0))
    sspec = pl.BlockSpec((tm, kin), lambda i: (jnp.maximum(i - npt, 0), 0))
    wspec = pl.BlockSpec((kin, D_MODEL), lambda i: (0, 0))
    if gate_p is None:
        body, specs, args = _proj_res_kernel, [xspec, pspec, sspec, wspec], (x, a_p, a_s, w)
    else:
        body = _proj_res_gated_kernel
        specs = [xspec, pspec, sspec, pspec, sspec, wspec]
        args = (x, a_p, a_s, gate_p, gate_s, w)
    return pl.pallas_call(
        functools.partial(body, npt=npt),
        grid=(n // tm,),
        in_specs=specs,
        out_specs=xspec,
        out_shape=jax.ShapeDtypeStruct(x.shape, F32),
        compiler_params=_cparams(("parallel",)),
        name="proj_res",
    )(*args)


def _mla_proj_kernel(x_ref, g_ref, wd_ref, gq_ref, gkv_ref, wqa_ref, wqb_ref, wuk_ref, wuv_ref,
                     gqn_ref, gkn_ref, ct_ref, st_ref,
                     q_ref, c_ref, kr_ref, k_ref, v_ref):
    hb = _rms(x_ref[...], g_ref[...]).astype(BF16)
    lat = _dot(hb, wd_ref[...])
    cq = _rms(lat[:, :Q_RANK], gq_ref[...]).astype(BF16)
    c = _rms(lat[:, Q_RANK:Q_RANK + KV_RANK], gkv_ref[...])
    ct = ct_ref[...]
    st = st_ref[...]
    o = Q_RANK + KV_RANK
    kr = lat[:, o:o + 128] * ct + lat[:, o + 128:o + 256] * st
    c_ref[...] = c
    kr_ref[...] = kr[:, :ROPE_DIM]

    qa = _dot(cq, wqa_ref[...])
    qb = _dot(cq, wqb_ref[...])
    gqn = gqn_ref[...]
    gkn = gkn_ref[...]
    cb = c.astype(BF16)
    kn = _dot(cb, wuk_ref[...])
    vv = _dot(cb, wuv_ref[...])
    ssr = jnp.sum(kr * kr, axis=-1, keepdims=True)
    for h in range(MLA_HEADS):
        nope = qa[:, h * QK_PAD:h * QK_PAD + 128]
        rp = qa[:, h * QK_PAD + 128:(h + 1) * QK_PAD] * ct + qb[:, h * 128:(h + 1) * 128] * st
        ss = jnp.sum(nope * nope, axis=-1, keepdims=True) + jnp.sum(rp * rp, axis=-1, keepdims=True)
        rs = lax.rsqrt(ss * (1.0 / QK_DIM) + RMS_EPS)
        q_ref[h, :, 0:128] = (nope * rs * gqn[:, 0:128]).astype(BF16)
        q_ref[h, :, 128:256] = (rp * rs * gqn[:, 128:256]).astype(BF16)
        knh = kn[:, h * 128:(h + 1) * 128]
        rk = lax.rsqrt((jnp.sum(knh * knh, axis=-1, keepdims=True) + ssr) * (1.0 / QK_DIM) + RMS_EPS)
        k_ref[h, :, 0:128] = (knh * rk * gkn[:, 0:128]).astype(BF16)
        k_ref[h, :, 128:256] = (kr * rk * gkn[:, 128:256]).astype(BF16)
        v_ref[h] = vv[:, h * 128:(h + 1) * 128].astype(BF16)


def _rot_cols(w):
    half = ROPE_DIM // 2
    return jnp.concatenate([-w[..., half:], w[..., :half]], axis=-1)


def _pad_lanes(w, n):
    return jnp.pad(w, [(0, 0)] * (w.ndim - 1) + [(0, n - w.shape[-1])])


def _mla_weights(m, mla_w_down, mla_g_q_lat, mla_g_kv_lat, mla_w_uq, mla_w_uk, mla_w_uv, mla_g_qn, mla_g_kn):
    wd = mla_w_down[m]
    o = Q_RANK + KV_RANK
    wkr = wd[:, o:]
    wd_ext = jnp.concatenate([wd[:, :o], _pad_lanes(wkr, 128), _pad_lanes(_rot_cols(wkr), 128)], axis=1)
    wq = mla_w_uq[m].reshape(Q_RANK, MLA_HEADS, QK_DIM)
    wqa = _pad_lanes(wq, QK_PAD).reshape(Q_RANK, MLA_HEADS * QK_PAD)
    wqb = _pad_lanes(_rot_cols(wq[..., NOPE_DIM:]), 128).reshape(Q_RANK, MLA_HEADS * 128)
    return dict(
        wd=wd_ext.astype(BF16), gq=mla_g_q_lat[m].reshape(1, Q_RANK), gkv=mla_g_kv_lat[m].reshape(1, KV_RANK),
        wqa=wqa.astype(BF16), wqb=wqb.astype(BF16),
        wuk=mla_w_uk[m].reshape(KV_RANK, MLA_HEADS * NOPE_DIM).astype(BF16),
        wuv=mla_w_uv[m].reshape(KV_RANK, MLA_HEADS * V_DIM).astype(BF16),
        gqn=_pad_lanes(mla_g_qn[m], QK_PAD).reshape(1, QK_PAD),
        gkn=_pad_lanes(mla_g_kn[m], QK_PAD).reshape(1, QK_PAD),
    )


def _rope_tables(pos):
    inv = ROPE_THETA ** (-jnp.arange(0, ROPE_DIM, 2, dtype=F32) / ROPE_DIM)
    ang = pos.astype(F32)[:, None] * inv[None, :]
    cos, sin = jnp.cos(ang), jnp.sin(ang)
    return (_pad_lanes(jnp.concatenate([cos, cos], axis=-1), 128),
            _pad_lanes(jnp.concatenate([sin, sin], axis=-1), 128))


def _mla_proj(x, g, w, ctab, stab, tm, tab_index):
    n = x.shape[0]
    full = lambda shape: pl.BlockSpec(shape, lambda i: (0,) * len(shape))
    tspec = pl.BlockSpec((tm, 128), lambda i: (tab_index(i), 0))
    return pl.pallas_call(
        _mla_proj_kernel,
        grid=(n // tm,),
        in_specs=[
            pl.BlockSpec((tm, D_MODEL), lambda i: (i, 0)),
            full((1, D_MODEL)), full(w["wd"].shape), full((1, Q_RANK)), full((1, KV_RANK)),
            full(w["wqa"].shape), full(w["wqb"].shape), full(w["wuk"].shape), full(w["wuv"].shape),
            full((1, QK_PAD)), full((1, QK_PAD)), tspec, tspec,
        ],
        out_specs=[
            pl.BlockSpec((MLA_HEADS, tm, QK_PAD), lambda i: (0, i, 0)),
            pl.BlockSpec((tm, KV_RANK), lambda i: (i, 0)),
            pl.BlockSpec((tm, ROPE_DIM), lambda i: (i, 0)),
            pl.BlockSpec((MLA_HEADS, tm, QK_PAD), lambda i: (0, i, 0)),
            pl.BlockSpec((MLA_HEADS, tm, V_DIM), lambda i: (0, i, 0)),
        ],
        out_shape=[
            jax.ShapeDtypeStruct((MLA_HEADS, n, QK_PAD), BF16),
            jax.ShapeDtypeStruct((n, KV_RANK), F32),
            jax.ShapeDtypeStruct((n, ROPE_DIM), F32),
            jax.ShapeDtypeStruct((MLA_HEADS, n, QK_PAD), BF16),
            jax.ShapeDtypeStruct((MLA_HEADS, n, V_DIM), BF16),
        ],
        compiler_params=_cparams(("parallel",)),
        name="mla_proj",
    )(x, g.reshape(1, D_MODEL), w["wd"], w["gq"], w["gkv"], w["wqa"], w["wqb"], w["wuk"], w["wuv"],
      w["gqn"], w["gkn"], ctab, stab)


def _prompt_attn_kernel(q_ref, k_ref, v_ref, o_ref, *, tq):
    l = q_ref.shape[1]
    nq = l // tq
    row = lax.broadcasted_iota(jnp.int32, (tq, tq), 0)
    col = lax.broadcasted_iota(jnp.int32, (tq, tq), 1)
    diag_mask = row >= col
    for qi in range(nq):
        q = q_ref[0, qi * tq:(qi + 1) * tq, :]
        m = l_sum = acc = None
        for ki in range(qi + 1):
            k = k_ref[0, ki * tq:(ki + 1) * tq, :]
            v = v_ref[0, ki * tq:(ki + 1) * tq, :]
            s = _dot_nt(q, k) * ATTN_SCALE
            if ki == qi:
                s = jnp.where(diag_mask, s, -jnp.inf)
            m_blk = jnp.max(s, axis=-1, keepdims=True)
            if ki == 0:
                m = m_blk
                p = jnp.exp(s - m)
                l_sum = jnp.sum(p, axis=-1, keepdims=True)
                acc = _dot(p.astype(BF16), v)
            else:
                m_new = jnp.maximum(m, m_blk)
                alpha = jnp.exp(m - m_new)
                p = jnp.exp(s - m_new)
                l_sum = alpha * l_sum + jnp.sum(p, axis=-1, keepdims=True)
                acc = alpha * acc + _dot(p.astype(BF16), v)
                m = m_new
        o_ref[qi * tq:(qi + 1) * tq, :] = (acc / l_sum).astype(o_ref.dtype)


def _prompt_attn(q, k, v, nb, l):
    tq = _pick(l, (512, 256, 128))
    return pl.pallas_call(
        functools.partial(_prompt_attn_kernel, tq=tq),
        grid=(nb, MLA_HEADS),
        in_specs=[
            pl.BlockSpec((1, l, QK_PAD), lambda b, h: (h, b, 0)),
            pl.BlockSpec((1, l, QK_PAD), lambda b, h: (h, b, 0)),
            pl.BlockSpec((1, l, V_DIM), lambda b, h: (h, b, 0)),
        ],
        out_specs=pl.BlockSpec((l, V_DIM), lambda b, h: (b, h)),
        out_shape=jax.ShapeDtypeStruct((nb * l, MLA_HEADS * V_DIM), BF16),
        compiler_params=_cparams(("parallel", "parallel")),
        name="prompt_attn",
    )(q, k, v)


def _sample_attn_kernel(pt_ref, q_ref, cn_ref, krn_ref, wukt_ref, wuk_ref, wuv_ref, gkn_ref, *rest, pg, tc):
    cpages = rest[:pg]
    kpages = rest[pg:2 * pg]
    o_ref = rest[2 * pg]
    wq_ref, qr_ref, m_ref, l_ref, acc_ref, cb_ref, krt_ref = rest[2 * pg + 1:]
    del pt_ref
    g = pl.program_id(1)
    ng = pl.num_programs(1)
    nq = q_ref.shape[1] // MLA_HEADS
    rows = q_ref.shape[1]
    n_nope = MLA_HEADS * NOPE_DIM

    @pl.when(g == 0)
    def _():
        gkn = gkn_ref[...]
        qts, qrs = [], []
        for h in range(MLA_HEADS):
            qh = q_ref[0, h * nq:(h + 1) * nq, :].astype(F32)
            qg = (qh[:, :NOPE_DIM] * gkn[:, :NOPE_DIM]).astype(BF16)
            qts.append(_dot_nt(qg, wuk_ref[:, h * NOPE_DIM:(h + 1) * NOPE_DIM]))
            qrs.append(qh[:, NOPE_DIM:QK_DIM] * gkn[:, NOPE_DIM:QK_DIM])
        wq_ref[0:n_nope, :] = wukt_ref[...]
        wq_ref[n_nope:n_nope + rows, :] = jnp.concatenate(qts, axis=0).astype(BF16)
        qr_ref[...] = jnp.concatenate(qrs, axis=0).astype(BF16)
        m_ref[...] = jnp.full(m_ref.shape, -jnp.inf, F32)
        l_ref[...] = jnp.zeros(l_ref.shape, F32)
        acc_ref[...] = jnp.zeros(acc_ref.shape, F32)

    def scores(cbc, ssr, s_rope):
        t = cbc.shape[0]
        big = _dot_nt(wq_ref[...], cbc)
        rs = []
        for h in range(MLA_HEADS):
            kh = big[h * NOPE_DIM:(h + 1) * NOPE_DIM]
            ssq = jnp.sum(kh * kh, axis=0, keepdims=True)
            rs_h = lax.rsqrt((ssq + ssr) * (1.0 / QK_DIM) + RMS_EPS)
            rs.append(jnp.broadcast_to(rs_h, (nq, t)))
        return (big[n_nope:] + s_rope) * ATTN_SCALE * jnp.concatenate(rs, axis=0)

    def update(s_list, c_list):
        m_old = m_ref[...]
        m_new = m_old
        for s in s_list:
            m_new = jnp.maximum(m_new, jnp.max(s, axis=-1, keepdims=True))
        alpha = jnp.exp(m_old - m_new)
        ps = [jnp.exp(s - m_new) for s in s_list]
        l_new = alpha * l_ref[...]
        for p in ps:
            l_new = l_new + jnp.sum(p, axis=-1, keepdims=True)
        acc = alpha * acc_ref[...]
        for p, cbc in zip(ps, c_list):
            acc = acc + _dot(p.astype(BF16), cbc)
        l_ref[...] = l_new
        acc_ref[...] = acc
        m_ref[...] = m_new

    for j in range(pg):
        cb_ref[j * PAGE_SIZE:(j + 1) * PAGE_SIZE, :] = cpages[j][0, 0].astype(BF16)
        krt_ref[:, j * PAGE_SIZE:(j + 1) * PAGE_SIZE] = kpages[j][0, 0]
    s_list, c_list = [], []
    for ci in range(pg * PAGE_SIZE // tc):
        cbc = cb_ref[ci * tc:(ci + 1) * tc, :]
        krt = krt_ref[:, ci * tc:(ci + 1) * tc]
        ssr = jnp.sum(krt * krt, axis=0, keepdims=True)
        s_list.append(scores(cbc, ssr, _dot(qr_ref[...], krt.astype(BF16))))
        c_list.append(cbc)
    update(s_list, c_list)

    @pl.when(g == ng - 1)
    def _():
        pad = PAGE_SIZE - nq
        cn = jnp.concatenate([cn_ref[...], jnp.zeros((pad, KV_RANK), F32)], axis=0).astype(BF16)
        krn = jnp.concatenate([krn_ref[...], jnp.zeros((pad, ROPE_DIM), F32)], axis=0)
        ssr = _dot_nt(jnp.ones((8, ROPE_DIM), BF16), (krn * krn).astype(BF16))[0:1]
        s = scores(cn, ssr, _dot_nt(qr_ref[...], krn.astype(BF16)))
        qpos = lax.broadcasted_iota(jnp.int32, (rows, PAGE_SIZE), 0) % nq
        tok = lax.broadcasted_iota(jnp.int32, (rows, PAGE_SIZE), 1)
        update([jnp.where(tok <= qpos, s, -jnp.inf)], [cn])
        o_lat = (acc_ref[...] / l_ref[...]).astype(BF16)
        for h in range(MLA_HEADS):
            o_ref[:, h * V_DIM:(h + 1) * V_DIM] = _dot(o_lat[h * nq:(h + 1) * nq], wuv_ref[:, h * V_DIM:(h + 1) * V_DIM])


def _sample_attn(q_s, c_all, kr_all, row0, cache_c, cache_kr, m, page_table, wukt, wuk, wuv, gkn):
    ns, n_pages = page_table.shape
    rows = q_s.shape[1]
    nq = rows // MLA_HEADS
    assert row0 % nq == 0 and nq % 8 == 0
    pg = min(16, n_pages)
    assert n_pages % pg == 0 and pg % 2 == 0
    tc = 2 * PAGE_SIZE
    blk0 = row0 // nq
    full = lambda shape: pl.BlockSpec(shape, lambda b, g, pt: (0,) * len(shape))

    def page_spec(j, shape):
        return pl.BlockSpec((1, 1) + shape, lambda b, g, pt: (m, pt[b * n_pages + g * pg + j], 0, 0))

    cache_kr_t = jnp.swapaxes(cache_kr, 2, 3)
    grid_spec = pltpu.PrefetchScalarGridSpec(
        num_scalar_prefetch=1,
        grid=(ns, n_pages // pg),
        in_specs=[
            pl.BlockSpec((1, rows, QK_PAD), lambda b, g, pt: (b, 0, 0)),
            pl.BlockSpec((nq, KV_RANK), lambda b, g, pt: (blk0 + b, 0)),
            pl.BlockSpec((nq, ROPE_DIM), lambda b, g, pt: (blk0 + b, 0)),
            full(wukt.shape), full(wuk.shape), full(wuv.shape), full((1, QK_PAD)),
        ] + [page_spec(j, (PAGE_SIZE, KV_RANK)) for j in range(pg)]
          + [page_spec(j, (ROPE_DIM, PAGE_SIZE)) for j in range(pg)],
        out_specs=pl.BlockSpec((nq, MLA_HEADS * V_DIM), lambda b, g, pt: (b, 0)),
        scratch_shapes=[
            pltpu.VMEM((MLA_HEADS * NOPE_DIM + rows, KV_RANK), BF16), pltpu.VMEM((rows, ROPE_DIM), BF16),
            pltpu.VMEM((rows, 1), F32), pltpu.VMEM((rows, 1), F32), pltpu.VMEM((rows, KV_RANK), F32),
            pltpu.VMEM((pg * PAGE_SIZE, KV_RANK), BF16), pltpu.VMEM((ROPE_DIM, pg * PAGE_SIZE), F32),
        ],
    )
    return pl.pallas_call(
        functools.partial(_sample_attn_kernel, pg=pg, tc=tc),
        grid_spec=grid_spec,
        out_shape=jax.ShapeDtypeStruct((ns * nq, MLA_HEADS * V_DIM), F32),
        compiler_params=_cparams(("arbitrary", "arbitrary")),
        name="sample_attn",
    )(page_table.reshape(-1), q_s, c_all, kr_all, wukt, wuk, wuv, gkn,
      *([cache_c] * pg), *([cache_kr_t] * pg))


def _softplus(y):
    return jnp.maximum(y, 0.0) + jnp.log(1.0 + jnp.exp(-jnp.abs(y)))


def _rwkv_proj_kernel(*refs, sample, has_vres, tiles_per_seq, nq):
    it = iter(refs)
    x_ref, g_ref, mu_ref = next(it), next(it), next(it)
    shift_ref = next(it) if sample else None
    wr, wk, wv, w0, w1, w2, a0, a1, a2, g1, g2 = (next(it) for _ in range(11))
    if has_vres:
        v0, v1, v2, vf_ref = (next(it) for _ in range(4))
    r_ref, k_ref, v_ref, lw_ref, alr_ref, gate_ref, hl_ref = (next(it) for _ in range(7))
    carry_ref = next(it)
    tm = x_ref.shape[0]
    i = pl.program_id(0)

    h = _rms(x_ref[...], g_ref[...])
    rolled = pltpu.roll(h, 1, axis=0)
    row = lax.broadcasted_iota(jnp.int32, (tm, 1), 0)
    if sample:
        prev = jnp.where(row % nq == 0, shift_ref[...], rolled)
        hl_ref[...] = h
    else:
        @pl.when(i == 0)
        def _():
            carry_ref[...] = jnp.zeros(carry_ref.shape, F32)

        first = jnp.where(i % tiles_per_seq == 0, 0.0, carry_ref[0:1, :])
        prev = jnp.where(row == 0, first, rolled)
        carry_ref[0:1, :] = h[tm - 1:tm, :]
        hl_ref[...] = h[tm - 8:tm, :]
    xx = prev - h
    mu = mu_ref[...]
    mix = lambda n: (h + xx * mu[n:n + 1, :]).astype(BF16)
    xr, xw, xk, xv, xa, xg = (mix(n) for n in range(6))
    r = _dot(xr, wr[...])
    k = _dot(xk, wk[...])
    v = _dot(xv, wv[...])
    wl = w0[...] + _dot(jnp.tanh(_dot(xw, w1[...])).astype(BF16), w2[...])
    lw = -jnp.exp(-_softplus(-wl) - 0.5)
    alr = jax.nn.sigmoid(a0[...] + _dot(_dot(xa, a1[...]).astype(BF16), a2[...]))
    gate_ref[...] = _dot(jax.nn.sigmoid(_dot(xg, g1[...])).astype(BF16), g2[...])
    if has_vres:
        vgate = jax.nn.sigmoid(v0[...] + _dot(_dot(xv, v1[...]).astype(BF16), v2[...]))
    for p in range(HEAD_PAIRS):
        sl = slice(p * PAIR_W, (p + 1) * PAIR_W)
        vp = v[:, sl]
        if has_vres:
            vp = vp + (vf_ref[p] - vp) * vgate[:, sl]
        r_ref[p] = r[:, sl]
        k_ref[p] = k[:, sl]
        v_ref[p] = vp
        lw_ref[p] = lw[:, sl]
        alr_ref[p] = alr[:, sl]


def _rwkv_weights(j, rwkv_mu, rwkv_w_r, rwkv_w_k, rwkv_w_v, rwkv_w0, rwkv_w1, rwkv_w2, rwkv_a0, rwkv_a1,
                  rwkv_a2, rwkv_v0, rwkv_v1, rwkv_v2, rwkv_g1, rwkv_g2):
    row = lambda a: a.reshape(1, D_MODEL)
    w = [rwkv_w_r[j].astype(BF16), rwkv_w_k[j].astype(BF16), rwkv_w_v[j].astype(BF16),
         row(rwkv_w0[j]), rwkv_w1[j].astype(BF16), rwkv_w2[j].astype(BF16),
         row(rwkv_a0[j]), rwkv_a1[j].astype(BF16), rwkv_a2[j].astype(BF16),
         rwkv_g1[j].astype(BF16), rwkv_g2[j].astype(BF16)]
    if j > 0:
        w += [row(rwkv_v0[j - 1]), rwkv_v1[j - 1].astype(BF16), rwkv_v2[j - 1].astype(BF16)]
    return rwkv_mu[j], w


def _rwkv_proj(x, row0, n, g, mu, weights, v_first, shift_exp, seq_len):
    sample = shift_exp is not None
    has_vres = v_first is not None
    tm = _pick(n if sample else seq_len, (256, 128, 64, 32, 16, 8))
    assert row0 % tm == 0 and n % tm == 0
    off = row0 // tm
    if sample:
        assert tm % seq_len == 0
        tiles_per_seq = 1
    else:
        assert seq_len % tm == 0
        tiles_per_seq = seq_len // tm
    full = lambda a: pl.BlockSpec(a.shape, lambda i: (0,) * a.ndim)
    pair_spec = pl.BlockSpec((HEAD_PAIRS, tm, PAIR_W), lambda i: (0, i, 0))
    row_spec = pl.BlockSpec((tm, D_MODEL), lambda i: (i, 0))
    g2d = g.reshape(1, D_MODEL)
    args = [x, g2d, mu]
    specs = [pl.BlockSpec((tm, D_MODEL), lambda i: (i + off, 0)), full(g2d), full(mu)]
    if sample:
        args.append(shift_exp)
        specs.append(row_spec)
    args += weights
    specs += [full(a) for a in weights]
    if has_vres:
        args.append(v_first)
        specs.append(pair_spec)
    pair_shape = jax.ShapeDtypeStruct((HEAD_PAIRS, n, PAIR_W), F32)
    if sample:
        hl_spec, hl_shape = row_spec, jax.ShapeDtypeStruct((n, D_MODEL), F32)
    else:
        hl_spec = pl.BlockSpec((8, D_MODEL), lambda i: (i, 0))
        hl_shape = jax.ShapeDtypeStruct((n // tm * 8, D_MODEL), F32)
    return pl.pallas_call(
        functools.partial(_rwkv_proj_kernel, sample=sample, has_vres=has_vres,
                          tiles_per_seq=tiles_per_seq, nq=seq_len),
        grid=(n // tm,),
        in_specs=specs,
        out_specs=[pair_spec] * 5 + [row_spec, hl_spec],
        out_shape=[pair_shape] * 5 + [jax.ShapeDtypeStruct((n, D_MODEL), F32), hl_shape],
        scratch_shapes=[pltpu.VMEM((8, D_MODEL), F32)],
        compiler_params=_cparams(("arbitrary",)),
        name="rwkv_proj",
    )(*args)


def _wkv_masks(c):
    c2 = 2 * c
    ri = lax.broadcasted_iota(jnp.int32, (c2, c2), 0)
    cj = lax.broadcasted_iota(jnp.int32, (c2, c2), 1)
    same = (ri // c) == (cj // c)
    t, s = ri % c, cj % c
    levels = []
    step = 1
    while step < c:
        levels.append(same & ((t ^ s) < 2 * step) & ((t & step) != 0) & ((s & step) == 0))
        step *= 2
    ti = lax.broadcasted_iota(jnp.int32, (c, c), 0)
    tj = lax.broadcasted_iota(jnp.int32, (c, c), 1)
    lane = lax.broadcasted_iota(jnp.int32, (1, PAIR_W), 1)
    row = lax.broadcasted_iota(jnp.int32, (c2, 1), 0)
    ei = lax.broadcasted_iota(jnp.int32, (PAIR_W, PAIR_W), 0)
    ej = lax.broadcasted_iota(jnp.int32, (PAIR_W, PAIR_W), 1)
    return dict(
        strict=same & (t > s), incl=same & (t >= s), levels=levels,
        tri=jnp.where(ti >= tj, 1.0, 0.0).astype(BF16),
        m0=lane < HEAD_SIZE, own=(row < c) == (lane < HEAD_SIZE), eye=ei == ej,
    )


def _wkv_block(chunks, prm, msk, h0, chained):
    kkp, kap, rkp, lnw, lnb = prm
    n = len(chunks)
    c = chunks[0][0].shape[0]
    c2 = 2 * c
    m0 = msk["m0"]
    tri = msk["tri"]
    b16 = lambda a: a.astype(BF16)
    cat0 = lambda a, b: jnp.concatenate([a, b], axis=0)
    cat1 = lambda a, b: jnp.concatenate([a, b], axis=1)

    def stack(a):
        return cat0(jnp.where(m0, a, 0.0), jnp.where(m0, 0.0, a))

    cums = []
    for (_, _, _, lw, _) in chunks:
        hi = b16(lw)
        r1 = lw - hi.astype(F32)
        mid = b16(r1)
        lo = b16(r1 - mid.astype(F32))
        cums.append(_dot(tri, hi) + _dot(tri, mid) + _dot(tri, lo))

    lhs, rhs, at, rt, bh, kh, v16, bonus, gcol = ([] for _ in range(9))
    for (r, k, v, lw, alr), cum in zip(chunks, cums):
        cum_last = cum[c - 1:c, :]
        e_in, e_ex = jnp.exp(cum), jnp.exp(cum - lw)
        e_inv, e_end = jnp.exp(-cum), jnp.exp(cum_last - cum)
        kk = stack(k * kkp)
        kk = kk / jnp.maximum(jnp.sqrt(jnp.sum(kk * kk, axis=-1, keepdims=True)), 1e-12)
        kmod = stack(k * (1.0 + (alr - 1.0) * kap))
        r_s, v_s = stack(r), stack(v)
        bb = kk * stack(alr)
        a_t, r_t = -kk * cat0(e_ex, e_ex), r_s * cat0(e_in, e_in)
        lhs.append(b16(cat0(a_t, r_t)))
        rhs.append(b16(cat0(bb * cat0(e_inv, e_inv), kmod * cat0(e_inv, e_inv))))
        at.append(a_t)
        rt.append(r_t)
        bh.append(b16(bb * cat0(e_end, e_end)))
        kh.append(b16(kmod * cat0(e_end, e_end)))
        v16.append(b16(v_s))
        bonus.append(jnp.sum(r_s * kmod * rkp, axis=-1, keepdims=True) * v_s)
        gcol.append(jnp.sum(jnp.where(msk["eye"], jnp.exp(cum_last), 0.0), axis=1, keepdims=True))

    gram = [_dot_nt(a, b) for a, b in zip(lhs, rhs)]
    ab = [jnp.where(msk["strict"], g[:c2, :c2], 0.0) for g in gram]
    ak = [b16(jnp.where(msk["strict"], g[:c2, c2:], 0.0)) for g in gram]
    rb = [b16(jnp.where(msk["incl"], g[c2:, :c2], 0.0)) for g in gram]
    rk = [b16(jnp.where(msk["incl"], g[c2:, c2:], 0.0)) for g in gram]
    akv = [_dot(a, v) for a, v in zip(ak, v16)]
    rkv = [_dot(a, v) for a, v in zip(rk, v16)]
    khv = [_dot_tn(a, v) for a, v in zip(kh, v16)]

    levels = msk["levels"]
    e = [jnp.where(levels[0], a, 0.0) for a in ab]
    for lvl in levels[1:]:
        low = [jnp.where(lvl, a, 0.0) for a in ab]
        x = [lo_ + _dot(b16(e_), b16(lo_)) for e_, lo_ in zip(e, low)]
        e = [e_ + x_ + _dot(b16(x_), b16(e_)) for e_, x_ in zip(e, x)]

    w2 = [cat1(a, b) for a, b in zip(akv, at)]
    u2 = [b16(w + _dot(b16(e_), b16(w))) for w, e_ in zip(w2, e)]
    y2 = [_dot(a, u) + cat1(b, r_) for a, u, b, r_ in zip(rb, u2, rkv, rt)]
    mj = [_dot_tn(a, u) for a, u in zip(bh, u2)]

    y_st, h_out = [], []
    h = h0
    for i in range(n):
        h_in = h if chained else h0[i]
        h16 = b16(h_in)
        y_st.append(y2[i][:, :PAIR_W] + _dot(b16(y2[i][:, PAIR_W:]), h16))
        h = gcol[i] * h_in + _dot(b16(mj[i][:, PAIR_W:]), h16) + (mj[i][:, :PAIR_W] + khv[i])
        h_out.append(h)

    own = msk["own"]
    inv_n = 1.0 / HEAD_SIZE
    zs = []
    for y, bo in zip(y_st, bonus):
        mean = jnp.sum(y, axis=-1, keepdims=True) * inv_n
        d = jnp.where(own, y - mean, 0.0)
        var = jnp.sum(d * d, axis=-1, keepdims=True) * inv_n
        z_st = d * lax.rsqrt(var + LNX_EPS) * lnw + jnp.where(own, lnb, 0.0) + bo
        zs.append(z_st[:c] + z_st[c:])
    return zs, h_out


def _state_in(s_ref, i):
    z = jnp.zeros((HEAD_SIZE, HEAD_SIZE), F32)
    top = jnp.concatenate([s_ref[i, 0], z], axis=1)
    bot = jnp.concatenate([z, s_ref[i, 1]], axis=1)
    return jnp.concatenate([top, bot], axis=0).T


def _state_out(s_ref, i, hbd):
    ht = hbd.T
    s_ref[i, 0] = ht[:HEAD_SIZE, :HEAD_SIZE]
    s_ref[i, 1] = ht[HEAD_SIZE:, HEAD_SIZE:]


def _wkv_kernel(*refs, c, nb, blocks_per_seq, per_seq_state):
    data = refs[:5]
    prm = tuple(ref[0] for ref in refs[5:10])
    msk = _wkv_masks(c)
    chunks = [tuple(ref[0, ci * c:(ci + 1) * c, :] for ref in data) for ci in range(nb)]
    if per_seq_state:
        s0_ref, z_ref, sout_ref = refs[10:]
        zs, hs = _wkv_block(chunks, prm, msk, [_state_in(s0_ref, ci) for ci in range(nb)], False)
        for ci in range(nb):
            _state_out(sout_ref, ci, hs[ci])
    else:
        z_ref, sout_ref, h_ref = refs[10:]
        b = pl.program_id(1)

        @pl.when(b % blocks_per_seq == 0)
        def _():
            h_ref[...] = jnp.zeros(h_ref.shape, F32)

        zs, hs = _wkv_block(chunks, prm, msk, h_ref[...], True)
        h_ref[...] = hs[-1]

        @pl.when(b % blocks_per_seq == blocks_per_seq - 1)
        def _():
            _state_out(sout_ref, 0, hs[-1])
    for ci in range(nb):
        z_ref[ci * c:(ci + 1) * c, :] = zs[ci]


def _pair_rows(a):
    return a.reshape(HEAD_PAIRS, 1, PAIR_W)


def _wkv(r, k, v, lw, alr, params, s0, n_seq, seq_len):
    c = WKV_CHUNK
    per_seq_state = s0 is not None
    chunks_per_seq = seq_len // c
    assert seq_len % c == 0 and (chunks_per_seq == 1 or not per_seq_state)
    nb = _pick(n_seq if per_seq_state else chunks_per_seq, (8, 4, 2, 1))
    blocks_per_seq = max(chunks_per_seq // nb, 1)
    n_blocks = n_seq * chunks_per_seq // nb
    dspec = pl.BlockSpec((1, nb * c, PAIR_W), lambda p, b: (p, b, 0))
    pspec = pl.BlockSpec((1, 1, PAIR_W), lambda p, b: (p, 0, 0))
    if per_seq_state:
        sspec = pl.BlockSpec((nb, 2, HEAD_SIZE, HEAD_SIZE), lambda p, b: (b, p, 0, 0))
        extra_in, extra_args, scratch = [sspec], [s0], []
    else:
        sspec = pl.BlockSpec((1, 2, HEAD_SIZE, HEAD_SIZE), lambda p, b: (b // blocks_per_seq, p, 0, 0))
        extra_in, extra_args, scratch = [], [], [pltpu.VMEM((PAIR_W, PAIR_W), F32)]
    return pl.pallas_call(
        functools.partial(_wkv_kernel, c=c, nb=nb, blocks_per_seq=blocks_per_seq, per_seq_state=per_seq_state),
        grid=(HEAD_PAIRS, n_blocks),
        in_specs=[dspec] * 5 + [pspec] * 5 + extra_in,
        out_specs=[pl.BlockSpec((nb * c, PAIR_W), lambda p, b: (b, p)), sspec],
        out_shape=[jax.ShapeDtypeStruct((n_seq * seq_len, D_MODEL), F32),
                   jax.ShapeDtypeStruct((n_seq, RWKV_HEADS, HEAD_SIZE, HEAD_SIZE), F32)],
        scratch_shapes=scratch,
        compiler_params=_cparams(("parallel", "arbitrary")),
        name="wkv",
    )(r, k, v, lw, alr, *[_pair_rows(p) for p in params], *extra_args)


def _pad_steps(a, n_seq, seq_len, padded):
    a = a.reshape(HEAD_PAIRS, n_seq, seq_len, PAIR_W)
    a = jnp.pad(a, ((0, 0), (0, 0), (0, padded - seq_len), (0, 0)))
    return a.reshape(HEAD_PAIRS, n_seq * padded, PAIR_W)


def kernel(x_prompt, x_sample, cache_kv_latent, cache_k_rope, state_wkv, state_shift, page_table,
           ffn_norm, ffn_w_gate, ffn_w_up, ffn_w_down, mix_norm,
           mla_w_down, mla_g_q_lat, mla_g_kv_lat, mla_w_uq, mla_w_uk, mla_w_uv, mla_g_qn, mla_g_kn, mla_w_o,
           rwkv_mu, rwkv_w_r, rwkv_w_k, rwkv_w_v, rwkv_w_o, rwkv_w0, rwkv_w1, rwkv_w2,
           rwkv_a0, rwkv_a1, rwkv_a2, rwkv_v0, rwkv_v1, rwkv_v2, rwkv_g1, rwkv_g2,
           rwkv_k_k, rwkv_k_a, rwkv_r_k, rwkv_lnx_w, rwkv_lnx_b):
    nb, l, _ = x_prompt.shape
    ns, nq, _ = x_sample.shape
    n_p, n_s = nb * l, ns * nq
    depth = ffn_norm.shape[0]
    past = page_table.shape[1] * PAGE_SIZE
    x = jnp.concatenate([x_prompt.reshape(n_p, D_MODEL), x_sample.reshape(n_s, D_MODEL)], axis=0)

    tm = _pick(n_s, (512, 256, 128, 64, 32, 16, 8))
    assert l % tm == 0 and tm % nq == 0
    cos_p, sin_p = _rope_tables(jnp.arange(l))
    cos_s, sin_s = _rope_tables(past + jnp.arange(nq))
    ctab = jnp.concatenate([cos_p, jnp.tile(cos_s, (tm // nq, 1))], axis=0)
    stab = jnp.concatenate([sin_p, jnp.tile(sin_s, (tm // nq, 1))], axis=0)
    n_ptiles, tiles_per_seq = n_p // tm, l // tm
    tab_index = lambda i: jnp.where(i < n_ptiles, i % tiles_per_seq, tiles_per_seq)

    lat_rows, rope_rows, p_wkv, p_shift, s_wkv, s_shift = [], [], [], [], [], []
    vf_p = vf_s = None
    for i in range(depth):
        x = _ffn(x, ffn_norm[i, 0], ffn_w_gate[i, 0].astype(BF16), ffn_w_up[i, 0].astype(BF16),
                 ffn_w_down[i, 0].astype(BF16))
        if i % 2 == 0:
            m = i // 2
            w = _mla_weights(m, mla_w_down, mla_g_q_lat, mla_g_kv_lat, mla_w_uq, mla_w_uk, mla_w_uv,
                             mla_g_qn, mla_g_kn)
            q, c, kr, k, v = _mla_proj(x, mix_norm[i], w, ctab, stab, tm, tab_index)
            o_p = _prompt_attn(q, k, v, nb, l)
            q_s = jnp.transpose(q[:, n_p:, :].reshape(MLA_HEADS, ns, nq, QK_PAD), (1, 0, 2, 3))
            q_s = q_s.reshape(ns, MLA_HEADS * nq, QK_PAD)
            o_s = _sample_attn(q_s, c, kr, n_p, cache_kv_latent, cache_k_rope, m, page_table,
                               jnp.transpose(w["wuk"]), w["wuk"], w["wuv"], w["gkn"])
            x = _proj_res(x, o_p, o_s, mla_w_o[m].astype(BF16))
            lat_rows.append(c)
            rope_rows.append(kr)
        else:
            j = i // 2
            mu, w = _rwkv_weights(j, rwkv_mu, rwkv_w_r, rwkv_w_k, rwkv_w_v, rwkv_w0, rwkv_w1, rwkv_w2,
                                  rwkv_a0, rwkv_a1, rwkv_a2, rwkv_v0, rwkv_v1, rwkv_v2, rwkv_g1, rwkv_g2)
            params = [rwkv_k_k[j], rwkv_k_a[j], rwkv_r_k[j], rwkv_lnx_w[j], rwkv_lnx_b[j]]
            r, k, v, lw, alr, gate_p, hl = _rwkv_proj(x, 0, n_p, mix_norm[i], mu, w, vf_p, None, l)
            if j == 0:
                vf_p = v
            z_p, st_p = _wkv(r, k, v, lw, alr, params, None, nb, l)
            p_wkv.append(st_p)
            p_shift.append(hl.reshape(nb, -1, 8, D_MODEL)[:, -1, 7])
            shift_exp = jnp.zeros((ns, nq, D_MODEL), F32).at[:, 0, :].set(state_shift[j]).reshape(n_s, D_MODEL)
            r, k, v, lw, alr, gate_s, h_s = _rwkv_proj(x, n_p, n_s, mix_norm[i], mu, w, vf_s, shift_exp, nq)
            if j == 0:
                vf_s = v
            pad = lambda a: _pad_steps(a, ns, nq, WKV_CHUNK)
            z_s, st_s = _wkv(pad(r), pad(k), pad(v), pad(lw), pad(alr), params, state_wkv[j], ns, WKV_CHUNK)
            z_s = z_s.reshape(ns, WKV_CHUNK, D_MODEL)[:, :nq].reshape(n_s, D_MODEL)
            s_wkv.append(st_s)
            s_shift.append(h_s.reshape(ns, nq, D_MODEL)[:, -1])
            x = _proj_res(x, z_p, z_s, rwkv_w_o[j].astype(BF16), gate_p, gate_s)
        x = _ffn(x, ffn_norm[i, 1], ffn_w_gate[i, 1].astype(BF16), ffn_w_up[i, 1].astype(BF16),
                 ffn_w_down[i, 1].astype(BF16))

    lat = jnp.stack(lat_rows)
    rope = jnp.stack(rope_rows)
    return (x[:n_p].reshape(nb, l, D_MODEL), x[n_p:].reshape(ns, nq, D_MODEL),
            lat[:, :n_p].reshape(-1, nb, l, KV_RANK), rope[:, :n_p].reshape(-1, nb, l, ROPE_DIM),
            jnp.stack(p_wkv), jnp.stack(p_shift),
            lat[:, n_p:].reshape(-1, ns, nq, KV_RANK), rope[:, n_p:].reshape(-1, ns, nq, ROPE_DIM),
            jnp.stack(s_wkv), jnp.stack(s_shift))
```

```python
import functools

import jax
import jax.numpy as jnp
from jax import lax
from jax.experimental import pallas as pl
from jax.experimental.pallas import tpu as pltpu

F32 = jnp.float32
BF16 = jnp.bfloat16

D_MODEL = 1024
D_FF = 2816
RMS_EPS = 1e-6
MLA_HEADS = 8
Q_RANK = 384
KV_RANK = 256
NOPE_DIM = 128
ROPE_DIM = 64
QK_DIM = NOPE_DIM + ROPE_DIM
QK_PAD = 256
V_DIM = 128
ROPE_THETA = 10000.0
ATTN_SCALE = QK_DIM ** -0.5
PAGE_SIZE = 128
HEAD_SIZE = 64
RWKV_HEADS = D_MODEL // HEAD_SIZE
HEAD_PAIRS = RWKV_HEADS // 2
PAIR_W = 2 * HEAD_SIZE
LNX_EPS = 64e-5
WKV_CHUNK = 64

VMEM_LIMIT = 48 * 1024 * 1024


def _cparams(sem):
    return pltpu.CompilerParams(dimension_semantics=sem, vmem_limit_bytes=VMEM_LIMIT)


def _pick(n, cands):
    for c in cands:
        if n % c == 0:
            return c
    raise ValueError(f"no tile in {cands} divides {n}")


def _rms(x, g):
    return x * lax.rsqrt(jnp.mean(x * x, axis=-1, keepdims=True) + RMS_EPS) * g


def _dot(a, b):
    return jnp.dot(a, b, preferred_element_type=F32)


def _dot_nt(a, b):
    return lax.dot_general(a, b, (((1,), (1,)), ((), ())), preferred_element_type=F32)


def _dot_tn(a, b):
    return lax.dot_general(a, b, (((0,), (0,)), ((), ())), preferred_element_type=F32)


FFN_CHUNK = 256


def _swiglu_res(x, g_ref, wg_ref, wu_ref, wd_ref):
    hb = _rms(x, g_ref[...]).astype(BF16)
    acc = x
    for c in range(D_FF // FFN_CHUNK):
        sl = slice(c * FFN_CHUNK, (c + 1) * FFN_CHUNK)
        a = _dot(hb, wg_ref[:, sl])
        u = _dot(hb, wu_ref[:, sl])
        act = (0.5 * a * jax.nn.sigmoid(a) * u).astype(BF16)
        acc = acc + _dot(act, wd_ref[sl, :])
    return acc


def _ffn_kernel(x_ref, g_ref, wg_ref, wu_ref, wd_ref, o_ref):
    o_ref[...] = _swiglu_res(x_ref[...], g_ref, wg_ref, wu_ref, wd_ref)


def _proj_ffn_kernel(*refs, npt, gated):
    if gated:
        x_ref, ap_ref, as_ref, gp_ref, gs_ref, wp_ref, g_ref, wg_ref, wu_ref, wd_ref, o_ref = refs
    else:
        x_ref, ap_ref, as_ref, wp_ref, g_ref, wg_ref, wu_ref, wd_ref, o_ref = refs
        gp_ref = gs_ref = None
    i = pl.program_id(0)

    def project(a_ref, gate_ref):
        a = a_ref[...] if gate_ref is None else a_ref[...] * gate_ref[...]
        o_ref[...] = x_ref[...] + _dot(a.astype(BF16), wp_ref[...])

    @pl.when(i < npt)
    def _():
        project(ap_ref, gp_ref)

    @pl.when(i >= npt)
    def _():
        project(as_ref, gs_ref)

    o_ref[...] = _swiglu_res(o_ref[...], g_ref, wg_ref, wu_ref, wd_ref)


def _resident(shape):
    return pl.BlockSpec(shape, lambda i: (0,) * len(shape), pipeline_mode=pl.Buffered(1))


def _ffn(x, g, wg, wu, wd, proj=None):
    n = x.shape[0]
    tm = _pick(n, (512, 256, 128, 64, 32, 16, 8))
    xspec = pl.BlockSpec((tm, D_MODEL), lambda i: (i, 0))
    wspecs = [_resident((1, D_MODEL)), _resident(wg.shape), _resident(wu.shape), _resident(wd.shape)]
    wargs = (g.reshape(1, D_MODEL), wg, wu, wd)
    if proj is None:
        body, specs, args = _ffn_kernel, [xspec], (x,)
    else:
        a_p, a_s, w, gate_p, gate_s = proj
        n_p, n_s = a_p.shape[0], a_s.shape[0]
        tm = _pick(n_s, (512, 256, 128, 64, 32, 16, 8))
        assert n == n_p + n_s and n_p % tm == 0
        npt = n_p // tm
        kin = a_p.shape[1]
        xspec = pl.BlockSpec((tm, D_MODEL), lambda i: (i, 0))
        pspec = pl.BlockSpec((tm, kin), lambda i: (jnp.minimum(i, npt - 1), 0))
        sspec = pl.BlockSpec((tm, kin), lambda i: (jnp.maximum(i - npt, 0), 0))
        gated = gate_p is not None
        body = functools.partial(_proj_ffn_kernel, npt=npt, gated=gated)
        specs = [xspec, pspec, sspec] + ([pspec, sspec] if gated else []) + [_resident(w.shape)]
        args = (x, a_p, a_s) + ((gate_p, gate_s) if gated else ()) + (w,)
    return pl.pallas_call(
        body,
        grid=(n // tm,),
        in_specs=specs + wspecs,
        out_specs=xspec,
        out_shape=jax.ShapeDtypeStruct((n, D_MODEL), F32),
        compiler_params=pltpu.CompilerParams(dimension_semantics=("parallel",), vmem_limit_bytes=56 * 1024 * 1024),
        name="ffn",
    )(*args, *wargs)


def _mla_proj_kernel(x_ref, g_ref, wd_ref, gq_ref, gkv_ref, wqa_ref, wqb_ref, wuk_ref, wuv_ref,
                     gqn_ref, gkn_ref, ct_ref, st_ref,
                     q_ref, c_ref, kr_ref, k_ref, v_ref):
    hb = _rms(x_ref[...], g_ref[...]).astype(BF16)
    lat = _dot(hb, wd_ref[...])
    cq = _rms(lat[:, :Q_RANK], gq_ref[...]).astype(BF16)
    c = _rms(lat[:, Q_RANK:Q_RANK + KV_RANK], gkv_ref[...])
    ct = ct_ref[...]
    st = st_ref[...]
    o = Q_RANK + KV_RANK
    kr = lat[:, o:o + 128] * ct + lat[:, o + 128:o + 256] * st
    c_ref[...] = c
    kr_ref[...] = kr[:, :ROPE_DIM]

    qa = _dot(cq, wqa_ref[...])
    qb = _dot(cq, wqb_ref[...])
    gqn = gqn_ref[...]
    gkn = gkn_ref[...]
    cb = c.astype(BF16)
    kn = _dot(cb, wuk_ref[...])
    vv = _dot(cb, wuv_ref[...])
    ssr = jnp.sum(kr * kr, axis=-1, keepdims=True)
    for h in range(MLA_HEADS):
        nope = qa[:, h * QK_PAD:h * QK_PAD + 128]
        rp = qa[:, h * QK_PAD + 128:(h + 1) * QK_PAD] * ct + qb[:, h * 128:(h + 1) * 128] * st
        ss = jnp.sum(nope * nope, axis=-1, keepdims=True) + jnp.sum(rp * rp, axis=-1, keepdims=True)
        rs = lax.rsqrt(ss * (1.0 / QK_DIM) + RMS_EPS)
        q_ref[h, :, 0:128] = (nope * rs * gqn[:, 0:128]).astype(BF16)
        q_ref[h, :, 128:256] = (rp * rs * gqn[:, 128:256]).astype(BF16)
        knh = kn[:, h * 128:(h + 1) * 128]
        rk = lax.rsqrt((jnp.sum(knh * knh, axis=-1, keepdims=True) + ssr) * (1.0 / QK_DIM) + RMS_EPS)
        k_ref[h, :, 0:128] = (knh * rk * gkn[:, 0:128]).astype(BF16)
        k_ref[h, :, 128:256] = (kr * rk * gkn[:, 128:256]).astype(BF16)
        v_ref[h] = vv[:, h * 128:(h + 1) * 128].astype(BF16)


def _rot_cols(w):
    half = ROPE_DIM // 2
    return jnp.concatenate([-w[..., half:], w[..., :half]], axis=-1)


def _pad_lanes(w, n):
    return jnp.pad(w, [(0, 0)] * (w.ndim - 1) + [(0, n - w.shape[-1])])


def _mla_weights(m, mla_w_down, mla_g_q_lat, mla_g_kv_lat, mla_w_uq, mla_w_uk, mla_w_uv, mla_g_qn, mla_g_kn):
    wd = mla_w_down[m]
    o = Q_RANK + KV_RANK
    wkr = wd[:, o:]
    wd_ext = jnp.concatenate([wd[:, :o], _pad_lanes(wkr, 128), _pad_lanes(_rot_cols(wkr), 128)], axis=1)
    wq = mla_w_uq[m].reshape(Q_RANK, MLA_HEADS, QK_DIM)
    wqa = _pad_lanes(wq, QK_PAD).reshape(Q_RANK, MLA_HEADS * QK_PAD)
    wqb = _pad_lanes(_rot_cols(wq[..., NOPE_DIM:]), 128).reshape(Q_RANK, MLA_HEADS * 128)
    return dict(
        wd=wd_ext.astype(BF16), gq=mla_g_q_lat[m].reshape(1, Q_RANK), gkv=mla_g_kv_lat[m].reshape(1, KV_RANK),
        wqa=wqa.astype(BF16), wqb=wqb.astype(BF16),
        wuk=mla_w_uk[m].reshape(KV_RANK, MLA_HEADS * NOPE_DIM).astype(BF16),
        wuv=mla_w_uv[m].reshape(KV_RANK, MLA_HEADS * V_DIM).astype(BF16),
        gqn=_pad_lanes(mla_g_qn[m], QK_PAD).reshape(1, QK_PAD),
        gkn=_pad_lanes(mla_g_kn[m], QK_PAD).reshape(1, QK_PAD),
    )


def _rope_tables(pos):
    inv = ROPE_THETA ** (-jnp.arange(0, ROPE_DIM, 2, dtype=F32) / ROPE_DIM)
    ang = pos.astype(F32)[:, None] * inv[None, :]
    cos, sin = jnp.cos(ang), jnp.sin(ang)
    return (_pad_lanes(jnp.concatenate([cos, cos], axis=-1), 128),
            _pad_lanes(jnp.concatenate([sin, sin], axis=-1), 128))


def _mla_proj(x, g, w, ctab, stab, tm, tab_index):
    n = x.shape[0]
    full = lambda shape: pl.BlockSpec(shape, lambda i: (0,) * len(shape))
    tspec = pl.BlockSpec((tm, 128), lambda i: (tab_index(i), 0))
    return pl.pallas_call(
        _mla_proj_kernel,
        grid=(n // tm,),
        in_specs=[
            pl.BlockSpec((tm, D_MODEL), lambda i: (i, 0)),
            full((1, D_MODEL)), full(w["wd"].shape), full((1, Q_RANK)), full((1, KV_RANK)),
            full(w["wqa"].shape), full(w["wqb"].shape), full(w["wuk"].shape), full(w["wuv"].shape),
            full((1, QK_PAD)), full((1, QK_PAD)), tspec, tspec,
        ],
        out_specs=[
            pl.BlockSpec((MLA_HEADS, tm, QK_PAD), lambda i: (0, i, 0)),
            pl.BlockSpec((tm, KV_RANK), lambda i: (i, 0)),
            pl.BlockSpec((tm, ROPE_DIM), lambda i: (i, 0)),
            pl.BlockSpec((MLA_HEADS, tm, QK_PAD), lambda i: (0, i, 0)),
            pl.BlockSpec((MLA_HEADS, tm, V_DIM), lambda i: (0, i, 0)),
        ],
        out_shape=[
            jax.ShapeDtypeStruct((MLA_HEADS, n, QK_PAD), BF16),
            jax.ShapeDtypeStruct((n, KV_RANK), F32),
            jax.ShapeDtypeStruct((n, ROPE_DIM), F32),
            jax.ShapeDtypeStruct((MLA_HEADS, n, QK_PAD), BF16),
            jax.ShapeDtypeStruct((MLA_HEADS, n, V_DIM), BF16),
        ],
        compiler_params=_cparams(("parallel",)),
        name="mla_proj",
    )(x, g.reshape(1, D_MODEL), w["wd"], w["gq"], w["gkv"], w["wqa"], w["wqb"], w["wuk"], w["wuv"],
      w["gqn"], w["gkn"], ctab, stab)


def _prompt_attn_kernel(q_ref, k_ref, v_ref, o_ref, *, tq):
    l = q_ref.shape[1]
    nq = l // tq
    row = lax.broadcasted_iota(jnp.int32, (tq, tq), 0)
    col = lax.broadcasted_iota(jnp.int32, (tq, tq), 1)
    diag_mask = row >= col
    for qi in range(nq):
        q = q_ref[0, qi * tq:(qi + 1) * tq, :]
        m = l_sum = acc = None
        for ki in range(qi + 1):
            k = k_ref[0, ki * tq:(ki + 1) * tq, :]
            v = v_ref[0, ki * tq:(ki + 1) * tq, :]
            s = _dot_nt(q, k) * ATTN_SCALE
            if ki == qi:
                s = jnp.where(diag_mask, s, -jnp.inf)
            m_blk = jnp.max(s, axis=-1, keepdims=True)
            if ki == 0:
                m = m_blk
                p = jnp.exp(s - m)
                l_sum = jnp.sum(p, axis=-1, keepdims=True)
                acc = _dot(p.astype(BF16), v)
            else:
                m_new = jnp.maximum(m, m_blk)
                alpha = jnp.exp(m - m_new)
                p = jnp.exp(s - m_new)
                l_sum = alpha * l_sum + jnp.sum(p, axis=-1, keepdims=True)
                acc = alpha * acc + _dot(p.astype(BF16), v)
                m = m_new
        o_ref[qi * tq:(qi + 1) * tq, :] = (acc / l_sum).astype(o_ref.dtype)


def _prompt_attn(q, k, v, nb, l):
    tq = _pick(l, (512, 256, 128))
    return pl.pallas_call(
        functools.partial(_prompt_attn_kernel, tq=tq),
        grid=(nb, MLA_HEADS),
        in_specs=[
            pl.BlockSpec((1, l, QK_PAD), lambda b, h: (h, b, 0)),
            pl.BlockSpec((1, l, QK_PAD), lambda b, h: (h, b, 0)),
            pl.BlockSpec((1, l, V_DIM), lambda b, h: (h, b, 0)),
        ],
        out_specs=pl.BlockSpec((l, V_DIM), lambda b, h: (b, h)),
        out_shape=jax.ShapeDtypeStruct((nb * l, MLA_HEADS * V_DIM), BF16),
        compiler_params=_cparams(("parallel", "parallel")),
        name="prompt_attn",
    )(q, k, v)


def _sample_attn_kernel(pt_ref, q_ref, cn_ref, krn_ref, wukt_ref, wuk_ref, wuv_ref, gkn_ref, *rest, pg, tc):
    cpages = rest[:pg]
    kpages = rest[pg:2 * pg]
    o_ref = rest[2 * pg]
    wq_ref, qr_ref, m_ref, l_ref, acc_ref, cb_ref, krt_ref = rest[2 * pg + 1:]
    del pt_ref
    g = pl.program_id(1)
    ng = pl.num_programs(1)
    nq = q_ref.shape[1] // MLA_HEADS
    rows = q_ref.shape[1]
    n_nope = MLA_HEADS * NOPE_DIM

    @pl.when(g == 0)
    def _():
        gkn = gkn_ref[...]
        qts, qrs = [], []
        for h in range(MLA_HEADS):
            qh = q_ref[0, h * nq:(h + 1) * nq, :].astype(F32)
            qg = (qh[:, :NOPE_DIM] * gkn[:, :NOPE_DIM]).astype(BF16)
            qts.append(_dot_nt(qg, wuk_ref[:, h * NOPE_DIM:(h + 1) * NOPE_DIM]))
            qrs.append(qh[:, NOPE_DIM:QK_DIM] * gkn[:, NOPE_DIM:QK_DIM])
        wq_ref[0:n_nope, :] = wukt_ref[...]
        wq_ref[n_nope:n_nope + rows, :] = jnp.concatenate(qts, axis=0).astype(BF16)
        qr_ref[...] = jnp.concatenate(qrs, axis=0).astype(BF16)
        m_ref[...] = jnp.full(m_ref.shape, -jnp.inf, F32)
        l_ref[...] = jnp.zeros(l_ref.shape, F32)
        acc_ref[...] = jnp.zeros(acc_ref.shape, F32)

    def scores(cbc, ssr, s_rope):
        t = cbc.shape[0]
        big = _dot_nt(wq_ref[...], cbc)
        rs = []
        for h in range(MLA_HEADS):
            kh = big[h * NOPE_DIM:(h + 1) * NOPE_DIM]
            ssq = jnp.sum(kh * kh, axis=0, keepdims=True)
            rs_h = lax.rsqrt((ssq + ssr) * (1.0 / QK_DIM) + RMS_EPS)
            rs.append(jnp.broadcast_to(rs_h, (nq, t)))
        return (big[n_nope:] + s_rope) * ATTN_SCALE * jnp.concatenate(rs, axis=0)

    def update(s_list, c_list):
        m_old = m_ref[...]
        m_new = m_old
        for s in s_list:
            m_new = jnp.maximum(m_new, jnp.max(s, axis=-1, keepdims=True))
        alpha = jnp.exp(m_old - m_new)
        ps = [jnp.exp(s - m_new) for s in s_list]
        l_new = alpha * l_ref[...]
        for p in ps:
            l_new = l_new + jnp.sum(p, axis=-1, keepdims=True)
        acc = alpha * acc_ref[...]
        for p, cbc in zip(ps, c_list):
            acc = acc + _dot(p.astype(BF16), cbc)
        l_ref[...] = l_new
        acc_ref[...] = acc
        m_ref[...] = m_new

    for j in range(pg):
        cb_ref[j * PAGE_SIZE:(j + 1) * PAGE_SIZE, :] = cpages[j][0, 0].astype(BF16)
        krt_ref[:, j * PAGE_SIZE:(j + 1) * PAGE_SIZE] = kpages[j][0, 0]
    s_list, c_list = [], []
    for ci in range(pg * PAGE_SIZE // tc):
        cbc = cb_ref[ci * tc:(ci + 1) * tc, :]
        krt = krt_ref[:, ci * tc:(ci + 1) * tc]
        ssr = jnp.sum(krt * krt, axis=0, keepdims=True)
        s_list.append(scores(cbc, ssr, _dot(qr_ref[...], krt.astype(BF16))))
        c_list.append(cbc)
    update(s_list, c_list)

    @pl.when(g == ng - 1)
    def _():
        pad = PAGE_SIZE - nq
        cn = jnp.concatenate([cn_ref[...], jnp.zeros((pad, KV_RANK), F32)], axis=0).astype(BF16)
        krn = jnp.concatenate([krn_ref[...], jnp.zeros((pad, ROPE_DIM), F32)], axis=0)
        ssr = _dot_nt(jnp.ones((8, ROPE_DIM), BF16), (krn * krn).astype(BF16))[0:1]
        s = scores(cn, ssr, _dot_nt(qr_ref[...], krn.astype(BF16)))
        qpos = lax.broadcasted_iota(jnp.int32, (rows, PAGE_SIZE), 0) % nq
        tok = lax.broadcasted_iota(jnp.int32, (rows, PAGE_SIZE), 1)
        update([jnp.where(tok <= qpos, s, -jnp.inf)], [cn])
        o_lat = (acc_ref[...] / l_ref[...]).astype(BF16)
        for h in range(MLA_HEADS):
            o_ref[:, h * V_DIM:(h + 1) * V_DIM] = _dot(o_lat[h * nq:(h + 1) * nq], wuv_ref[:, h * V_DIM:(h + 1) * V_DIM])


def _sample_attn(q_s, c_all, kr_all, row0, cache_c, cache_kr, m, page_table, wukt, wuk, wuv, gkn):
    ns, n_pages = page_table.shape
    rows = q_s.shape[1]
    nq = rows // MLA_HEADS
    assert row0 % nq == 0 and nq % 8 == 0
    pg = min(16, n_pages)
    assert n_pages % pg == 0 and pg % 2 == 0
    tc = 2 * PAGE_SIZE
    blk0 = row0 // nq
    full = lambda shape: pl.BlockSpec(shape, lambda b, g, pt: (0,) * len(shape))

    def page_spec(j, shape):
        return pl.BlockSpec((1, 1) + shape, lambda b, g, pt: (m, pt[b * n_pages + g * pg + j], 0, 0))

    cache_kr_t = jnp.swapaxes(cache_kr, 2, 3)
    grid_spec = pltpu.PrefetchScalarGridSpec(
        num_scalar_prefetch=1,
        grid=(ns, n_pages // pg),
        in_specs=[
            pl.BlockSpec((1, rows, QK_PAD), lambda b, g, pt: (b, 0, 0)),
            pl.BlockSpec((nq, KV_RANK), lambda b, g, pt: (blk0 + b, 0)),
            pl.BlockSpec((nq, ROPE_DIM), lambda b, g, pt: (blk0 + b, 0)),
            full(wukt.shape), full(wuk.shape), full(wuv.shape), full((1, QK_PAD)),
        ] + [page_spec(j, (PAGE_SIZE, KV_RANK)) for j in range(pg)]
          + [page_spec(j, (ROPE_DIM, PAGE_SIZE)) for j in range(pg)],
        out_specs=pl.BlockSpec((nq, MLA_HEADS * V_DIM), lambda b, g, pt: (b, 0)),
        scratch_shapes=[
            pltpu.VMEM((MLA_HEADS * NOPE_DIM + rows, KV_RANK), BF16), pltpu.VMEM((rows, ROPE_DIM), BF16),
            pltpu.VMEM((rows, 1), F32), pltpu.VMEM((rows, 1), F32), pltpu.VMEM((rows, KV_RANK), F32),
            pltpu.VMEM((pg * PAGE_SIZE, KV_RANK), BF16), pltpu.VMEM((ROPE_DIM, pg * PAGE_SIZE), F32),
        ],
    )
    return pl.pallas_call(
        functools.partial(_sample_attn_kernel, pg=pg, tc=tc),
        grid_spec=grid_spec,
        out_shape=jax.ShapeDtypeStruct((ns * nq, MLA_HEADS * V_DIM), F32),
        compiler_params=_cparams(("arbitrary", "arbitrary")),
        name="sample_attn",
    )(page_table.reshape(-1), q_s, c_all, kr_all, wukt, wuk, wuv, gkn,
      *([cache_c] * pg), *([cache_kr_t] * pg))


def _softplus(y):
    return jnp.maximum(y, 0.0) + jnp.log(1.0 + jnp.exp(-jnp.abs(y)))


def _rwkv_proj_kernel(*refs, sample, has_vres, tiles_per_seq, nq):
    it = iter(refs)
    x_ref, g_ref, mu_ref = next(it), next(it), next(it)
    shift_ref = next(it) if sample else None
    wr, wk, wv, w0, w1, w2, a0, a1, a2, g1, g2 = (next(it) for _ in range(11))
    if has_vres:
        v0, v1, v2, vf_ref = (next(it) for _ in range(4))
    r_ref, k_ref, v_ref, lw_ref, alr_ref, gate_ref, hl_ref = (next(it) for _ in range(7))
    carry_ref = next(it)
    tm = x_ref.shape[0]
    i = pl.program_id(0)

    h = _rms(x_ref[...], g_ref[...])
    rolled = pltpu.roll(h, 1, axis=0)
    row = lax.broadcasted_iota(jnp.int32, (tm, 1), 0)
    if sample:
        prev = jnp.where(row % nq == 0, shift_ref[...], rolled)
        hl_ref[...] = h
    else:
        @pl.when(i == 0)
        def _():
            carry_ref[...] = jnp.zeros(carry_ref.shape, F32)

        first = jnp.where(i % tiles_per_seq == 0, 0.0, carry_ref[0:1, :])
        prev = jnp.where(row == 0, first, rolled)
        carry_ref[0:1, :] = h[tm - 1:tm, :]
        hl_ref[...] = h[tm - 8:tm, :]
    xx = prev - h
    mu = mu_ref[...]
    mix = lambda n: (h + xx * mu[n:n + 1, :]).astype(BF16)
    xr, xw, xk, xv, xa, xg = (mix(n) for n in range(6))
    r = _dot(xr, wr[...])
    k = _dot(xk, wk[...])
    v = _dot(xv, wv[...])
    wl = w0[...] + _dot(jnp.tanh(_dot(xw, w1[...])).astype(BF16), w2[...])
    lw = -jnp.exp(-_softplus(-wl) - 0.5)
    alr = jax.nn.sigmoid(a0[...] + _dot(_dot(xa, a1[...]).astype(BF16), a2[...]))
    gate_ref[...] = _dot(jax.nn.sigmoid(_dot(xg, g1[...])).astype(BF16), g2[...])
    if has_vres:
        vgate = jax.nn.sigmoid(v0[...] + _dot(_dot(xv, v1[...]).astype(BF16), v2[...]))
    for p in range(HEAD_PAIRS):
        sl = slice(p * PAIR_W, (p + 1) * PAIR_W)
        vp = v[:, sl]
        if has_vres:
            vp = vp + (vf_ref[p] - vp) * vgate[:, sl]
        r_ref[p] = r[:, sl]
        k_ref[p] = k[:, sl]
        v_ref[p] = vp
        lw_ref[p] = lw[:, sl]
        alr_ref[p] = alr[:, sl]


def _rwkv_weights(j, rwkv_mu, rwkv_w_r, rwkv_w_k, rwkv_w_v, rwkv_w0, rwkv_w1, rwkv_w2, rwkv_a0, rwkv_a1,
                  rwkv_a2, rwkv_v0, rwkv_v1, rwkv_v2, rwkv_g1, rwkv_g2):
    row = lambda a: a.reshape(1, D_MODEL)
    w = [rwkv_w_r[j].astype(BF16), rwkv_w_k[j].astype(BF16), rwkv_w_v[j].astype(BF16),
         row(rwkv_w0[j]), rwkv_w1[j].astype(BF16), rwkv_w2[j].astype(BF16),
         row(rwkv_a0[j]), rwkv_a1[j].astype(BF16), rwkv_a2[j].astype(BF16),
         rwkv_g1[j].astype(BF16), rwkv_g2[j].astype(BF16)]
    if j > 0:
        w += [row(rwkv_v0[j - 1]), rwkv_v1[j - 1].astype(BF16), rwkv_v2[j - 1].astype(BF16)]
    return rwkv_mu[j], w


def _rwkv_proj(x, row0, n, g, mu, weights, v_first, shift_exp, seq_len):
    sample = shift_exp is not None
    has_vres = v_first is not None
    tm = _pick(n if sample else seq_len, (256, 128, 64, 32, 16, 8))
    assert row0 % tm == 0 and n % tm == 0
    off = row0 // tm
    if sample:
        assert tm % seq_len == 0
        tiles_per_seq = 1
    else:
        assert seq_len % tm == 0
        tiles_per_seq = seq_len // tm
    full = lambda a: pl.BlockSpec(a.shape, lambda i: (0,) * a.ndim)
    pair_spec = pl.BlockSpec((HEAD_PAIRS, tm, PAIR_W), lambda i: (0, i, 0))
    row_spec = pl.BlockSpec((tm, D_MODEL), lambda i: (i, 0))
    g2d = g.reshape(1, D_MODEL)
    args = [x, g2d, mu]
    specs = [pl.BlockSpec((tm, D_MODEL), lambda i: (i + off, 0)), full(g2d), full(mu)]
    if sample:
        args.append(shift_exp)
        specs.append(row_spec)
    args += weights
    specs += [full(a) for a in weights]
    if has_vres:
        args.append(v_first)
        specs.append(pair_spec)
    pair_shape = jax.ShapeDtypeStruct((HEAD_PAIRS, n, PAIR_W), F32)
    if sample:
        hl_spec, hl_shape = row_spec, jax.ShapeDtypeStruct((n, D_MODEL), F32)
    else:
        hl_spec = pl.BlockSpec((8, D_MODEL), lambda i: (i, 0))
        hl_shape = jax.ShapeDtypeStruct((n // tm * 8, D_MODEL), F32)
    return pl.pallas_call(
        functools.partial(_rwkv_proj_kernel, sample=sample, has_vres=has_vres,
                          tiles_per_seq=tiles_per_seq, nq=seq_len),
        grid=(n // tm,),
        in_specs=specs,
        out_specs=[pair_spec] * 5 + [row_spec, hl_spec],
        out_shape=[pair_shape] * 5 + [jax.ShapeDtypeStruct((n, D_MODEL), F32), hl_shape],
        scratch_shapes=[pltpu.VMEM((8, D_MODEL), F32)],
        compiler_params=_cparams(("arbitrary",)),
        name="rwkv_proj",
    )(*args)


def _wkv_masks(c, seg):
    c2 = 2 * c
    ri = lax.broadcasted_iota(jnp.int32, (c2, c2), 0)
    cj = lax.broadcasted_iota(jnp.int32, (c2, c2), 1)
    t, s = ri % c, cj % c
    same = ((ri // c) == (cj // c)) & ((t // seg) == (s // seg))
    levels = []
    step = 1
    while step < seg:
        levels.append(same & ((t ^ s) < 2 * step) & ((t & step) != 0) & ((s & step) == 0))
        step *= 2
    ti = lax.broadcasted_iota(jnp.int32, (c, c), 0)
    tj = lax.broadcasted_iota(jnp.int32, (c, c), 1)
    same_seq = (ti // seg) == (tj // seg)
    lane = lax.broadcasted_iota(jnp.int32, (1, PAIR_W), 1)
    row = lax.broadcasted_iota(jnp.int32, (c2, 1), 0)
    ei = lax.broadcasted_iota(jnp.int32, (PAIR_W, PAIR_W), 0)
    ej = lax.broadcasted_iota(jnp.int32, (PAIR_W, PAIR_W), 1)
    return dict(
        strict=same & (t > s), incl=same & (t >= s), levels=levels,
        tri=jnp.where(same_seq & (ti >= tj), 1.0, 0.0).astype(BF16),
        seq_ones=jnp.where(same_seq, 1.0, 0.0).astype(BF16),
        m0=lane < HEAD_SIZE, own=(row < c) == (lane < HEAD_SIZE), eye=ei == ej,
    )


def _wkv_block(chunks, prm, msk, h0, seg):
    kkp, kap, rkp, lnw, lnb = prm
    n = len(chunks)
    c = chunks[0][0].shape[0]
    c2 = 2 * c
    nseg = c // seg
    m0 = msk["m0"]
    tri = msk["tri"]
    b16 = lambda a: a.astype(BF16)
    cat0 = lambda a, b: jnp.concatenate([a, b], axis=0)
    cat1 = lambda a, b: jnp.concatenate([a, b], axis=1)

    def stack(a):
        return cat0(jnp.where(m0, a, 0.0), jnp.where(m0, 0.0, a))

    def seq_rows(a, g):
        return a if nseg == 1 else cat0(a[g * seg:(g + 1) * seg], a[c + g * seg:c + (g + 1) * seg])

    cums, tots = [], []
    for (_, _, _, lw, _) in chunks:
        hi = b16(lw)
        r1 = lw - hi.astype(F32)
        mid = b16(r1)
        lo = b16(r1 - mid.astype(F32))
        cum = _dot(tri, hi) + _dot(tri, mid) + _dot(tri, lo)
        cums.append(cum)
        if nseg == 1:
            tots.append(cum[c - 1:c, :])
        else:
            ones = msk["seq_ones"]
            tots.append(_dot(ones, hi) + _dot(ones, mid) + _dot(ones, lo))

    lhs, rhs, at, rt, bh, kh, v16, vs, bonus, gcol = ([] for _ in range(10))
    for (r, k, v, lw, alr), cum, tot in zip(chunks, cums, tots):
        e_in, e_ex = jnp.exp(cum), jnp.exp(cum - lw)
        e_inv, e_end = jnp.exp(-cum), jnp.exp(tot - cum)
        kk = stack(k * kkp)
        kk = kk * (1.0 / jnp.maximum(jnp.sqrt(jnp.sum(kk * kk, axis=-1, keepdims=True)), 1e-12))
        kmod = stack(k * (1.0 + (alr - 1.0) * kap))
        r_s, v_s = stack(r), stack(v)
        bb = kk * stack(alr)
        a_t, r_t = -kk * cat0(e_ex, e_ex), r_s * cat0(e_in, e_in)
        lhs.append(b16(cat0(a_t, r_t)))
        rhs.append(b16(cat0(bb * cat0(e_inv, e_inv), kmod * cat0(e_inv, e_inv))))
        at.append(a_t)
        rt.append(r_t)
        bh.append(bb * cat0(e_end, e_end))
        kh.append(kmod * cat0(e_end, e_end))
        v16.append(b16(v_s))
        vs.append(v_s)
        bonus.append(jnp.sum(r_s * kmod * rkp, axis=-1, keepdims=True) * v_s)
        gcol.append([jnp.sum(jnp.where(msk["eye"], jnp.exp(tot[g * seg:g * seg + 1, :]), 0.0), axis=1, keepdims=True)
                     for g in range(nseg)])

    gram = [_dot_nt(a, b) for a, b in zip(lhs, rhs)]
    ab = [jnp.where(msk["strict"], g[:c2, :c2], 0.0) for g in gram]
    ak = [b16(jnp.where(msk["strict"], g[:c2, c2:], 0.0)) for g in gram]
    rb = [b16(jnp.where(msk["incl"], g[c2:, :c2], 0.0)) for g in gram]
    rk = [b16(jnp.where(msk["incl"], g[c2:, c2:], 0.0)) for g in gram]
    akv = [_dot(a, v) for a, v in zip(ak, v16)]
    rkv = [_dot(a, v) for a, v in zip(rk, v16)]
    khv = [[_dot_tn(b16(seq_rows(a, g)), b16(seq_rows(v, g))) for g in range(nseg)] for a, v in zip(kh, vs)]

    levels = msk["levels"]
    e = [jnp.where(levels[0], a, 0.0) for a in ab]
    for lvl in levels[1:]:
        low = [jnp.where(lvl, a, 0.0) for a in ab]
        x = [lo_ + _dot(b16(e_), b16(lo_)) for e_, lo_ in zip(e, low)]
        e = [e_ + x_ + _dot(b16(x_), b16(e_)) for e_, x_ in zip(e, x)]

    w2 = [cat1(a, b) for a, b in zip(akv, at)]
    u2 = [w + _dot(b16(e_), b16(w)) for w, e_ in zip(w2, e)]
    y2 = [_dot(a, b16(u)) + cat1(b, r_) for a, u, b, r_ in zip(rb, u2, rkv, rt)]
    mj = [[_dot_tn(b16(seq_rows(a, g)), b16(seq_rows(u, g))) for g in range(nseg)] for a, u in zip(bh, u2)]

    y_st, h_out = [], []
    h = h0
    for i in range(n):
        parts = []
        for g in range(nseg):
            h_in = h if nseg == 1 else h0[i][g]
            h16 = b16(h_in)
            y2g = seq_rows(y2[i], g)
            parts.append(y2g[:, :PAIR_W] + _dot(b16(y2g[:, PAIR_W:]), h16))
            h = gcol[i][g] * h_in + _dot(b16(mj[i][g][:, PAIR_W:]), h16) + (mj[i][g][:, :PAIR_W] + khv[i][g])
            h_out.append(h)
        if nseg == 1:
            y_st.append(parts[0])
        else:
            y_st.append(jnp.concatenate([p[:seg] for p in parts] + [p[seg:] for p in parts], axis=0))

    own = msk["own"]
    inv_n = 1.0 / HEAD_SIZE
    zs = []
    for y, bo in zip(y_st, bonus):
        mean = jnp.sum(y, axis=-1, keepdims=True) * inv_n
        d = jnp.where(own, y - mean, 0.0)
        var = jnp.sum(d * d, axis=-1, keepdims=True) * inv_n
        z_st = d * lax.rsqrt(var + LNX_EPS) * lnw + jnp.where(own, lnb, 0.0) + bo
        zs.append(z_st[:c] + z_st[c:])
    return zs, h_out


def _state_in(s_ref, i):
    z = jnp.zeros((HEAD_SIZE, HEAD_SIZE), F32)
    top = jnp.concatenate([s_ref[i, 0], z], axis=1)
    bot = jnp.concatenate([z, s_ref[i, 1]], axis=1)
    return jnp.concatenate([top, bot], axis=0).T


def _state_out(s_ref, i, hbd):
    ht = hbd.T
    s_ref[i, 0] = ht[:HEAD_SIZE, :HEAD_SIZE]
    s_ref[i, 1] = ht[HEAD_SIZE:, HEAD_SIZE:]


def _wkv_kernel(*refs, c, seg, nb, blocks_per_seq, per_seq_state):
    data = refs[:5]
    prm = tuple(ref[0] for ref in refs[5:10])
    msk = _wkv_masks(c, seg)
    chunks = [tuple(ref[0, ci * c:(ci + 1) * c, :] for ref in data) for ci in range(nb)]
    if per_seq_state:
        s0_ref, z_ref, sout_ref = refs[10:]
        nseg = c // seg
        h0 = [[_state_in(s0_ref, ci * nseg + g) for g in range(nseg)] for ci in range(nb)]
        zs, hs = _wkv_block(chunks, prm, msk, h0, seg)
        for si in range(nb * nseg):
            _state_out(sout_ref, si, hs[si])
    else:
        z_ref, sout_ref, h_ref = refs[10:]
        b = pl.program_id(1)

        @pl.when(b % blocks_per_seq == 0)
        def _():
            h_ref[...] = jnp.zeros(h_ref.shape, F32)

        zs, hs = _wkv_block(chunks, prm, msk, h_ref[...], c)
        h_ref[...] = hs[-1]

        @pl.when(b % blocks_per_seq == blocks_per_seq - 1)
        def _():
            _state_out(sout_ref, 0, hs[-1])
    for ci in range(nb):
        z_ref[ci * c:(ci + 1) * c, :] = zs[ci]


def _pair_rows(a):
    return a.reshape(HEAD_PAIRS, 1, PAIR_W)


def _wkv(r, k, v, lw, alr, params, s0, n_seq, seq_len):
    c = WKV_CHUNK
    per_seq_state = s0 is not None
    n_chunks = n_seq * seq_len // c
    if per_seq_state:
        assert c % seq_len == 0 and (n_seq * seq_len) % c == 0
        seg, nb, blocks_per_seq = seq_len, _pick(n_chunks, (8, 4, 2, 1)), 1
        seqs = nb * (c // seg)
        sspec = pl.BlockSpec((seqs, 2, HEAD_SIZE, HEAD_SIZE), lambda p, b: (b, p, 0, 0))
        extra_in, extra_args, scratch = [sspec], [s0], []
    else:
        assert seq_len % c == 0
        seg, nb = c, _pick(seq_len // c, (8, 4, 2, 1))
        blocks_per_seq = seq_len // c // nb
        sspec = pl.BlockSpec((1, 2, HEAD_SIZE, HEAD_SIZE), lambda p, b: (b // blocks_per_seq, p, 0, 0))
        extra_in, extra_args, scratch = [], [], [pltpu.VMEM((PAIR_W, PAIR_W), F32)]
    n_blocks = n_chunks // nb
    dspec = pl.BlockSpec((1, nb * c, PAIR_W), lambda p, b: (p, b, 0))
    pspec = pl.BlockSpec((1, 1, PAIR_W), lambda p, b: (p, 0, 0))
    return pl.pallas_call(
        functools.partial(_wkv_kernel, c=c, seg=seg, nb=nb, blocks_per_seq=blocks_per_seq,
                          per_seq_state=per_seq_state),
        grid=(HEAD_PAIRS, n_blocks),
        in_specs=[dspec] * 5 + [pspec] * 5 + extra_in,
        out_specs=[pl.BlockSpec((nb * c, PAIR_W), lambda p, b: (b, p)), sspec],
        out_shape=[jax.ShapeDtypeStruct((n_seq * seq_len, D_MODEL), F32),
                   jax.ShapeDtypeStruct((n_seq, RWKV_HEADS, HEAD_SIZE, HEAD_SIZE), F32)],
        scratch_shapes=scratch,
        compiler_params=_cparams(("parallel", "arbitrary")),
        name="wkv",
    )(r, k, v, lw, alr, *[_pair_rows(p) for p in params], *extra_args)


def kernel(x_prompt, x_sample, cache_kv_latent, cache_k_rope, state_wkv, state_shift, page_table,
           ffn_norm, ffn_w_gate, ffn_w_up, ffn_w_down, mix_norm,
           mla_w_down, mla_g_q_lat, mla_g_kv_lat, mla_w_uq, mla_w_uk, mla_w_uv, mla_g_qn, mla_g_kn, mla_w_o,
           rwkv_mu, rwkv_w_r, rwkv_w_k, rwkv_w_v, rwkv_w_o, rwkv_w0, rwkv_w1, rwkv_w2,
           rwkv_a0, rwkv_a1, rwkv_a2, rwkv_v0, rwkv_v1, rwkv_v2, rwkv_g1, rwkv_g2,
           rwkv_k_k, rwkv_k_a, rwkv_r_k, rwkv_lnx_w, rwkv_lnx_b):
    nb, l, _ = x_prompt.shape
    ns, nq, _ = x_sample.shape
    n_p, n_s = nb * l, ns * nq
    depth = ffn_norm.shape[0]
    past = page_table.shape[1] * PAGE_SIZE
    x = jnp.concatenate([x_prompt.reshape(n_p, D_MODEL), x_sample.reshape(n_s, D_MODEL)], axis=0)

    tm = _pick(n_s, (512, 256, 128, 64, 32, 16, 8))
    assert l % tm == 0 and tm % nq == 0
    cos_p, sin_p = _rope_tables(jnp.arange(l))
    cos_s, sin_s = _rope_tables(past + jnp.arange(nq))
    ctab = jnp.concatenate([cos_p, jnp.tile(cos_s, (tm // nq, 1))], axis=0)
    stab = jnp.concatenate([sin_p, jnp.tile(sin_s, (tm // nq, 1))], axis=0)
    n_ptiles, tiles_per_seq = n_p // tm, l // tm
    tab_index = lambda i: jnp.where(i < n_ptiles, i % tiles_per_seq, tiles_per_seq)

    lat_rows, rope_rows, p_wkv, p_shift, s_wkv, s_shift = [], [], [], [], [], []
    vf_p = vf_s = None
    for i in range(depth):
        x = _ffn(x, ffn_norm[i, 0], ffn_w_gate[i, 0].astype(BF16), ffn_w_up[i, 0].astype(BF16),
                 ffn_w_down[i, 0].astype(BF16))
        if i % 2 == 0:
            m = i // 2
            w = _mla_weights(m, mla_w_down, mla_g_q_lat, mla_g_kv_lat, mla_w_uq, mla_w_uk, mla_w_uv,
                             mla_g_qn, mla_g_kn)
            q, c, kr, k, v = _mla_proj(x, mix_norm[i], w, ctab, stab, tm, tab_index)
            o_p = _prompt_attn(q, k, v, nb, l)
            q_s = jnp.transpose(q[:, n_p:, :].reshape(MLA_HEADS, ns, nq, QK_PAD), (1, 0, 2, 3))
            q_s = q_s.reshape(ns, MLA_HEADS * nq, QK_PAD)
            o_s = _sample_attn(q_s, c, kr, n_p, cache_kv_latent, cache_k_rope, m, page_table,
                               jnp.transpose(w["wuk"]), w["wuk"], w["wuv"], w["gkn"])
            proj = (o_p, o_s, mla_w_o[m].astype(BF16), None, None)
            lat_rows.append(c)
            rope_rows.append(kr)
        else:
            j = i // 2
            mu, w = _rwkv_weights(j, rwkv_mu, rwkv_w_r, rwkv_w_k, rwkv_w_v, rwkv_w0, rwkv_w1, rwkv_w2,
                                  rwkv_a0, rwkv_a1, rwkv_a2, rwkv_v0, rwkv_v1, rwkv_v2, rwkv_g1, rwkv_g2)
            params = [rwkv_k_k[j], rwkv_k_a[j], rwkv_r_k[j], rwkv_lnx_w[j], rwkv_lnx_b[j]]
            r, k, v, lw, alr, gate_p, hl = _rwkv_proj(x, 0, n_p, mix_norm[i], mu, w, vf_p, None, l)
            if j == 0:
                vf_p = v
            z_p, st_p = _wkv(r, k, v, lw, alr, params, None, nb, l)
            p_wkv.append(st_p)
            p_shift.append(hl.reshape(nb, -1, 8, D_MODEL)[:, -1, 7])
            shift_exp = jnp.zeros((ns, nq, D_MODEL), F32).at[:, 0, :].set(state_shift[j]).reshape(n_s, D_MODEL)
            r, k, v, lw, alr, gate_s, h_s = _rwkv_proj(x, n_p, n_s, mix_norm[i], mu, w, vf_s, shift_exp, nq)
            if j == 0:
                vf_s = v
            z_s, st_s = _wkv(r, k, v, lw, alr, params, state_wkv[j], ns, nq)
            s_wkv.append(st_s)
            s_shift.append(h_s.reshape(ns, nq, D_MODEL)[:, -1])
            proj = (z_p, z_s, rwkv_w_o[j].astype(BF16), gate_p, gate_s)
        x = _ffn(x, ffn_norm[i, 1], ffn_w_gate[i, 1].astype(BF16), ffn_w_up[i, 1].astype(BF16),
                 ffn_w_down[i, 1].astype(BF16), proj)

    lat = jnp.stack(lat_rows)
    rope = jnp.stack(rope_rows)
    return (x[:n_p].reshape(nb, l, D_MODEL), x[n_p:].reshape(ns, nq, D_MODEL),
            lat[:, :n_p].reshape(-1, nb, l, KV_RANK), rope[:, :n_p].reshape(-1, nb, l, ROPE_DIM),
            jnp.stack(p_wkv), jnp.stack(p_shift),
            lat[:, n_p:].reshape(-1, ns, nq, KV_RANK), rope[:, n_p:].reshape(-1, ns, nq, ROPE_DIM),
            jnp.stack(s_wkv), jnp.stack(s_shift))
```

```python
import functools

import jax
import jax.numpy as jnp
from jax import lax
from jax.experimental import pallas as pl
from jax.experimental.pallas import tpu as pltpu

F32 = jnp.float32
BF16 = jnp.bfloat16

D_MODEL = 1024
D_FF = 2816
RMS_EPS = 1e-6
MLA_HEADS = 8
Q_RANK = 384
KV_RANK = 256
NOPE_DIM = 128
ROPE_DIM = 64
QK_DIM = NOPE_DIM + ROPE_DIM
QK_PAD = 256
V_DIM = 128
ROPE_THETA = 10000.0
ATTN_SCALE = QK_DIM ** -0.5
PAGE_SIZE = 128
HEAD_SIZE = 64
RWKV_HEADS = D_MODEL // HEAD_SIZE
HEAD_PAIRS = RWKV_HEADS // 2
PAIR_W = 2 * HEAD_SIZE
LNX_EPS = 64e-5
WKV_CHUNK = 64

VMEM_LIMIT = 48 * 1024 * 1024


def _cparams(sem):
    return pltpu.CompilerParams(dimension_semantics=sem, vmem_limit_bytes=VMEM_LIMIT)


def _pick(n, cands):
    for c in cands:
        if n % c == 0:
            return c
    raise ValueError(f"no tile in {cands} divides {n}")


def _rms(x, g):
    return x * lax.rsqrt(jnp.mean(x * x, axis=-1, keepdims=True) + RMS_EPS) * g


def _dot(a, b):
    return jnp.dot(a, b, preferred_element_type=F32)


def _dot_nt(a, b):
    return lax.dot_general(a, b, (((1,), (1,)), ((), ())), preferred_element_type=F32)


def _dot_tn(a, b):
    return lax.dot_general(a, b, (((0,), (0,)), ((), ())), preferred_element_type=F32)


FFN_CHUNK = 256


def _swiglu_res(x, g_ref, wg_ref, wu_ref, wd_ref):
    hb = _rms(x, g_ref[...]).astype(BF16)
    acc = x
    for c in range(D_FF // FFN_CHUNK):
        sl = slice(c * FFN_CHUNK, (c + 1) * FFN_CHUNK)
        a = _dot(hb, wg_ref[:, sl])
        u = _dot(hb, wu_ref[:, sl])
        act = (0.5 * a * jax.nn.sigmoid(a) * u).astype(BF16)
        acc = acc + _dot(act, wd_ref[sl, :])
    return acc


def _ffn_kernel(x_ref, g_ref, wg_ref, wu_ref, wd_ref, o_ref):
    o_ref[...] = _swiglu_res(x_ref[...], g_ref, wg_ref, wu_ref, wd_ref)


def _ffn_parts_kernel(*refs, npt, two_x, proj, gated):
    it = iter(refs)
    xp_ref = next(it)
    xs_ref = next(it) if two_x else xp_ref
    ap_ref = as_ref = gp_ref = gs_ref = wp_ref = None
    if proj:
        ap_ref, as_ref = next(it), next(it)
        if gated:
            gp_ref, gs_ref = next(it), next(it)
        wp_ref = next(it)
    g_ref, wg_ref, wu_ref, wd_ref = (next(it) for _ in range(4))
    o_ref = next(it)
    x1_ref = next(it)
    i = pl.program_id(0)

    def stage(x_ref, a_ref, gate_ref):
        x = x_ref[...]
        if proj:
            a = a_ref[...] if gate_ref is None else a_ref[...] * gate_ref[...]
            x = x + _dot(a.astype(BF16), wp_ref[...])
        x1_ref[...] = x

    @pl.when(i < npt)
    def _():
        stage(xp_ref, ap_ref, gp_ref)

    @pl.when(i >= npt)
    def _():
        stage(xs_ref, as_ref, gs_ref)

    o_ref[...] = _swiglu_res(x1_ref[...], g_ref, wg_ref, wu_ref, wd_ref)


def _resident(shape):
    return pl.BlockSpec(shape, lambda i: (0,) * len(shape), pipeline_mode=pl.Buffered(1))


def _ffn(x, g, wg, wu, wd, proj=None):
    two_x = isinstance(x, tuple)
    n = x[0].shape[0] + x[1].shape[0] if two_x else x.shape[0]
    wspecs = [_resident((1, D_MODEL)), _resident(wg.shape), _resident(wu.shape), _resident(wd.shape)]
    wargs = (g.reshape(1, D_MODEL), wg, wu, wd)
    cparams = pltpu.CompilerParams(dimension_semantics=("parallel",), vmem_limit_bytes=56 * 1024 * 1024)
    if not two_x and proj is None:
        tm = _pick(n, (512, 256, 128, 64, 32, 16, 8))
        xspec = pl.BlockSpec((tm, D_MODEL), lambda i: (i, 0))
        return pl.pallas_call(
            _ffn_kernel, grid=(n // tm,), in_specs=[xspec] + wspecs, out_specs=xspec,
            out_shape=jax.ShapeDtypeStruct((n, D_MODEL), F32), compiler_params=cparams, name="ffn",
        )(x, *wargs)

    if two_x:
        n_p, n_s = x[0].shape[0], x[1].shape[0]
    else:
        n_p, n_s = proj[0].shape[0], proj[1].shape[0]
    tm = _pick(n_s, (512, 256, 128, 64, 32, 16, 8))
    assert n == n_p + n_s and n_p % tm == 0
    npt = n_p // tm
    row = lambda width: pl.BlockSpec((tm, width), lambda i: (i, 0))
    prm = lambda width: pl.BlockSpec((tm, width), lambda i: (jnp.minimum(i, npt - 1), 0))
    smp = lambda width: pl.BlockSpec((tm, width), lambda i: (jnp.maximum(i - npt, 0), 0))
    specs, args = ([prm(D_MODEL), smp(D_MODEL)], list(x)) if two_x else ([row(D_MODEL)], [x])
    gated = False
    if proj is not None:
        a_p, a_s, w, gate_p, gate_s = proj
        kin = a_p.shape[1]
        gated = gate_p is not None
        specs += [prm(kin), smp(kin)] + ([prm(kin), smp(kin)] if gated else []) + [_resident(w.shape)]
        args += [a_p, a_s] + ([gate_p, gate_s] if gated else []) + [w]
    return pl.pallas_call(
        functools.partial(_ffn_parts_kernel, npt=npt, two_x=two_x, proj=proj is not None, gated=gated),
        grid=(n // tm,),
        in_specs=specs + wspecs,
        out_specs=row(D_MODEL),
        out_shape=jax.ShapeDtypeStruct((n, D_MODEL), F32),
        scratch_shapes=[pltpu.VMEM((tm, D_MODEL), F32)],
        compiler_params=cparams,
        name="ffn",
    )(*args, *wargs)


def _mla_proj_kernel(x_ref, g_ref, wd_ref, gq_ref, gkv_ref, wqa_ref, wqb_ref, wuk_ref, wuv_ref,
                     gqn_ref, gkn_ref, ct_ref, st_ref,
                     q_ref, c_ref, kr_ref, k_ref, v_ref):
    hb = _rms(x_ref[...], g_ref[...]).astype(BF16)
    lat = _dot(hb, wd_ref[...])
    cq = _rms(lat[:, :Q_RANK], gq_ref[...]).astype(BF16)
    c = _rms(lat[:, Q_RANK:Q_RANK + KV_RANK], gkv_ref[...])
    ct = ct_ref[...]
    st = st_ref[...]
    o = Q_RANK + KV_RANK
    kr = lat[:, o:o + 128] * ct + lat[:, o + 128:o + 256] * st
    c_ref[...] = c
    kr_ref[...] = kr[:, :ROPE_DIM]

    qa = _dot(cq, wqa_ref[...])
    qb = _dot(cq, wqb_ref[...])
    gqn = gqn_ref[...]
    gkn = gkn_ref[...]
    cb = c.astype(BF16)
    kn = _dot(cb, wuk_ref[...])
    vv = _dot(cb, wuv_ref[...])
    ssr = jnp.sum(kr * kr, axis=-1, keepdims=True)
    for h in range(MLA_HEADS):
        nope = qa[:, h * QK_PAD:h * QK_PAD + 128]
        rp = qa[:, h * QK_PAD + 128:(h + 1) * QK_PAD] * ct + qb[:, h * 128:(h + 1) * 128] * st
        ss = jnp.sum(nope * nope, axis=-1, keepdims=True) + jnp.sum(rp * rp, axis=-1, keepdims=True)
        rs = lax.rsqrt(ss * (1.0 / QK_DIM) + RMS_EPS)
        q_ref[h, :, 0:128] = (nope * rs * gqn[:, 0:128]).astype(BF16)
        q_ref[h, :, 128:256] = (rp * rs * gqn[:, 128:256]).astype(BF16)
        knh = kn[:, h * 128:(h + 1) * 128]
        rk = lax.rsqrt((jnp.sum(knh * knh, axis=-1, keepdims=True) + ssr) * (1.0 / QK_DIM) + RMS_EPS)
        k_ref[h, :, 0:128] = (knh * rk * gkn[:, 0:128]).astype(BF16)
        k_ref[h, :, 128:256] = (kr * rk * gkn[:, 128:256]).astype(BF16)
        v_ref[h] = vv[:, h * 128:(h + 1) * 128].astype(BF16)


def _rot_cols(w):
    half = ROPE_DIM // 2
    return jnp.concatenate([-w[..., half:], w[..., :half]], axis=-1)


def _pad_lanes(w, n):
    return jnp.pad(w, [(0, 0)] * (w.ndim - 1) + [(0, n - w.shape[-1])])


def _mla_weights(m, mla_w_down, mla_g_q_lat, mla_g_kv_lat, mla_w_uq, mla_w_uk, mla_w_uv, mla_g_qn, mla_g_kn):
    wd = mla_w_down[m]
    o = Q_RANK + KV_RANK
    wkr = wd[:, o:]
    wd_ext = jnp.concatenate([wd[:, :o], _pad_lanes(wkr, 128), _pad_lanes(_rot_cols(wkr), 128)], axis=1)
    wq = mla_w_uq[m].reshape(Q_RANK, MLA_HEADS, QK_DIM)
    wqa = _pad_lanes(wq, QK_PAD).reshape(Q_RANK, MLA_HEADS * QK_PAD)
    wqb = _pad_lanes(_rot_cols(wq[..., NOPE_DIM:]), 128).reshape(Q_RANK, MLA_HEADS * 128)
    return dict(
        wd=wd_ext.astype(BF16), gq=mla_g_q_lat[m].reshape(1, Q_RANK), gkv=mla_g_kv_lat[m].reshape(1, KV_RANK),
        wqa=wqa.astype(BF16), wqb=wqb.astype(BF16),
        wuk=mla_w_uk[m].reshape(KV_RANK, MLA_HEADS * NOPE_DIM).astype(BF16),
        wuv=mla_w_uv[m].reshape(KV_RANK, MLA_HEADS * V_DIM).astype(BF16),
        gqn=_pad_lanes(mla_g_qn[m] * ATTN_SCALE, QK_PAD).reshape(1, QK_PAD),
        gkn=_pad_lanes(mla_g_kn[m], QK_PAD).reshape(1, QK_PAD),
    )


def _rope_tables(pos):
    inv = ROPE_THETA ** (-jnp.arange(0, ROPE_DIM, 2, dtype=F32) / ROPE_DIM)
    ang = pos.astype(F32)[:, None] * inv[None, :]
    cos, sin = jnp.cos(ang), jnp.sin(ang)
    return (_pad_lanes(jnp.concatenate([cos, cos], axis=-1), 128),
            _pad_lanes(jnp.concatenate([sin, sin], axis=-1), 128))


def _mla_proj(x, g, w, ctab, stab, tm, tab_index):
    n = x.shape[0]
    full = lambda shape: pl.BlockSpec(shape, lambda i: (0,) * len(shape))
    tspec = pl.BlockSpec((tm, 128), lambda i: (tab_index(i), 0))
    return pl.pallas_call(
        _mla_proj_kernel,
        grid=(n // tm,),
        in_specs=[
            pl.BlockSpec((tm, D_MODEL), lambda i: (i, 0)),
            full((1, D_MODEL)), full(w["wd"].shape), full((1, Q_RANK)), full((1, KV_RANK)),
            full(w["wqa"].shape), full(w["wqb"].shape), full(w["wuk"].shape), full(w["wuv"].shape),
            full((1, QK_PAD)), full((1, QK_PAD)), tspec, tspec,
        ],
        out_specs=[
            pl.BlockSpec((MLA_HEADS, tm, QK_PAD), lambda i: (0, i, 0)),
            pl.BlockSpec((tm, KV_RANK), lambda i: (i, 0)),
            pl.BlockSpec((tm, ROPE_DIM), lambda i: (i, 0)),
            pl.BlockSpec((MLA_HEADS, tm, QK_PAD), lambda i: (0, i, 0)),
            pl.BlockSpec((MLA_HEADS, tm, V_DIM), lambda i: (0, i, 0)),
        ],
        out_shape=[
            jax.ShapeDtypeStruct((MLA_HEADS, n, QK_PAD), BF16),
            jax.ShapeDtypeStruct((n, KV_RANK), F32),
            jax.ShapeDtypeStruct((n, ROPE_DIM), F32),
            jax.ShapeDtypeStruct((MLA_HEADS, n, QK_PAD), BF16),
            jax.ShapeDtypeStruct((MLA_HEADS, n, V_DIM), BF16),
        ],
        compiler_params=_cparams(("parallel",)),
        name="mla_proj",
    )(x, g.reshape(1, D_MODEL), w["wd"], w["gq"], w["gkv"], w["wqa"], w["wqb"], w["wuk"], w["wuv"],
      w["gqn"], w["gkn"], ctab, stab)


def _prompt_attn_kernel(q_ref, k_ref, v_ref, o_ref, *, tq):
    l = q_ref.shape[1]
    nq = l // tq
    row = lax.broadcasted_iota(jnp.int32, (tq, tq), 0)
    col = lax.broadcasted_iota(jnp.int32, (tq, tq), 1)
    diag_mask = row >= col
    for qi in range(nq):
        q = q_ref[0, qi * tq:(qi + 1) * tq, :]
        m = l_sum = acc = None
        for ki in range(qi + 1):
            k = k_ref[0, ki * tq:(ki + 1) * tq, :]
            v = v_ref[0, ki * tq:(ki + 1) * tq, :]
            s = _dot_nt(q, k)
            if ki == qi:
                s = jnp.where(diag_mask, s, -jnp.inf)
            m_blk = jnp.max(s, axis=-1, keepdims=True)
            if ki == 0:
                m = m_blk
                p = jnp.exp(s - m)
                l_sum = jnp.sum(p, axis=-1, keepdims=True)
                acc = _dot(p.astype(BF16), v)
            else:
                m_new = jnp.maximum(m, m_blk)
                alpha = jnp.exp(m - m_new)
                p = jnp.exp(s - m_new)
                l_sum = alpha * l_sum + jnp.sum(p, axis=-1, keepdims=True)
                acc = alpha * acc + _dot(p.astype(BF16), v)
                m = m_new
        o_ref[qi * tq:(qi + 1) * tq, :] = (acc / l_sum).astype(o_ref.dtype)


def _prompt_attn(q, k, v, nb, l):
    tq = _pick(l, (512, 256, 128))
    return pl.pallas_call(
        functools.partial(_prompt_attn_kernel, tq=tq),
        grid=(nb, MLA_HEADS),
        in_specs=[
            pl.BlockSpec((1, l, QK_PAD), lambda b, h: (h, b, 0)),
            pl.BlockSpec((1, l, QK_PAD), lambda b, h: (h, b, 0)),
            pl.BlockSpec((1, l, V_DIM), lambda b, h: (h, b, 0)),
        ],
        out_specs=pl.BlockSpec((l, V_DIM), lambda b, h: (b, h)),
        out_shape=jax.ShapeDtypeStruct((nb * l, MLA_HEADS * V_DIM), BF16),
        compiler_params=_cparams(("parallel", "parallel")),
        name="prompt_attn",
    )(q, k, v)


def _sample_attn_kernel(pt_ref, q_ref, cn_ref, krn_ref, wukt_ref, wuk_ref, wuv_ref, gkn_ref, *rest, pg, tc):
    cpages = rest[:pg]
    kpages = rest[pg:2 * pg]
    o_ref = rest[2 * pg]
    wq_ref, qr_ref, m_ref, l_ref, acc_ref, cb_ref, krt_ref = rest[2 * pg + 1:]
    del pt_ref
    g = pl.program_id(1)
    ng = pl.num_programs(1)
    nq = q_ref.shape[1] // MLA_HEADS
    rows = q_ref.shape[1]
    n_nope = MLA_HEADS * NOPE_DIM

    @pl.when(g == 0)
    def _():
        gkn = gkn_ref[...]
        qts, qrs = [], []
        for h in range(MLA_HEADS):
            qh = q_ref[0, h * nq:(h + 1) * nq, :].astype(F32)
            qg = (qh[:, :NOPE_DIM] * gkn[:, :NOPE_DIM]).astype(BF16)
            qts.append(_dot_nt(qg, wuk_ref[:, h * NOPE_DIM:(h + 1) * NOPE_DIM]))
            qrs.append(qh[:, NOPE_DIM:QK_DIM] * gkn[:, NOPE_DIM:QK_DIM])
        wq_ref[0:n_nope, :] = wukt_ref[...]
        wq_ref[n_nope:n_nope + rows, :] = jnp.concatenate(qts, axis=0).astype(BF16)
        qr_ref[...] = jnp.concatenate(qrs, axis=0).astype(BF16)
        m_ref[...] = jnp.full(m_ref.shape, -jnp.inf, F32)
        l_ref[...] = jnp.zeros(l_ref.shape, F32)
        acc_ref[...] = jnp.zeros(acc_ref.shape, F32)

    def scores(cbc, ssr, s_rope):
        t = cbc.shape[0]
        big = _dot_nt(wq_ref[...], cbc)
        rs = []
        for h in range(MLA_HEADS):
            kh = big[h * NOPE_DIM:(h + 1) * NOPE_DIM]
            ssq = jnp.sum(kh * kh, axis=0, keepdims=True)
            rs_h = lax.rsqrt((ssq + ssr) * (1.0 / QK_DIM) + RMS_EPS)
            rs.append(jnp.broadcast_to(rs_h, (nq, t)))
        return (big[n_nope:] + s_rope) * jnp.concatenate(rs, axis=0)

    def update(s_list, c_list):
        m_old = m_ref[...]
        m_new = m_old
        for s in s_list:
            m_new = jnp.maximum(m_new, jnp.max(s, axis=-1, keepdims=True))
        alpha = jnp.exp(m_old - m_new)
        ps = [jnp.exp(s - m_new) for s in s_list]
        l_new = alpha * l_ref[...]
        for p in ps:
            l_new = l_new + jnp.sum(p, axis=-1, keepdims=True)
        acc = alpha * acc_ref[...]
        for p, cbc in zip(ps, c_list):
            acc = acc + _dot(p.astype(BF16), cbc)
        l_ref[...] = l_new
        acc_ref[...] = acc
        m_ref[...] = m_new

    for j in range(pg):
        cb_ref[j * PAGE_SIZE:(j + 1) * PAGE_SIZE, :] = cpages[j][0, 0].astype(BF16)
        krt_ref[:, j * PAGE_SIZE:(j + 1) * PAGE_SIZE] = kpages[j][0, 0]
    s_list, c_list = [], []
    for ci in range(pg * PAGE_SIZE // tc):
        cbc = cb_ref[ci * tc:(ci + 1) * tc, :]
        krt = krt_ref[:, ci * tc:(ci + 1) * tc]
        ssr = jnp.sum(krt * krt, axis=0, keepdims=True)
        s_list.append(scores(cbc, ssr, _dot(qr_ref[...], krt.astype(BF16))))
        c_list.append(cbc)
    update(s_list, c_list)

    @pl.when(g == ng - 1)
    def _():
        pad = PAGE_SIZE - nq
        cn = jnp.concatenate([cn_ref[...], jnp.zeros((pad, KV_RANK), F32)], axis=0).astype(BF16)
        krn = jnp.concatenate([krn_ref[...], jnp.zeros((pad, ROPE_DIM), F32)], axis=0)
        ssr = _dot_nt(jnp.ones((8, ROPE_DIM), BF16), (krn * krn).astype(BF16))[0:1]
        s = scores(cn, ssr, _dot_nt(qr_ref[...], krn.astype(BF16)))
        qpos = lax.broadcasted_iota(jnp.int32, (rows, PAGE_SIZE), 0) % nq
        tok = lax.broadcasted_iota(jnp.int32, (rows, PAGE_SIZE), 1)
        update([jnp.where(tok <= qpos, s, -jnp.inf)], [cn])
        o_lat = (acc_ref[...] / l_ref[...]).astype(BF16)
        for h in range(MLA_HEADS):
            o_ref[:, h * V_DIM:(h + 1) * V_DIM] = _dot(o_lat[h * nq:(h + 1) * nq], wuv_ref[:, h * V_DIM:(h + 1) * V_DIM])


def _sample_attn(q_s, c_all, kr_all, row0, cache_c, cache_kr, m, page_table, wukt, wuk, wuv, gkn):
    ns, n_pages = page_table.shape
    rows = q_s.shape[1]
    nq = rows // MLA_HEADS
    assert row0 % nq == 0 and nq % 8 == 0
    pg = min(64, n_pages)
    assert n_pages % pg == 0 and pg % 2 == 0
    tc = 2 * PAGE_SIZE
    blk0 = row0 // nq
    full = lambda shape: pl.BlockSpec(shape, lambda b, g, pt: (0,) * len(shape))

    def page_spec(j, shape):
        return pl.BlockSpec((1, 1) + shape, lambda b, g, pt: (m, pt[b * n_pages + g * pg + j], 0, 0))

    cache_kr_t = jnp.swapaxes(cache_kr, 2, 3)
    grid_spec = pltpu.PrefetchScalarGridSpec(
        num_scalar_prefetch=1,
        grid=(ns, n_pages // pg),
        in_specs=[
            pl.BlockSpec((1, rows, QK_PAD), lambda b, g, pt: (b, 0, 0)),
            pl.BlockSpec((nq, KV_RANK), lambda b, g, pt: (blk0 + b, 0)),
            pl.BlockSpec((nq, ROPE_DIM), lambda b, g, pt: (blk0 + b, 0)),
            full(wukt.shape), full(wuk.shape), full(wuv.shape), full((1, QK_PAD)),
        ] + [page_spec(j, (PAGE_SIZE, KV_RANK)) for j in range(pg)]
          + [page_spec(j, (ROPE_DIM, PAGE_SIZE)) for j in range(pg)],
        out_specs=pl.BlockSpec((nq, MLA_HEADS * V_DIM), lambda b, g, pt: (b, 0)),
        scratch_shapes=[
            pltpu.VMEM((MLA_HEADS * NOPE_DIM + rows, KV_RANK), BF16), pltpu.VMEM((rows, ROPE_DIM), BF16),
            pltpu.VMEM((rows, 1), F32), pltpu.VMEM((rows, 1), F32), pltpu.VMEM((rows, KV_RANK), F32),
            pltpu.VMEM((pg * PAGE_SIZE, KV_RANK), BF16), pltpu.VMEM((ROPE_DIM, pg * PAGE_SIZE), F32),
        ],
    )
    return pl.pallas_call(
        functools.partial(_sample_attn_kernel, pg=pg, tc=tc),
        grid_spec=grid_spec,
        out_shape=jax.ShapeDtypeStruct((ns * nq, MLA_HEADS * V_DIM), F32),
        compiler_params=_cparams(("arbitrary", "arbitrary")),
        name="sample_attn",
    )(page_table.reshape(-1), q_s, c_all, kr_all, wukt, wuk, wuv, gkn,
      *([cache_c] * pg), *([cache_kr_t] * pg))


def _softplus(y):
    return jnp.maximum(y, 0.0) + jnp.log(1.0 + jnp.exp(-jnp.abs(y)))


def _rwkv_proj_kernel(*refs, sample, has_vres, tiles_per_seq, nq):
    it = iter(refs)
    x_ref, g_ref, mu_ref = next(it), next(it), next(it)
    shift_ref = next(it) if sample else None
    wr, wk, wv, w0, w1, w2, a0, a1, a2, g1, g2 = (next(it) for _ in range(11))
    if has_vres:
        v0, v1, v2, vf_ref = (next(it) for _ in range(4))
    r_ref, k_ref, v_ref, lw_ref, alr_ref, gate_ref, hl_ref = (next(it) for _ in range(7))
    carry_ref = next(it)
    tm = x_ref.shape[0]
    i = pl.program_id(0)

    h = _rms(x_ref[...], g_ref[...])
    rolled = pltpu.roll(h, 1, axis=0)
    row = lax.broadcasted_iota(jnp.int32, (tm, 1), 0)
    if sample:
        prev = jnp.where(row % nq == 0, shift_ref[...], rolled)
        hl_ref[...] = h
    else:
        @pl.when(i == 0)
        def _():
            carry_ref[...] = jnp.zeros(carry_ref.shape, F32)

        first = jnp.where(i % tiles_per_seq == 0, 0.0, carry_ref[0:1, :])
        prev = jnp.where(row == 0, first, rolled)
        carry_ref[0:1, :] = h[tm - 1:tm, :]
        hl_ref[...] = h[tm - 8:tm, :]
    xx = prev - h
    mu = mu_ref[...]
    mix = lambda n: (h + xx * mu[n:n + 1, :]).astype(BF16)
    xr, xw, xk, xv, xa, xg = (mix(n) for n in range(6))
    r = _dot(xr, wr[...])
    k = _dot(xk, wk[...])
    v = _dot(xv, wv[...])
    wl = w0[...] + _dot(jnp.tanh(_dot(xw, w1[...])).astype(BF16), w2[...])
    lw = -jnp.exp(-_softplus(-wl) - 0.5)
    alr = jax.nn.sigmoid(a0[...] + _dot(_dot(xa, a1[...]).astype(BF16), a2[...]))
    gate_ref[...] = _dot(jax.nn.sigmoid(_dot(xg, g1[...])).astype(BF16), g2[...])
    if has_vres:
        vgate = jax.nn.sigmoid(v0[...] + _dot(_dot(xv, v1[...]).astype(BF16), v2[...]))
    for p in range(HEAD_PAIRS):
        sl = slice(p * PAIR_W, (p + 1) * PAIR_W)
        vp = v[:, sl]
        if has_vres:
            vp = vp + (vf_ref[p] - vp) * vgate[:, sl]
        r_ref[p] = r[:, sl]
        k_ref[p] = k[:, sl]
        v_ref[p] = vp
        lw_ref[p] = lw[:, sl]
        alr_ref[p] = alr[:, sl]


def _rwkv_weights(j, rwkv_mu, rwkv_w_r, rwkv_w_k, rwkv_w_v, rwkv_w0, rwkv_w1, rwkv_w2, rwkv_a0, rwkv_a1,
                  rwkv_a2, rwkv_v0, rwkv_v1, rwkv_v2, rwkv_g1, rwkv_g2):
    row = lambda a: a.reshape(1, D_MODEL)
    w = [rwkv_w_r[j].astype(BF16), rwkv_w_k[j].astype(BF16), rwkv_w_v[j].astype(BF16),
         row(rwkv_w0[j]), rwkv_w1[j].astype(BF16), rwkv_w2[j].astype(BF16),
         row(rwkv_a0[j]), rwkv_a1[j].astype(BF16), rwkv_a2[j].astype(BF16),
         rwkv_g1[j].astype(BF16), rwkv_g2[j].astype(BF16)]
    if j > 0:
        w += [row(rwkv_v0[j - 1]), rwkv_v1[j - 1].astype(BF16), rwkv_v2[j - 1].astype(BF16)]
    return rwkv_mu[j], w


def _rwkv_proj(x, row0, n, g, mu, weights, v_first, shift_exp, seq_len):
    sample = shift_exp is not None
    has_vres = v_first is not None
    tm = _pick(n if sample else seq_len, (256, 128, 64, 32, 16, 8))
    assert row0 % tm == 0 and n % tm == 0
    off = row0 // tm
    if sample:
        assert tm % seq_len == 0
        tiles_per_seq = 1
    else:
        assert seq_len % tm == 0
        tiles_per_seq = seq_len // tm
    full = lambda a: pl.BlockSpec(a.shape, lambda i: (0,) * a.ndim)
    pair_spec = pl.BlockSpec((HEAD_PAIRS, tm, PAIR_W), lambda i: (0, i, 0))
    row_spec = pl.BlockSpec((tm, D_MODEL), lambda i: (i, 0))
    g2d = g.reshape(1, D_MODEL)
    args = [x, g2d, mu]
    specs = [pl.BlockSpec((tm, D_MODEL), lambda i: (i + off, 0)), full(g2d), full(mu)]
    if sample:
        args.append(shift_exp)
        specs.append(row_spec)
    args += weights
    specs += [full(a) for a in weights]
    if has_vres:
        args.append(v_first)
        specs.append(pair_spec)
    pair_shape = jax.ShapeDtypeStruct((HEAD_PAIRS, n, PAIR_W), F32)
    if sample:
        hl_spec, hl_shape = row_spec, jax.ShapeDtypeStruct((n, D_MODEL), F32)
    else:
        hl_spec = pl.BlockSpec((8, D_MODEL), lambda i: (i, 0))
        hl_shape = jax.ShapeDtypeStruct((n // tm * 8, D_MODEL), F32)
    return pl.pallas_call(
        functools.partial(_rwkv_proj_kernel, sample=sample, has_vres=has_vres,
                          tiles_per_seq=tiles_per_seq, nq=seq_len),
        grid=(n // tm,),
        in_specs=specs,
        out_specs=[pair_spec] * 5 + [row_spec, hl_spec],
        out_shape=[pair_shape] * 5 + [jax.ShapeDtypeStruct((n, D_MODEL), F32), hl_shape],
        scratch_shapes=[pltpu.VMEM((8, D_MODEL), F32)],
        compiler_params=_cparams(("arbitrary",)),
        name="rwkv_proj",
    )(*args)


def _wkv_masks(c, seg):
    c2 = 2 * c
    ri = lax.broadcasted_iota(jnp.int32, (c2, c2), 0)
    cj = lax.broadcasted_iota(jnp.int32, (c2, c2), 1)
    t, s = ri % c, cj % c
    same = ((ri // c) == (cj // c)) & ((t // seg) == (s // seg))
    levels = []
    step = 1
    while step < seg:
        levels.append(same & ((t ^ s) < 2 * step) & ((t & step) != 0) & ((s & step) == 0))
        step *= 2
    ti = lax.broadcasted_iota(jnp.int32, (c, c), 0)
    tj = lax.broadcasted_iota(jnp.int32, (c, c), 1)
    same_seq = (ti // seg) == (tj // seg)
    lane = lax.broadcasted_iota(jnp.int32, (1, PAIR_W), 1)
    row = lax.broadcasted_iota(jnp.int32, (c2, 1), 0)
    ei = lax.broadcasted_iota(jnp.int32, (PAIR_W, PAIR_W), 0)
    ej = lax.broadcasted_iota(jnp.int32, (PAIR_W, PAIR_W), 1)
    return dict(
        strict=same & (t > s), incl=same & (t >= s), levels=levels,
        tri=jnp.where(same_seq & (ti >= tj), 1.0, 0.0).astype(BF16),
        seq_ones=jnp.where(same_seq, 1.0, 0.0).astype(BF16),
        m0=lane < HEAD_SIZE, own=(row < c) == (lane < HEAD_SIZE), eye=ei == ej,
    )


def _wkv_block(chunks, prm, msk, h0, seg):
    kkp, kap, rkp, lnw, lnb = prm
    n = len(chunks)
    c = chunks[0][0].shape[0]
    c2 = 2 * c
    nseg = c // seg
    m0 = msk["m0"]
    tri = msk["tri"]
    b16 = lambda a: a.astype(BF16)
    cat0 = lambda a, b: jnp.concatenate([a, b], axis=0)
    cat1 = lambda a, b: jnp.concatenate([a, b], axis=1)

    def stack(a):
        return cat0(jnp.where(m0, a, 0.0), jnp.where(m0, 0.0, a))

    def seq_rows(a, g):
        return a if nseg == 1 else cat0(a[g * seg:(g + 1) * seg], a[c + g * seg:c + (g + 1) * seg])

    cums, tots = [], []
    for (_, _, _, lw, _) in chunks:
        hi = b16(lw)
        r1 = lw - hi.astype(F32)
        mid = b16(r1)
        lo = b16(r1 - mid.astype(F32))
        cum = _dot(tri, hi) + _dot(tri, mid) + _dot(tri, lo)
        cums.append(cum)
        if nseg == 1:
            tots.append(cum[c - 1:c, :])
        else:
            ones = msk["seq_ones"]
            tots.append(_dot(ones, hi) + _dot(ones, mid) + _dot(ones, lo))

    lhs, rhs, at, rt, bh, kh, v16, vs, bonus, gcol = ([] for _ in range(10))
    for (r, k, v, lw, alr), cum, tot in zip(chunks, cums, tots):
        e_in, e_ex = jnp.exp(cum), jnp.exp(cum - lw)
        e_inv, e_end = jnp.exp(-cum), jnp.exp(tot - cum)
        kk = stack(k * kkp)
        kk = kk * (1.0 / jnp.maximum(jnp.sqrt(jnp.sum(kk * kk, axis=-1, keepdims=True)), 1e-12))
        kmod = stack(k * (1.0 + (alr - 1.0) * kap))
        r_s, v_s = stack(r), stack(v)
        bb = kk * stack(alr)
        a_t, r_t = -kk * cat0(e_ex, e_ex), r_s * cat0(e_in, e_in)
        lhs.append(b16(cat0(a_t, r_t)))
        rhs.append(b16(cat0(bb * cat0(e_inv, e_inv), kmod * cat0(e_inv, e_inv))))
        at.append(a_t)
        rt.append(r_t)
        bh.append(bb * cat0(e_end, e_end))
        kh.append(kmod * cat0(e_end, e_end))
        v16.append(b16(v_s))
        vs.append(v_s)
        bonus.append(jnp.sum(r_s * kmod * rkp, axis=-1, keepdims=True) * v_s)
        gcol.append([jnp.sum(jnp.where(msk["eye"], jnp.exp(tot[g * seg:g * seg + 1, :]), 0.0), axis=1, keepdims=True)
                     for g in range(nseg)])

    gram = [_dot_nt(a, b) for a, b in zip(lhs, rhs)]
    ab = [jnp.where(msk["strict"], g[:c2, :c2], 0.0) for g in gram]
    ak = [b16(jnp.where(msk["strict"], g[:c2, c2:], 0.0)) for g in gram]
    rb = [b16(jnp.where(msk["incl"], g[c2:, :c2], 0.0)) for g in gram]
    rk = [b16(jnp.where(msk["incl"], g[c2:, c2:], 0.0)) for g in gram]
    akv = [_dot(a, v) for a, v in zip(ak, v16)]
    rkv = [_dot(a, v) for a, v in zip(rk, v16)]
    khv = [[_dot_tn(b16(seq_rows(a, g)), b16(seq_rows(v, g))) for g in range(nseg)] for a, v in zip(kh, vs)]

    levels = msk["levels"]
    e = [jnp.where(levels[0], a, 0.0) for a in ab]
    for lvl in levels[1:]:
        low = [jnp.where(lvl, a, 0.0) for a in ab]
        x = [lo_ + _dot(b16(e_), b16(lo_)) for e_, lo_ in zip(e, low)]
        e = [e_ + x_ + _dot(b16(x_), b16(e_)) for e_, x_ in zip(e, x)]

    w2 = [cat1(a, b) for a, b in zip(akv, at)]
    u2 = [w + _dot(b16(e_), b16(w)) for w, e_ in zip(w2, e)]
    y2 = [_dot(a, b16(u)) + cat1(b, r_) for a, u, b, r_ in zip(rb, u2, rkv, rt)]
    mj = [[_dot_tn(b16(seq_rows(a, g)), b16(seq_rows(u, g))) for g in range(nseg)] for a, u in zip(bh, u2)]

    y_st, h_out = [], []
    h = h0
    for i in range(n):
        parts = []
        for g in range(nseg):
            h_in = h if nseg == 1 else h0[i][g]
            h16 = b16(h_in)
            y2g = seq_rows(y2[i], g)
            parts.append(y2g[:, :PAIR_W] + _dot(b16(y2g[:, PAIR_W:]), h16))
            h = gcol[i][g] * h_in + _dot(b16(mj[i][g][:, PAIR_W:]), h16) + (mj[i][g][:, :PAIR_W] + khv[i][g])
            h_out.append(h)
        if nseg == 1:
            y_st.append(parts[0])
        else:
            y_st.append(jnp.concatenate([p[:seg] for p in parts] + [p[seg:] for p in parts], axis=0))

    own = msk["own"]
    inv_n = 1.0 / HEAD_SIZE
    zs = []
    for y, bo in zip(y_st, bonus):
        mean = jnp.sum(y, axis=-1, keepdims=True) * inv_n
        d = jnp.where(own, y - mean, 0.0)
        var = jnp.sum(d * d, axis=-1, keepdims=True) * inv_n
        z_st = d * lax.rsqrt(var + LNX_EPS) * lnw + jnp.where(own, lnb, 0.0) + bo
        zs.append(z_st[:c] + z_st[c:])
    return zs, h_out


def _state_in(s_ref, i):
    z = jnp.zeros((HEAD_SIZE, HEAD_SIZE), F32)
    top = jnp.concatenate([s_ref[i, 0], z], axis=1)
    bot = jnp.concatenate([z, s_ref[i, 1]], axis=1)
    return jnp.concatenate([top, bot], axis=0).T


def _state_out(s_ref, i, hbd):
    ht = hbd.T
    s_ref[i, 0] = ht[:HEAD_SIZE, :HEAD_SIZE]
    s_ref[i, 1] = ht[HEAD_SIZE:, HEAD_SIZE:]


def _wkv_kernel(*refs, c, seg, nb, blocks_per_seq, per_seq_state):
    data = refs[:5]
    prm = tuple(ref[0] for ref in refs[5:10])
    msk = _wkv_masks(c, seg)
    chunks = [tuple(ref[0, ci * c:(ci + 1) * c, :] for ref in data) for ci in range(nb)]
    if per_seq_state:
        s0_ref, z_ref, sout_ref = refs[10:]
        nseg = c // seg
        h0 = [[_state_in(s0_ref, ci * nseg + g) for g in range(nseg)] for ci in range(nb)]
        zs, hs = _wkv_block(chunks, prm, msk, h0, seg)
        for si in range(nb * nseg):
            _state_out(sout_ref, si, hs[si])
    else:
        z_ref, sout_ref, h_ref = refs[10:]
        b = pl.program_id(1)

        @pl.when(b % blocks_per_seq == 0)
        def _():
            h_ref[...] = jnp.zeros(h_ref.shape, F32)

        zs, hs = _wkv_block(chunks, prm, msk, h_ref[...], c)
        h_ref[...] = hs[-1]

        @pl.when(b % blocks_per_seq == blocks_per_seq - 1)
        def _():
            _state_out(sout_ref, 0, hs[-1])
    for ci in range(nb):
        z_ref[ci * c:(ci + 1) * c, :] = zs[ci]


def _pair_rows(a):
    return a.reshape(HEAD_PAIRS, 1, PAIR_W)


def _wkv(r, k, v, lw, alr, params, s0, n_seq, seq_len):
    c = WKV_CHUNK
    per_seq_state = s0 is not None
    n_chunks = n_seq * seq_len // c
    if per_seq_state:
        assert c % seq_len == 0 and (n_seq * seq_len) % c == 0
        seg, nb, blocks_per_seq = seq_len, _pick(n_chunks, (8, 4, 2, 1)), 1
        seqs = nb * (c // seg)
        sspec = pl.BlockSpec((seqs, 2, HEAD_SIZE, HEAD_SIZE), lambda p, b: (b, p, 0, 0))
        extra_in, extra_args, scratch = [sspec], [s0], []
    else:
        assert seq_len % c == 0
        seg, nb = c, _pick(seq_len // c, (8, 4, 2, 1))
        blocks_per_seq = seq_len // c // nb
        sspec = pl.BlockSpec((1, 2, HEAD_SIZE, HEAD_SIZE), lambda p, b: (b // blocks_per_seq, p, 0, 0))
        extra_in, extra_args, scratch = [], [], [pltpu.VMEM((PAIR_W, PAIR_W), F32)]
    n_blocks = n_chunks // nb
    dspec = pl.BlockSpec((1, nb * c, PAIR_W), lambda p, b: (p, b, 0))
    pspec = pl.BlockSpec((1, 1, PAIR_W), lambda p, b: (p, 0, 0))
    return pl.pallas_call(
        functools.partial(_wkv_kernel, c=c, seg=seg, nb=nb, blocks_per_seq=blocks_per_seq,
                          per_seq_state=per_seq_state),
        grid=(HEAD_PAIRS, n_blocks),
        in_specs=[dspec] * 5 + [pspec] * 5 + extra_in,
        out_specs=[pl.BlockSpec((nb * c, PAIR_W), lambda p, b: (b, p)), sspec],
        out_shape=[jax.ShapeDtypeStruct((n_seq * seq_len, D_MODEL), F32),
                   jax.ShapeDtypeStruct((n_seq, RWKV_HEADS, HEAD_SIZE, HEAD_SIZE), F32)],
        scratch_shapes=scratch,
        compiler_params=_cparams(("parallel", "arbitrary")),
        name="wkv",
    )(r, k, v, lw, alr, *[_pair_rows(p) for p in params], *extra_args)


def kernel(x_prompt, x_sample, cache_kv_latent, cache_k_rope, state_wkv, state_shift, page_table,
           ffn_norm, ffn_w_gate, ffn_w_up, ffn_w_down, mix_norm,
           mla_w_down, mla_g_q_lat, mla_g_kv_lat, mla_w_uq, mla_w_uk, mla_w_uv, mla_g_qn, mla_g_kn, mla_w_o,
           rwkv_mu, rwkv_w_r, rwkv_w_k, rwkv_w_v, rwkv_w_o, rwkv_w0, rwkv_w1, rwkv_w2,
           rwkv_a0, rwkv_a1, rwkv_a2, rwkv_v0, rwkv_v1, rwkv_v2, rwkv_g1, rwkv_g2,
           rwkv_k_k, rwkv_k_a, rwkv_r_k, rwkv_lnx_w, rwkv_lnx_b):
    nb, l, _ = x_prompt.shape
    ns, nq, _ = x_sample.shape
    n_p, n_s = nb * l, ns * nq
    depth = ffn_norm.shape[0]
    past = page_table.shape[1] * PAGE_SIZE
    x = (x_prompt.reshape(n_p, D_MODEL), x_sample.reshape(n_s, D_MODEL))

    tm = _pick(n_s, (512, 256, 128, 64, 32, 16, 8))
    assert l % tm == 0 and tm % nq == 0
    cos_p, sin_p = _rope_tables(jnp.arange(l))
    cos_s, sin_s = _rope_tables(past + jnp.arange(nq))
    ctab = jnp.concatenate([cos_p, jnp.tile(cos_s, (tm // nq, 1))], axis=0)
    stab = jnp.concatenate([sin_p, jnp.tile(sin_s, (tm // nq, 1))], axis=0)
    n_ptiles, tiles_per_seq = n_p // tm, l // tm
    tab_index = lambda i: jnp.where(i < n_ptiles, i % tiles_per_seq, tiles_per_seq)

    lat_rows, rope_rows, p_wkv, p_shift, s_wkv, s_shift = [], [], [], [], [], []
    vf_p = vf_s = None
    for i in range(depth):
        x = _ffn(x, ffn_norm[i, 0], ffn_w_gate[i, 0].astype(BF16), ffn_w_up[i, 0].astype(BF16),
                 ffn_w_down[i, 0].astype(BF16))
        if i % 2 == 0:
            m = i // 2
            w = _mla_weights(m, mla_w_down, mla_g_q_lat, mla_g_kv_lat, mla_w_uq, mla_w_uk, mla_w_uv,
                             mla_g_qn, mla_g_kn)
            q, c, kr, k, v = _mla_proj(x, mix_norm[i], w, ctab, stab, tm, tab_index)
            o_p = _prompt_attn(q, k, v, nb, l)
            q_s = jnp.transpose(q[:, n_p:, :].reshape(MLA_HEADS, ns, nq, QK_PAD), (1, 0, 2, 3))
            q_s = q_s.reshape(ns, MLA_HEADS * nq, QK_PAD)
            o_s = _sample_attn(q_s, c, kr, n_p, cache_kv_latent, cache_k_rope, m, page_table,
                               jnp.transpose(w["wuk"]), w["wuk"], w["wuv"], w["gkn"])
            proj = (o_p, o_s, mla_w_o[m].astype(BF16), None, None)
            lat_rows.append(c)
            rope_rows.append(kr)
        else:
            j = i // 2
            mu, w = _rwkv_weights(j, rwkv_mu, rwkv_w_r, rwkv_w_k, rwkv_w_v, rwkv_w0, rwkv_w1, rwkv_w2,
                                  rwkv_a0, rwkv_a1, rwkv_a2, rwkv_v0, rwkv_v1, rwkv_v2, rwkv_g1, rwkv_g2)
            params = [rwkv_k_k[j], rwkv_k_a[j], rwkv_r_k[j], rwkv_lnx_w[j], rwkv_lnx_b[j]]
            r, k, v, lw, alr, gate_p, hl = _rwkv_proj(x, 0, n_p, mix_norm[i], mu, w, vf_p, None, l)
            if j == 0:
                vf_p = v
            z_p, st_p = _wkv(r, k, v, lw, alr, params, None, nb, l)
            p_wkv.append(st_p)
            p_shift.append(hl.reshape(nb, -1, 8, D_MODEL)[:, -1, 7])
            shift_exp = jnp.zeros((ns, nq, D_MODEL), F32).at[:, 0, :].set(state_shift[j]).reshape(n_s, D_MODEL)
            r, k, v, lw, alr, gate_s, h_s = _rwkv_proj(x, n_p, n_s, mix_norm[i], mu, w, vf_s, shift_exp, nq)
            if j == 0:
                vf_s = v
            z_s, st_s = _wkv(r, k, v, lw, alr, params, state_wkv[j], ns, nq)
            s_wkv.append(st_s)
            s_shift.append(h_s.reshape(ns, nq, D_MODEL)[:, -1])
            proj = (z_p, z_s, rwkv_w_o[j].astype(BF16), gate_p, gate_s)
        x = _ffn(x, ffn_norm[i, 1], ffn_w_gate[i, 1].astype(BF16), ffn_w_up[i, 1].astype(BF16),
                 ffn_w_down[i, 1].astype(BF16), proj)

    lat = jnp.stack(lat_rows)
    rope = jnp.stack(rope_rows)
    return (x[:n_p].reshape(nb, l, D_MODEL), x[n_p:].reshape(ns, nq, D_MODEL),
            lat[:, :n_p].reshape(-1, nb, l, KV_RANK), rope[:, :n_p].reshape(-1, nb, l, ROPE_DIM),
            jnp.stack(p_wkv), jnp.stack(p_shift),
            lat[:, n_p:].reshape(-1, ns, nq, KV_RANK), rope[:, n_p:].reshape(-1, ns, nq, ROPE_DIM),
            jnp.stack(s_wkv), jnp.stack(s_shift))
```

```python
import functools

import jax
import jax.numpy as jnp
from jax import lax
from jax.experimental import pallas as pl
from jax.experimental.pallas import tpu as pltpu

F32 = jnp.float32
BF16 = jnp.bfloat16

D_MODEL = 1024
D_FF = 2816
RMS_EPS = 1e-6
MLA_HEADS = 8
Q_RANK = 384
KV_RANK = 256
NOPE_DIM = 128
ROPE_DIM = 64
QK_DIM = NOPE_DIM + ROPE_DIM
QK_PAD = 256
V_DIM = 128
ROPE_THETA = 10000.0
ATTN_SCALE = QK_DIM ** -0.5
PAGE_SIZE = 128
HEAD_SIZE = 64
RWKV_HEADS = D_MODEL // HEAD_SIZE
HEAD_PAIRS = RWKV_HEADS // 2
PAIR_W = 2 * HEAD_SIZE
LNX_EPS = 64e-5
WKV_CHUNK = 64

VMEM_LIMIT = 48 * 1024 * 1024


def _cparams(sem):
    return pltpu.CompilerParams(dimension_semantics=sem, vmem_limit_bytes=VMEM_LIMIT)


def _pick(n, cands):
    for c in cands:
        if n % c == 0:
            return c
    raise ValueError(f"no tile in {cands} divides {n}")


def _rms(x, g):
    return x * lax.rsqrt(jnp.mean(x * x, axis=-1, keepdims=True) + RMS_EPS) * g


def _dot(a, b):
    return jnp.dot(a, b, preferred_element_type=F32)


def _dot_nt(a, b):
    return lax.dot_general(a, b, (((1,), (1,)), ((), ())), preferred_element_type=F32)


def _dot_tn(a, b):
    return lax.dot_general(a, b, (((0,), (0,)), ((), ())), preferred_element_type=F32)


FFN_CHUNK = 256


def _swiglu_res(x, g_ref, wg_ref, wu_ref, wd_ref):
    hb = _rms(x, g_ref[...]).astype(BF16)
    acc = x
    for c in range(D_FF // FFN_CHUNK):
        sl = slice(c * FFN_CHUNK, (c + 1) * FFN_CHUNK)
        a = _dot(hb, wg_ref[:, sl])
        u = _dot(hb, wu_ref[:, sl])
        act = (0.5 * a * jax.nn.sigmoid(a) * u).astype(BF16)
        acc = acc + _dot(act, wd_ref[sl, :])
    return acc


def _ffn_kernel(x_ref, g_ref, wg_ref, wu_ref, wd_ref, o_ref):
    o_ref[...] = _swiglu_res(x_ref[...], g_ref, wg_ref, wu_ref, wd_ref)


def _ffn_parts_kernel(*refs, npt, two_x, proj, gated):
    it = iter(refs)
    xp_ref = next(it)
    xs_ref = next(it) if two_x else xp_ref
    ap_ref = as_ref = gp_ref = gs_ref = wp_ref = None
    if proj:
        ap_ref, as_ref = next(it), next(it)
        if gated:
            gp_ref, gs_ref = next(it), next(it)
        wp_ref = next(it)
    g_ref, wg_ref, wu_ref, wd_ref = (next(it) for _ in range(4))
    o_ref = next(it)
    x1_ref = next(it)
    i = pl.program_id(0)

    def stage(x_ref, a_ref, gate_ref):
        x = x_ref[...]
        if proj:
            a = a_ref[...] if gate_ref is None else a_ref[...] * gate_ref[...]
            x = x + _dot(a.astype(BF16), wp_ref[...])
        x1_ref[...] = x

    @pl.when(i < npt)
    def _():
        stage(xp_ref, ap_ref, gp_ref)

    @pl.when(i >= npt)
    def _():
        stage(xs_ref, as_ref, gs_ref)

    o_ref[...] = _swiglu_res(x1_ref[...], g_ref, wg_ref, wu_ref, wd_ref)


def _resident(shape):
    return pl.BlockSpec(shape, lambda i: (0,) * len(shape), pipeline_mode=pl.Buffered(1))


def _ffn(x, g, wg, wu, wd, proj=None):
    two_x = isinstance(x, tuple)
    n = x[0].shape[0] + x[1].shape[0] if two_x else x.shape[0]
    wspecs = [_resident((1, D_MODEL)), _resident(wg.shape), _resident(wu.shape), _resident(wd.shape)]
    wargs = (g.reshape(1, D_MODEL), wg, wu, wd)
    cparams = pltpu.CompilerParams(dimension_semantics=("parallel",), vmem_limit_bytes=56 * 1024 * 1024)
    if not two_x and proj is None:
        tm = _pick(n, (512, 256, 128, 64, 32, 16, 8))
        xspec = pl.BlockSpec((tm, D_MODEL), lambda i: (i, 0))
        return pl.pallas_call(
            _ffn_kernel, grid=(n // tm,), in_specs=[xspec] + wspecs, out_specs=xspec,
            out_shape=jax.ShapeDtypeStruct((n, D_MODEL), F32), compiler_params=cparams, name="ffn",
        )(x, *wargs)

    if two_x:
        n_p, n_s = x[0].shape[0], x[1].shape[0]
    else:
        n_p, n_s = proj[0].shape[0], proj[1].shape[0]
    tm = _pick(n_s, (512, 256, 128, 64, 32, 16, 8))
    assert n == n_p + n_s and n_p % tm == 0
    npt = n_p // tm
    row = lambda width: pl.BlockSpec((tm, width), lambda i: (i, 0))
    prm = lambda width: pl.BlockSpec((tm, width), lambda i: (jnp.minimum(i, npt - 1), 0))
    smp = lambda width: pl.BlockSpec((tm, width), lambda i: (jnp.maximum(i - npt, 0), 0))
    specs, args = ([prm(D_MODEL), smp(D_MODEL)], list(x)) if two_x else ([row(D_MODEL)], [x])
    gated = False
    if proj is not None:
        a_p, a_s, w, gate_p, gate_s = proj
        kin = a_p.shape[1]
        gated = gate_p is not None
        specs += [prm(kin), smp(kin)] + ([prm(kin), smp(kin)] if gated else []) + [_resident(w.shape)]
        args += [a_p, a_s] + ([gate_p, gate_s] if gated else []) + [w]
    return pl.pallas_call(
        functools.partial(_ffn_parts_kernel, npt=npt, two_x=two_x, proj=proj is not None, gated=gated),
        grid=(n // tm,),
        in_specs=specs + wspecs,
        out_specs=row(D_MODEL),
        out_shape=jax.ShapeDtypeStruct((n, D_MODEL), F32),
        scratch_shapes=[pltpu.VMEM((tm, D_MODEL), F32)],
        compiler_params=cparams,
        name="ffn",
    )(*args, *wargs)


def _mla_proj_kernel(x_ref, g_ref, wd_ref, gq_ref, gkv_ref, wqa_ref, wqb_ref, wuk_ref, wuv_ref,
                     gqn_ref, gkn_ref, ct_ref, st_ref,
                     q_ref, c_ref, kr_ref, k_ref, v_ref):
    hb = _rms(x_ref[...], g_ref[...]).astype(BF16)
    lat = _dot(hb, wd_ref[...])
    cq = _rms(lat[:, :Q_RANK], gq_ref[...]).astype(BF16)
    c = _rms(lat[:, Q_RANK:Q_RANK + KV_RANK], gkv_ref[...])
    ct = ct_ref[...]
    st = st_ref[...]
    o = Q_RANK + KV_RANK
    kr = lat[:, o:o + 128] * ct + lat[:, o + 128:o + 256] * st
    c_ref[...] = c
    kr_ref[...] = kr[:, :ROPE_DIM]

    qa = _dot(cq, wqa_ref[...])
    qb = _dot(cq, wqb_ref[...])
    gqn = gqn_ref[...]
    gkn = gkn_ref[...]
    cb = c.astype(BF16)
    kn = _dot(cb, wuk_ref[...])
    vv = _dot(cb, wuv_ref[...])
    ssr = jnp.sum(kr * kr, axis=-1, keepdims=True)
    for h in range(MLA_HEADS):
        nope = qa[:, h * QK_PAD:h * QK_PAD + 128]
        rp = qa[:, h * QK_PAD + 128:(h + 1) * QK_PAD] * ct + qb[:, h * 128:(h + 1) * 128] * st
        ss = jnp.sum(nope * nope, axis=-1, keepdims=True) + jnp.sum(rp * rp, axis=-1, keepdims=True)
        rs = lax.rsqrt(ss * (1.0 / QK_DIM) + RMS_EPS)
        q_ref[h, :, 0:128] = (nope * rs * gqn[:, 0:128]).astype(BF16)
        q_ref[h, :, 128:256] = (rp * rs * gqn[:, 128:256]).astype(BF16)
        knh = kn[:, h * 128:(h + 1) * 128]
        rk = lax.rsqrt((jnp.sum(knh * knh, axis=-1, keepdims=True) + ssr) * (1.0 / QK_DIM) + RMS_EPS)
        k_ref[h, :, 0:128] = (knh * rk * gkn[:, 0:128]).astype(BF16)
        k_ref[h, :, 128:256] = (kr * rk * gkn[:, 128:256]).astype(BF16)
        v_ref[h] = vv[:, h * 128:(h + 1) * 128].astype(BF16)


def _rot_cols(w):
    half = ROPE_DIM // 2
    return jnp.concatenate([-w[..., half:], w[..., :half]], axis=-1)


def _pad_lanes(w, n):
    return jnp.pad(w, [(0, 0)] * (w.ndim - 1) + [(0, n - w.shape[-1])])


def _mla_weights(m, mla_w_down, mla_g_q_lat, mla_g_kv_lat, mla_w_uq, mla_w_uk, mla_w_uv, mla_g_qn, mla_g_kn):
    wd = mla_w_down[m]
    o = Q_RANK + KV_RANK
    wkr = wd[:, o:]
    wd_ext = jnp.concatenate([wd[:, :o], _pad_lanes(wkr, 128), _pad_lanes(_rot_cols(wkr), 128)], axis=1)
    wq = mla_w_uq[m].reshape(Q_RANK, MLA_HEADS, QK_DIM)
    wqa = _pad_lanes(wq, QK_PAD).reshape(Q_RANK, MLA_HEADS * QK_PAD)
    wqb = _pad_lanes(_rot_cols(wq[..., NOPE_DIM:]), 128).reshape(Q_RANK, MLA_HEADS * 128)
    return dict(
        wd=wd_ext.astype(BF16), gq=mla_g_q_lat[m].reshape(1, Q_RANK), gkv=mla_g_kv_lat[m].reshape(1, KV_RANK),
        wqa=wqa.astype(BF16), wqb=wqb.astype(BF16),
        wuk=mla_w_uk[m].reshape(KV_RANK, MLA_HEADS * NOPE_DIM).astype(BF16),
        wuv=mla_w_uv[m].reshape(KV_RANK, MLA_HEADS * V_DIM).astype(BF16),
        gqn=_pad_lanes(mla_g_qn[m] * ATTN_SCALE, QK_PAD).reshape(1, QK_PAD),
        gkn=_pad_lanes(mla_g_kn[m], QK_PAD).reshape(1, QK_PAD),
    )


def _rope_tables(pos):
    inv = ROPE_THETA ** (-jnp.arange(0, ROPE_DIM, 2, dtype=F32) / ROPE_DIM)
    ang = pos.astype(F32)[:, None] * inv[None, :]
    cos, sin = jnp.cos(ang), jnp.sin(ang)
    return (_pad_lanes(jnp.concatenate([cos, cos], axis=-1), 128),
            _pad_lanes(jnp.concatenate([sin, sin], axis=-1), 128))


def _mla_proj(x, g, w, ctab, stab, tm, tab_index):
    n = x.shape[0]
    full = lambda shape: pl.BlockSpec(shape, lambda i: (0,) * len(shape))
    tspec = pl.BlockSpec((tm, 128), lambda i: (tab_index(i), 0))
    return pl.pallas_call(
        _mla_proj_kernel,
        grid=(n // tm,),
        in_specs=[
            pl.BlockSpec((tm, D_MODEL), lambda i: (i, 0)),
            full((1, D_MODEL)), full(w["wd"].shape), full((1, Q_RANK)), full((1, KV_RANK)),
            full(w["wqa"].shape), full(w["wqb"].shape), full(w["wuk"].shape), full(w["wuv"].shape),
            full((1, QK_PAD)), full((1, QK_PAD)), tspec, tspec,
        ],
        out_specs=[
            pl.BlockSpec((MLA_HEADS, tm, QK_PAD), lambda i: (0, i, 0)),
            pl.BlockSpec((tm, KV_RANK), lambda i: (i, 0)),
            pl.BlockSpec((tm, ROPE_DIM), lambda i: (i, 0)),
            pl.BlockSpec((MLA_HEADS, tm, QK_PAD), lambda i: (0, i, 0)),
            pl.BlockSpec((MLA_HEADS, tm, V_DIM), lambda i: (0, i, 0)),
        ],
        out_shape=[
            jax.ShapeDtypeStruct((MLA_HEADS, n, QK_PAD), BF16),
            jax.ShapeDtypeStruct((n, KV_RANK), F32),
            jax.ShapeDtypeStruct((n, ROPE_DIM), F32),
            jax.ShapeDtypeStruct((MLA_HEADS, n, QK_PAD), BF16),
            jax.ShapeDtypeStruct((MLA_HEADS, n, V_DIM), BF16),
        ],
        compiler_params=_cparams(("parallel",)),
        name="mla_proj",
    )(x, g.reshape(1, D_MODEL), w["wd"], w["gq"], w["gkv"], w["wqa"], w["wqb"], w["wuk"], w["wuv"],
      w["gqn"], w["gkn"], ctab, stab)


def _prompt_attn_kernel(q_ref, k_ref, v_ref, o_ref, *, tq):
    l = q_ref.shape[1]
    nq = l // tq
    row = lax.broadcasted_iota(jnp.int32, (tq, tq), 0)
    col = lax.broadcasted_iota(jnp.int32, (tq, tq), 1)
    diag_mask = row >= col
    for qi in range(nq):
        q = q_ref[0, qi * tq:(qi + 1) * tq, :]
        m = l_sum = acc = None
        for ki in range(qi + 1):
            k = k_ref[0, ki * tq:(ki + 1) * tq, :]
            v = v_ref[0, ki * tq:(ki + 1) * tq, :]
            s = _dot_nt(q, k)
            if ki == qi:
                s = jnp.where(diag_mask, s, -jnp.inf)
            m_blk = jnp.max(s, axis=-1, keepdims=True)
            if ki == 0:
                m = m_blk
                p = jnp.exp(s - m)
                l_sum = jnp.sum(p, axis=-1, keepdims=True)
                acc = _dot(p.astype(BF16), v)
            else:
                m_new = jnp.maximum(m, m_blk)
                alpha = jnp.exp(m - m_new)
                p = jnp.exp(s - m_new)
                l_sum = alpha * l_sum + jnp.sum(p, axis=-1, keepdims=True)
                acc = alpha * acc + _dot(p.astype(BF16), v)
                m = m_new
        o_ref[qi * tq:(qi + 1) * tq, :] = (acc / l_sum).astype(o_ref.dtype)


def _prompt_attn(q, k, v, nb, l):
    tq = _pick(l, (512, 256, 128))
    return pl.pallas_call(
        functools.partial(_prompt_attn_kernel, tq=tq),
        grid=(nb, MLA_HEADS),
        in_specs=[
            pl.BlockSpec((1, l, QK_PAD), lambda b, h: (h, b, 0)),
            pl.BlockSpec((1, l, QK_PAD), lambda b, h: (h, b, 0)),
            pl.BlockSpec((1, l, V_DIM), lambda b, h: (h, b, 0)),
        ],
        out_specs=pl.BlockSpec((l, V_DIM), lambda b, h: (b, h)),
        out_shape=jax.ShapeDtypeStruct((nb * l, MLA_HEADS * V_DIM), BF16),
        compiler_params=_cparams(("parallel", "parallel")),
        name="prompt_attn",
    )(q, k, v)


def _sample_attn_kernel(pt_ref, q_ref, cn_ref, krn_ref, wukt_ref, wuk_ref, wuv_ref, gkn_ref, cc_hbm, ckr_hbm,
                        o_ref, wq_ref, cb_ref, cbuf, kbuf, sem, *, m, n_pages, tc):
    b = pl.program_id(0)
    nseq = pl.num_programs(0)
    slot = b % 2
    nq = q_ref.shape[1] // MLA_HEADS
    rows = q_ref.shape[1]
    n_nope = MLA_HEADS * NOPE_DIM

    def page_copies(page, slot_, j):
        return (pltpu.make_async_copy(cc_hbm.at[m, page], cbuf.at[slot_, j], sem.at[slot_]),
                pltpu.make_async_copy(ckr_hbm.at[m, page], kbuf.at[slot_, j], sem.at[slot_]))

    def start_pages(seq, slot_):
        for j in range(n_pages):
            for cp in page_copies(pt_ref[seq * n_pages + j], slot_, j):
                cp.start()

    def wait_pages(slot_):
        for j in range(n_pages):
            for cp in page_copies(0, slot_, j):
                cp.wait()

    @pl.when(b == 0)
    def _():
        start_pages(0, 0)
        wq_ref[0:n_nope, :] = wukt_ref[...]

    wait_pages(slot)
    start_pages(jnp.where(b + 1 == nseq, 0, b + 1), 1 - slot)

    gkn = gkn_ref[...]
    qts, qrs = [], []
    for h in range(MLA_HEADS):
        qh = q_ref[0, h * nq:(h + 1) * nq, :].astype(F32)
        qg = (qh[:, :NOPE_DIM] * gkn[:, :NOPE_DIM]).astype(BF16)
        qts.append(_dot_nt(qg, wuk_ref[:, h * NOPE_DIM:(h + 1) * NOPE_DIM]))
        qrs.append(qh[:, NOPE_DIM:QK_DIM] * gkn[:, NOPE_DIM:QK_DIM])
    wq_ref[n_nope:n_nope + rows, :] = jnp.concatenate(qts, axis=0).astype(BF16)
    qr = jnp.concatenate(qrs, axis=0).astype(BF16)

    def scores(cbc, ssr, s_rope):
        t = cbc.shape[0]
        big = _dot_nt(wq_ref[...], cbc)
        rs = []
        for h in range(MLA_HEADS):
            kh = big[h * NOPE_DIM:(h + 1) * NOPE_DIM]
            ssq = jnp.sum(kh * kh, axis=0, keepdims=True)
            rs_h = lax.rsqrt((ssq + ssr) * (1.0 / QK_DIM) + RMS_EPS)
            rs.append(jnp.broadcast_to(rs_h, (nq, t)))
        return (big[n_nope:] + s_rope) * jnp.concatenate(rs, axis=0)

    pages_per_chunk = tc // PAGE_SIZE
    for j in range(n_pages):
        cb_ref[j * PAGE_SIZE:(j + 1) * PAGE_SIZE, :] = cbuf[slot, j].astype(BF16)
    s_list, c_list = [], []
    for ci in range(n_pages // pages_per_chunk):
        cbc = cb_ref[ci * tc:(ci + 1) * tc, :]
        krt = jnp.concatenate([kbuf[slot, ci * pages_per_chunk + jj] for jj in range(pages_per_chunk)], axis=1)
        ssr = jnp.sum(krt * krt, axis=0, keepdims=True)
        s_list.append(scores(cbc, ssr, _dot(qr, krt.astype(BF16))))
        c_list.append(cbc)

    pad = PAGE_SIZE - nq
    cn = jnp.concatenate([cn_ref[...], jnp.zeros((pad, KV_RANK), F32)], axis=0).astype(BF16)
    krn = jnp.concatenate([krn_ref[...], jnp.zeros((pad, ROPE_DIM), F32)], axis=0)
    ssr = _dot_nt(jnp.ones((8, ROPE_DIM), BF16), (krn * krn).astype(BF16))[0:1]
    s = scores(cn, ssr, _dot_nt(qr, krn.astype(BF16)))
    qpos = lax.broadcasted_iota(jnp.int32, (rows, PAGE_SIZE), 0) % nq
    tok = lax.broadcasted_iota(jnp.int32, (rows, PAGE_SIZE), 1)
    s_list.append(jnp.where(tok <= qpos, s, -jnp.inf))
    c_list.append(cn)

    mx = jnp.max(s_list[0], axis=-1, keepdims=True)
    for s in s_list[1:]:
        mx = jnp.maximum(mx, jnp.max(s, axis=-1, keepdims=True))
    ps = [jnp.exp(s - mx) for s in s_list]
    den = jnp.sum(ps[0], axis=-1, keepdims=True)
    for p in ps[1:]:
        den = den + jnp.sum(p, axis=-1, keepdims=True)
    acc = _dot(ps[0].astype(BF16), c_list[0])
    for p, cbc in zip(ps[1:], c_list[1:]):
        acc = acc + _dot(p.astype(BF16), cbc)
    o_lat = (acc / den).astype(BF16)
    for h in range(MLA_HEADS):
        o_ref[:, h * V_DIM:(h + 1) * V_DIM] = _dot(o_lat[h * nq:(h + 1) * nq], wuv_ref[:, h * V_DIM:(h + 1) * V_DIM])

    @pl.when(b == nseq - 1)
    def _():
        wait_pages(1 - slot)


def _sample_attn(q_s, c_all, kr_all, row0, cache_c, cache_kr, m, page_table, wukt, wuk, wuv, gkn):
    ns, n_pages = page_table.shape
    rows = q_s.shape[1]
    nq = rows // MLA_HEADS
    assert row0 % nq == 0 and nq % 8 == 0
    tc = 2 * PAGE_SIZE
    assert n_pages % 2 == 0 and n_pages <= 64
    blk0 = row0 // nq
    full = lambda shape: pl.BlockSpec(shape, lambda b, pt: (0,) * len(shape))

    cache_kr_t = jnp.swapaxes(cache_kr, 2, 3)
    grid_spec = pltpu.PrefetchScalarGridSpec(
        num_scalar_prefetch=1,
        grid=(ns,),
        in_specs=[
            pl.BlockSpec((1, rows, QK_PAD), lambda b, pt: (b, 0, 0)),
            pl.BlockSpec((nq, KV_RANK), lambda b, pt: (blk0 + b, 0)),
            pl.BlockSpec((nq, ROPE_DIM), lambda b, pt: (blk0 + b, 0)),
            full(wukt.shape), full(wuk.shape), full(wuv.shape), full((1, QK_PAD)),
            pl.BlockSpec(memory_space=pl.ANY), pl.BlockSpec(memory_space=pl.ANY),
        ],
        out_specs=pl.BlockSpec((nq, MLA_HEADS * V_DIM), lambda b, pt: (b, 0)),
        scratch_shapes=[
            pltpu.VMEM((MLA_HEADS * NOPE_DIM + rows, KV_RANK), BF16),
            pltpu.VMEM((n_pages * PAGE_SIZE, KV_RANK), BF16),
            pltpu.VMEM((2, n_pages, PAGE_SIZE, KV_RANK), F32),
            pltpu.VMEM((2, n_pages, ROPE_DIM, PAGE_SIZE), F32),
            pltpu.SemaphoreType.DMA((2,)),
        ],
    )
    return pl.pallas_call(
        functools.partial(_sample_attn_kernel, m=m, n_pages=n_pages, tc=tc),
        grid_spec=grid_spec,
        out_shape=jax.ShapeDtypeStruct((ns * nq, MLA_HEADS * V_DIM), F32),
        compiler_params=_cparams(("arbitrary",)),
        name="sample_attn",
    )(page_table.reshape(-1), q_s, c_all, kr_all, wukt, wuk, wuv, gkn, cache_c, cache_kr_t)


def _softplus(y):
    return jnp.maximum(y, 0.0) + jnp.log(1.0 + jnp.exp(-jnp.abs(y)))


def _rwkv_proj_kernel(*refs, sample, has_vres, tiles_per_seq, nq):
    it = iter(refs)
    x_ref, g_ref, mu_ref = next(it), next(it), next(it)
    shift_ref = next(it) if sample else None
    wr, wk, wv, w0, w1, w2, a0, a1, a2, g1, g2 = (next(it) for _ in range(11))
    if has_vres:
        v0, v1, v2, vf_ref = (next(it) for _ in range(4))
    r_ref, k_ref, v_ref, lw_ref, alr_ref, gate_ref, hl_ref = (next(it) for _ in range(7))
    carry_ref = next(it)
    tm = x_ref.shape[0]
    i = pl.program_id(0)

    h = _rms(x_ref[...], g_ref[...])
    rolled = pltpu.roll(h, 1, axis=0)
    row = lax.broadcasted_iota(jnp.int32, (tm, 1), 0)
    if sample:
        prev = jnp.where(row % nq == 0, shift_ref[...], rolled)
        hl_ref[...] = h
    else:
        @pl.when(i == 0)
        def _():
            carry_ref[...] = jnp.zeros(carry_ref.shape, F32)

        first = jnp.where(i % tiles_per_seq == 0, 0.0, carry_ref[0:1, :])
        prev = jnp.where(row == 0, first, rolled)
        carry_ref[0:1, :] = h[tm - 1:tm, :]
        hl_ref[...] = h[tm - 8:tm, :]
    xx = prev - h
    mu = mu_ref[...]
    mix = lambda n: (h + xx * mu[n:n + 1, :]).astype(BF16)
    xr, xw, xk, xv, xa, xg = (mix(n) for n in range(6))
    r = _dot(xr, wr[...])
    k = _dot(xk, wk[...])
    v = _dot(xv, wv[...])
    wl = w0[...] + _dot(jnp.tanh(_dot(xw, w1[...])).astype(BF16), w2[...])
    lw = -jnp.exp(-_softplus(-wl) - 0.5)
    alr = jax.nn.sigmoid(a0[...] + _dot(_dot(xa, a1[...]).astype(BF16), a2[...]))
    gate_ref[...] = _dot(jax.nn.sigmoid(_dot(xg, g1[...])).astype(BF16), g2[...])
    if has_vres:
        vgate = jax.nn.sigmoid(v0[...] + _dot(_dot(xv, v1[...]).astype(BF16), v2[...]))
    for p in range(HEAD_PAIRS):
        sl = slice(p * PAIR_W, (p + 1) * PAIR_W)
        vp = v[:, sl]
        if has_vres:
            vp = vp + (vf_ref[p] - vp) * vgate[:, sl]
        r_ref[p] = r[:, sl]
        k_ref[p] = k[:, sl]
        v_ref[p] = vp
        lw_ref[p] = lw[:, sl]
        alr_ref[p] = alr[:, sl]


def _rwkv_weights(j, rwkv_mu, rwkv_w_r, rwkv_w_k, rwkv_w_v, rwkv_w0, rwkv_w1, rwkv_w2, rwkv_a0, rwkv_a1,
                  rwkv_a2, rwkv_v0, rwkv_v1, rwkv_v2, rwkv_g1, rwkv_g2):
    row = lambda a: a.reshape(1, D_MODEL)
    w = [rwkv_w_r[j].astype(BF16), rwkv_w_k[j].astype(BF16), rwkv_w_v[j].astype(BF16),
         row(rwkv_w0[j]), rwkv_w1[j].astype(BF16), rwkv_w2[j].astype(BF16),
         row(rwkv_a0[j]), rwkv_a1[j].astype(BF16), rwkv_a2[j].astype(BF16),
         rwkv_g1[j].astype(BF16), rwkv_g2[j].astype(BF16)]
    if j > 0:
        w += [row(rwkv_v0[j - 1]), rwkv_v1[j - 1].astype(BF16), rwkv_v2[j - 1].astype(BF16)]
    return rwkv_mu[j], w


def _rwkv_proj(x, row0, n, g, mu, weights, v_first, shift_exp, seq_len):
    sample = shift_exp is not None
    has_vres = v_first is not None
    tm = _pick(n if sample else seq_len, (256, 128, 64, 32, 16, 8))
    assert row0 % tm == 0 and n % tm == 0
    off = row0 // tm
    if sample:
        assert tm % seq_len == 0
        tiles_per_seq = 1
    else:
        assert seq_len % tm == 0
        tiles_per_seq = seq_len // tm
    full = lambda a: pl.BlockSpec(a.shape, lambda i: (0,) * a.ndim)
    pair_spec = pl.BlockSpec((HEAD_PAIRS, tm, PAIR_W), lambda i: (0, i, 0))
    row_spec = pl.BlockSpec((tm, D_MODEL), lambda i: (i, 0))
    g2d = g.reshape(1, D_MODEL)
    args = [x, g2d, mu]
    specs = [pl.BlockSpec((tm, D_MODEL), lambda i: (i + off, 0)), full(g2d), full(mu)]
    if sample:
        args.append(shift_exp)
        specs.append(row_spec)
    args += weights
    specs += [full(a) for a in weights]
    if has_vres:
        args.append(v_first)
        specs.append(pair_spec)
    pair_shape = jax.ShapeDtypeStruct((HEAD_PAIRS, n, PAIR_W), F32)
    if sample:
        hl_spec, hl_shape = row_spec, jax.ShapeDtypeStruct((n, D_MODEL), F32)
    else:
        hl_spec = pl.BlockSpec((8, D_MODEL), lambda i: (i, 0))
        hl_shape = jax.ShapeDtypeStruct((n // tm * 8, D_MODEL), F32)
    return pl.pallas_call(
        functools.partial(_rwkv_proj_kernel, sample=sample, has_vres=has_vres,
                          tiles_per_seq=tiles_per_seq, nq=seq_len),
        grid=(n // tm,),
        in_specs=specs,
        out_specs=[pair_spec] * 5 + [row_spec, hl_spec],
        out_shape=[pair_shape] * 5 + [jax.ShapeDtypeStruct((n, D_MODEL), F32), hl_shape],
        scratch_shapes=[pltpu.VMEM((8, D_MODEL), F32)],
        compiler_params=_cparams(("arbitrary",)),
        name="rwkv_proj",
    )(*args)


def _wkv_masks(c, seg):
    c2 = 2 * c
    ri = lax.broadcasted_iota(jnp.int32, (c2, c2), 0)
    cj = lax.broadcasted_iota(jnp.int32, (c2, c2), 1)
    t, s = ri % c, cj % c
    same = ((ri // c) == (cj // c)) & ((t // seg) == (s // seg))
    levels = []
    step = 1
    while step < seg:
        levels.append(same & ((t ^ s) < 2 * step) & ((t & step) != 0) & ((s & step) == 0))
        step *= 2
    ti = lax.broadcasted_iota(jnp.int32, (c, c), 0)
    tj = lax.broadcasted_iota(jnp.int32, (c, c), 1)
    same_seq = (ti // seg) == (tj // seg)
    lane = lax.broadcasted_iota(jnp.int32, (1, PAIR_W), 1)
    row = lax.broadcasted_iota(jnp.int32, (c2, 1), 0)
    ei = lax.broadcasted_iota(jnp.int32, (PAIR_W, PAIR_W), 0)
    ej = lax.broadcasted_iota(jnp.int32, (PAIR_W, PAIR_W), 1)
    return dict(
        strict=same & (t > s), incl=same & (t >= s), levels=levels,
        tri=jnp.where(same_seq & (ti >= tj), 1.0, 0.0).astype(BF16),
        seq_ones=jnp.where(same_seq, 1.0, 0.0).astype(BF16),
        m0=lane < HEAD_SIZE, own=(row < c) == (lane < HEAD_SIZE), eye=ei == ej,
    )


def _wkv_block(chunks, prm, msk, h0, seg):
    kkp, kap, rkp, lnw, lnb = prm
    n = len(chunks)
    c = chunks[0][0].shape[0]
    c2 = 2 * c
    nseg = c // seg
    m0 = msk["m0"]
    tri = msk["tri"]
    b16 = lambda a: a.astype(BF16)
    cat0 = lambda a, b: jnp.concatenate([a, b], axis=0)
    cat1 = lambda a, b: jnp.concatenate([a, b], axis=1)

    def stack(a):
        return cat0(jnp.where(m0, a, 0.0), jnp.where(m0, 0.0, a))

    def seq_rows(a, g):
        return a if nseg == 1 else cat0(a[g * seg:(g + 1) * seg], a[c + g * seg:c + (g + 1) * seg])

    cums, tots = [], []
    for (_, _, _, lw, _) in chunks:
        hi = b16(lw)
        r1 = lw - hi.astype(F32)
        mid = b16(r1)
        lo = b16(r1 - mid.astype(F32))
        cum = _dot(tri, hi) + _dot(tri, mid) + _dot(tri, lo)
        cums.append(cum)
        if nseg == 1:
            tots.append(cum[c - 1:c, :])
        else:
            ones = msk["seq_ones"]
            tots.append(_dot(ones, hi) + _dot(ones, mid) + _dot(ones, lo))

    lhs, rhs, at, rt, bh, kh, v16, vs, bonus, gcol = ([] for _ in range(10))
    for (r, k, v, lw, alr), cum, tot in zip(chunks, cums, tots):
        e_in, e_ex = jnp.exp(cum), jnp.exp(cum - lw)
        e_inv, e_end = jnp.exp(-cum), jnp.exp(tot - cum)
        kk = stack(k * kkp)
        kk = kk * (1.0 / jnp.maximum(jnp.sqrt(jnp.sum(kk * kk, axis=-1, keepdims=True)), 1e-12))
        kmod = stack(k * (1.0 + (alr - 1.0) * kap))
        r_s, v_s = stack(r), stack(v)
        bb = kk * stack(alr)
        a_t, r_t = -kk * cat0(e_ex, e_ex), r_s * cat0(e_in, e_in)
        lhs.append(b16(cat0(a_t, r_t)))
        rhs.append(b16(cat0(bb * cat0(e_inv, e_inv), kmod * cat0(e_inv, e_inv))))
        at.append(a_t)
        rt.append(r_t)
        bh.append(bb * cat0(e_end, e_end))
        kh.append(kmod * cat0(e_end, e_end))
        v16.append(b16(v_s))
        vs.append(v_s)
        bonus.append(jnp.sum(r_s * kmod * rkp, axis=-1, keepdims=True) * v_s)
        gcol.append([jnp.sum(jnp.where(msk["eye"], jnp.exp(tot[g * seg:g * seg + 1, :]), 0.0), axis=1, keepdims=True)
                     for g in range(nseg)])

    gram = [_dot_nt(a, b) for a, b in zip(lhs, rhs)]
    ab = [jnp.where(msk["strict"], g[:c2, :c2], 0.0) for g in gram]
    ak = [b16(jnp.where(msk["strict"], g[:c2, c2:], 0.0)) for g in gram]
    rb = [b16(jnp.where(msk["incl"], g[c2:, :c2], 0.0)) for g in gram]
    rk = [b16(jnp.where(msk["incl"], g[c2:, c2:], 0.0)) for g in gram]
    akv = [_dot(a, v) for a, v in zip(ak, v16)]
    rkv = [_dot(a, v) for a, v in zip(rk, v16)]
    khv = [[_dot_tn(b16(seq_rows(a, g)), b16(seq_rows(v, g))) for g in range(nseg)] for a, v in zip(kh, vs)]

    levels = msk["levels"]
    e = [jnp.where(levels[0], a, 0.0) for a in ab]
    for lvl in levels[1:]:
        low = [jnp.where(lvl, a, 0.0) for a in ab]
        x = [lo_ + _dot(b16(e_), b16(lo_)) for e_, lo_ in zip(e, low)]
        e = [e_ + x_ + _dot(b16(x_), b16(e_)) for e_, x_ in zip(e, x)]

    w2 = [cat1(a, b) for a, b in zip(akv, at)]
    u2 = [w + _dot(b16(e_), b16(w)) for w, e_ in zip(w2, e)]
    y2 = [_dot(a, b16(u)) + cat1(b, r_) for a, u, b, r_ in zip(rb, u2, rkv, rt)]
    mj = [[_dot_tn(b16(seq_rows(a, g)), b16(seq_rows(u, g))) for g in range(nseg)] for a, u in zip(bh, u2)]

    y_st, h_out = [], []
    h = h0
    for i in range(n):
        parts = []
        for g in range(nseg):
            h_in = h if nseg == 1 else h0[i][g]
            h16 = b16(h_in)
            y2g = seq_rows(y2[i], g)
            parts.append(y2g[:, :PAIR_W] + _dot(b16(y2g[:, PAIR_W:]), h16))
            h = gcol[i][g] * h_in + _dot(b16(mj[i][g][:, PAIR_W:]), h16) + (mj[i][g][:, :PAIR_W] + khv[i][g])
            h_out.append(h)
        if nseg == 1:
            y_st.append(parts[0])
        else:
            y_st.append(jnp.concatenate([p[:seg] for p in parts] + [p[seg:] for p in parts], axis=0))

    own = msk["own"]
    inv_n = 1.0 / HEAD_SIZE
    zs = []
    for y, bo in zip(y_st, bonus):
        mean = jnp.sum(y, axis=-1, keepdims=True) * inv_n
        d = jnp.where(own, y - mean, 0.0)
        var = jnp.sum(d * d, axis=-1, keepdims=True) * inv_n
        z_st = d * lax.rsqrt(var + LNX_EPS) * lnw + jnp.where(own, lnb, 0.0) + bo
        zs.append(z_st[:c] + z_st[c:])
    return zs, h_out


def _state_in(s_ref, i):
    z = jnp.zeros((HEAD_SIZE, HEAD_SIZE), F32)
    top = jnp.concatenate([s_ref[i, 0], z], axis=1)
    bot = jnp.concatenate([z, s_ref[i, 1]], axis=1)
    return jnp.concatenate([top, bot], axis=0).T


def _state_out(s_ref, i, hbd):
    ht = hbd.T
    s_ref[i, 0] = ht[:HEAD_SIZE, :HEAD_SIZE]
    s_ref[i, 1] = ht[HEAD_SIZE:, HEAD_SIZE:]


def _wkv_kernel(*refs, c, seg, nb, blocks_per_seq, per_seq_state):
    data = refs[:5]
    prm = tuple(ref[0] for ref in refs[5:10])
    msk = _wkv_masks(c, seg)
    chunks = [tuple(ref[0, ci * c:(ci + 1) * c, :] for ref in data) for ci in range(nb)]
    if per_seq_state:
        s0_ref, z_ref, sout_ref = refs[10:]
        nseg = c // seg
        h0 = [[_state_in(s0_ref, ci * nseg + g) for g in range(nseg)] for ci in range(nb)]
        zs, hs = _wkv_block(chunks, prm, msk, h0, seg)
        for si in range(nb * nseg):
            _state_out(sout_ref, si, hs[si])
    else:
        z_ref, sout_ref, h_ref = refs[10:]
        b = pl.program_id(1)

        @pl.when(b % blocks_per_seq == 0)
        def _():
            h_ref[...] = jnp.zeros(h_ref.shape, F32)

        zs, hs = _wkv_block(chunks, prm, msk, h_ref[...], c)
        h_ref[...] = hs[-1]

        @pl.when(b % blocks_per_seq == blocks_per_seq - 1)
        def _():
            _state_out(sout_ref, 0, hs[-1])
    for ci in range(nb):
        z_ref[ci * c:(ci + 1) * c, :] = zs[ci]


def _pair_rows(a):
    return a.reshape(HEAD_PAIRS, 1, PAIR_W)


def _wkv(r, k, v, lw, alr, params, s0, n_seq, seq_len):
    c = WKV_CHUNK
    per_seq_state = s0 is not None
    n_chunks = n_seq * seq_len // c
    if per_seq_state:
        assert c % seq_len == 0 and (n_seq * seq_len) % c == 0
        seg, nb, blocks_per_seq = seq_len, _pick(n_chunks, (8, 4, 2, 1)), 1
        seqs = nb * (c // seg)
        sspec = pl.BlockSpec((seqs, 2, HEAD_SIZE, HEAD_SIZE), lambda p, b: (b, p, 0, 0))
        extra_in, extra_args, scratch = [sspec], [s0], []
    else:
        assert seq_len % c == 0
        seg, nb = c, _pick(seq_len // c, (8, 4, 2, 1))
        blocks_per_seq = seq_len // c // nb
        sspec = pl.BlockSpec((1, 2, HEAD_SIZE, HEAD_SIZE), lambda p, b: (b // blocks_per_seq, p, 0, 0))
        extra_in, extra_args, scratch = [], [], [pltpu.VMEM((PAIR_W, PAIR_W), F32)]
    n_blocks = n_chunks // nb
    dspec = pl.BlockSpec((1, nb * c, PAIR_W), lambda p, b: (p, b, 0))
    pspec = pl.BlockSpec((1, 1, PAIR_W), lambda p, b: (p, 0, 0))
    return pl.pallas_call(
        functools.partial(_wkv_kernel, c=c, seg=seg, nb=nb, blocks_per_seq=blocks_per_seq,
                          per_seq_state=per_seq_state),
        grid=(HEAD_PAIRS, n_blocks),
        in_specs=[dspec] * 5 + [pspec] * 5 + extra_in,
        out_specs=[pl.BlockSpec((nb * c, PAIR_W), lambda p, b: (b, p)), sspec],
        out_shape=[jax.ShapeDtypeStruct((n_seq * seq_len, D_MODEL), F32),
                   jax.ShapeDtypeStruct((n_seq, RWKV_HEADS, HEAD_SIZE, HEAD_SIZE), F32)],
        scratch_shapes=scratch,
        compiler_params=_cparams(("parallel", "arbitrary")),
        name="wkv",
    )(r, k, v, lw, alr, *[_pair_rows(p) for p in params], *extra_args)


def kernel(x_prompt, x_sample, cache_kv_latent, cache_k_rope, state_wkv, state_shift, page_table,
           ffn_norm, ffn_w_gate, ffn_w_up, ffn_w_down, mix_norm,
           mla_w_down, mla_g_q_lat, mla_g_kv_lat, mla_w_uq, mla_w_uk, mla_w_uv, mla_g_qn, mla_g_kn, mla_w_o,
           rwkv_mu, rwkv_w_r, rwkv_w_k, rwkv_w_v, rwkv_w_o, rwkv_w0, rwkv_w1, rwkv_w2,
           rwkv_a0, rwkv_a1, rwkv_a2, rwkv_v0, rwkv_v1, rwkv_v2, rwkv_g1, rwkv_g2,
           rwkv_k_k, rwkv_k_a, rwkv_r_k, rwkv_lnx_w, rwkv_lnx_b):
    nb, l, _ = x_prompt.shape
    ns, nq, _ = x_sample.shape
    n_p, n_s = nb * l, ns * nq
    depth = ffn_norm.shape[0]
    past = page_table.shape[1] * PAGE_SIZE
    x = (x_prompt.reshape(n_p, D_MODEL), x_sample.reshape(n_s, D_MODEL))

    tm = _pick(n_s, (512, 256, 128, 64, 32, 16, 8))
    assert l % tm == 0 and tm % nq == 0
    cos_p, sin_p = _rope_tables(jnp.arange(l))
    cos_s, sin_s = _rope_tables(past + jnp.arange(nq))
    ctab = jnp.concatenate([cos_p, jnp.tile(cos_s, (tm // nq, 1))], axis=0)
    stab = jnp.concatenate([sin_p, jnp.tile(sin_s, (tm // nq, 1))], axis=0)
    n_ptiles, tiles_per_seq = n_p // tm, l // tm
    tab_index = lambda i: jnp.where(i < n_ptiles, i % tiles_per_seq, tiles_per_seq)

    lat_rows, rope_rows, p_wkv, p_shift, s_wkv, s_shift = [], [], [], [], [], []
    vf_p = vf_s = None
    for i in range(depth):
        x = _ffn(x, ffn_norm[i, 0], ffn_w_gate[i, 0].astype(BF16), ffn_w_up[i, 0].astype(BF16),
                 ffn_w_down[i, 0].astype(BF16))
        if i % 2 == 0:
            m = i // 2
            w = _mla_weights(m, mla_w_down, mla_g_q_lat, mla_g_kv_lat, mla_w_uq, mla_w_uk, mla_w_uv,
                             mla_g_qn, mla_g_kn)
            q, c, kr, k, v = _mla_proj(x, mix_norm[i], w, ctab, stab, tm, tab_index)
            o_p = _prompt_attn(q, k, v, nb, l)
            q_s = jnp.transpose(q[:, n_p:, :].reshape(MLA_HEADS, ns, nq, QK_PAD), (1, 0, 2, 3))
            q_s = q_s.reshape(ns, MLA_HEADS * nq, QK_PAD)
            o_s = _sample_attn(q_s, c, kr, n_p, cache_kv_latent, cache_k_rope, m, page_table,
                               jnp.transpose(w["wuk"]), w["wuk"], w["wuv"], w["gkn"])
            proj = (o_p, o_s, mla_w_o[m].astype(BF16), None, None)
            lat_rows.append(c)
            rope_rows.append(kr)
        else:
            j = i // 2
            mu, w = _rwkv_weights(j, rwkv_mu, rwkv_w_r, rwkv_w_k, rwkv_w_v, rwkv_w0, rwkv_w1, rwkv_w2,
                                  rwkv_a0, rwkv_a1, rwkv_a2, rwkv_v0, rwkv_v1, rwkv_v2, rwkv_g1, rwkv_g2)
            params = [rwkv_k_k[j], rwkv_k_a[j], rwkv_r_k[j], rwkv_lnx_w[j], rwkv_lnx_b[j]]
            r, k, v, lw, alr, gate_p, hl = _rwkv_proj(x, 0, n_p, mix_norm[i], mu, w, vf_p, None, l)
            if j == 0:
                vf_p = v
            z_p, st_p = _wkv(r, k, v, lw, alr, params, None, nb, l)
            p_wkv.append(st_p)
            p_shift.append(hl.reshape(nb, -1, 8, D_MODEL)[:, -1, 7])
            shift_exp = jnp.zeros((ns, nq, D_MODEL), F32).at[:, 0, :].set(state_shift[j]).reshape(n_s, D_MODEL)
            r, k, v, lw, alr, gate_s, h_s = _rwkv_proj(x, n_p, n_s, mix_norm[i], mu, w, vf_s, shift_exp, nq)
            if j == 0:
                vf_s = v
            z_s, st_s = _wkv(r, k, v, lw, alr, params, state_wkv[j], ns, nq)
            s_wkv.append(st_s)
            s_shift.append(h_s.reshape(ns, nq, D_MODEL)[:, -1])
            proj = (z_p, z_s, rwkv_w_o[j].astype(BF16), gate_p, gate_s)
        x = _ffn(x, ffn_norm[i, 1], ffn_w_gate[i, 1].astype(BF16), ffn_w_up[i, 1].astype(BF16),
                 ffn_w_down[i, 1].astype(BF16), proj)

    lat = jnp.stack(lat_rows)
    rope = jnp.stack(rope_rows)
    return (x[:n_p].reshape(nb, l, D_MODEL), x[n_p:].reshape(ns, nq, D_MODEL),
            lat[:, :n_p].reshape(-1, nb, l, KV_RANK), rope[:, :n_p].reshape(-1, nb, l, ROPE_DIM),
            jnp.stack(p_wkv), jnp.stack(p_shift),
            lat[:, n_p:].reshape(-1, ns, nq, KV_RANK), rope[:, n_p:].reshape(-1, ns, nq, ROPE_DIM),
            jnp.stack(s_wkv), jnp.stack(s_shift))
```

```python
import functools

import jax
import jax.numpy as jnp
from jax import lax
from jax.experimental import pallas as pl
from jax.experimental.pallas import tpu as pltpu

F32 = jnp.float32
BF16 = jnp.bfloat16

D_MODEL = 1024
D_FF = 2816
RMS_EPS = 1e-6
MLA_HEADS = 8
Q_RANK = 384
KV_RANK = 256
NOPE_DIM = 128
ROPE_DIM = 64
QK_DIM = NOPE_DIM + ROPE_DIM
QK_PAD = 256
V_DIM = 128
ROPE_THETA = 10000.0
ATTN_SCALE = QK_DIM ** -0.5
PAGE_SIZE = 128
HEAD_SIZE = 64
RWKV_HEADS = D_MODEL // HEAD_SIZE
HEAD_PAIRS = RWKV_HEADS // 2
PAIR_W = 2 * HEAD_SIZE
LNX_EPS = 64e-5
WKV_CHUNK = 64

VMEM_LIMIT = 48 * 1024 * 1024


def _cparams(sem):
    return pltpu.CompilerParams(dimension_semantics=sem, vmem_limit_bytes=VMEM_LIMIT)


def _pick(n, cands):
    for c in cands:
        if n % c == 0:
            return c
    raise ValueError(f"no tile in {cands} divides {n}")


def _rms(x, g):
    return x * lax.rsqrt(jnp.mean(x * x, axis=-1, keepdims=True) + RMS_EPS) * g


def _dot(a, b):
    return jnp.dot(a, b, preferred_element_type=F32)


def _dot_nt(a, b):
    return lax.dot_general(a, b, (((1,), (1,)), ((), ())), preferred_element_type=F32)


def _dot_tn(a, b):
    return lax.dot_general(a, b, (((0,), (0,)), ((), ())), preferred_element_type=F32)


FFN_CHUNK = 256


def _swiglu_res(x, g_ref, wg_ref, wu_ref, wd_ref):
    hb = _rms(x, g_ref[...]).astype(BF16)
    acc = x
    for c in range(D_FF // FFN_CHUNK):
        sl = slice(c * FFN_CHUNK, (c + 1) * FFN_CHUNK)
        a = _dot(hb, wg_ref[:, sl])
        u = _dot(hb, wu_ref[:, sl])
        act = (0.5 * a * jax.nn.sigmoid(a) * u).astype(BF16)
        acc = acc + _dot(act, wd_ref[sl, :])
    return acc


def _ffn_kernel(x_ref, g_ref, wg_ref, wu_ref, wd_ref, o_ref):
    o_ref[...] = _swiglu_res(x_ref[...], g_ref, wg_ref, wu_ref, wd_ref)


def _ffn_parts_kernel(*refs, npt, two_x, proj, gated):
    it = iter(refs)
    xp_ref = next(it)
    xs_ref = next(it) if two_x else xp_ref
    ap_ref = as_ref = gp_ref = gs_ref = wp_ref = None
    if proj:
        ap_ref, as_ref = next(it), next(it)
        if gated:
            gp_ref, gs_ref = next(it), next(it)
        wp_ref = next(it)
    g_ref, wg_ref, wu_ref, wd_ref = (next(it) for _ in range(4))
    o_ref = next(it)
    x1_ref = next(it)
    i = pl.program_id(0)

    def stage(x_ref, a_ref, gate_ref):
        x = x_ref[...]
        if proj:
            a = a_ref[...] if gate_ref is None else a_ref[...] * gate_ref[...]
            x = x + _dot(a.astype(BF16), wp_ref[...])
        x1_ref[...] = x

    @pl.when(i < npt)
    def _():
        stage(xp_ref, ap_ref, gp_ref)

    @pl.when(i >= npt)
    def _():
        stage(xs_ref, as_ref, gs_ref)

    o_ref[...] = _swiglu_res(x1_ref[...], g_ref, wg_ref, wu_ref, wd_ref)


def _resident(shape):
    return pl.BlockSpec(shape, lambda i: (0,) * len(shape), pipeline_mode=pl.Buffered(1))


def _ffn(x, g, wg, wu, wd, proj=None):
    two_x = isinstance(x, tuple)
    n = x[0].shape[0] + x[1].shape[0] if two_x else x.shape[0]
    wspecs = [_resident((1, D_MODEL)), _resident(wg.shape), _resident(wu.shape), _resident(wd.shape)]
    wargs = (g.reshape(1, D_MODEL), wg, wu, wd)
    cparams = pltpu.CompilerParams(dimension_semantics=("parallel",), vmem_limit_bytes=56 * 1024 * 1024)
    if not two_x and proj is None:
        tm = _pick(n, (512, 256, 128, 64, 32, 16, 8))
        xspec = pl.BlockSpec((tm, D_MODEL), lambda i: (i, 0))
        return pl.pallas_call(
            _ffn_kernel, grid=(n // tm,), in_specs=[xspec] + wspecs, out_specs=xspec,
            out_shape=jax.ShapeDtypeStruct((n, D_MODEL), F32), compiler_params=cparams, name="ffn",
        )(x, *wargs)

    if two_x:
        n_p, n_s = x[0].shape[0], x[1].shape[0]
    else:
        n_p, n_s = proj[0].shape[0], proj[1].shape[0]
    tm = _pick(n_s, (512, 256, 128, 64, 32, 16, 8))
    assert n == n_p + n_s and n_p % tm == 0
    npt = n_p // tm
    row = lambda width: pl.BlockSpec((tm, width), lambda i: (i, 0))
    prm = lambda width: pl.BlockSpec((tm, width), lambda i: (jnp.minimum(i, npt - 1), 0))
    smp = lambda width: pl.BlockSpec((tm, width), lambda i: (jnp.maximum(i - npt, 0), 0))
    specs, args = ([prm(D_MODEL), smp(D_MODEL)], list(x)) if two_x else ([row(D_MODEL)], [x])
    gated = False
    if proj is not None:
        a_p, a_s, w, gate_p, gate_s = proj
        kin = a_p.shape[1]
        gated = gate_p is not None
        specs += [prm(kin), smp(kin)] + ([prm(kin), smp(kin)] if gated else []) + [_resident(w.shape)]
        args += [a_p, a_s] + ([gate_p, gate_s] if gated else []) + [w]
    return pl.pallas_call(
        functools.partial(_ffn_parts_kernel, npt=npt, two_x=two_x, proj=proj is not None, gated=gated),
        grid=(n // tm,),
        in_specs=specs + wspecs,
        out_specs=row(D_MODEL),
        out_shape=jax.ShapeDtypeStruct((n, D_MODEL), F32),
        scratch_shapes=[pltpu.VMEM((tm, D_MODEL), F32)],
        compiler_params=cparams,
        name="ffn",
    )(*args, *wargs)


def _mla_proj_kernel(x_ref, g_ref, wd_ref, gq_ref, gkv_ref, wqa_ref, wqb_ref, wuk_ref, wuv_ref,
                     gqn_ref, gkn_ref, ct_ref, st_ref,
                     q_ref, c_ref, kr_ref, k_ref, v_ref):
    hb = _rms(x_ref[...], g_ref[...]).astype(BF16)
    lat = _dot(hb, wd_ref[...])
    cq = _rms(lat[:, :Q_RANK], gq_ref[...]).astype(BF16)
    c = _rms(lat[:, Q_RANK:Q_RANK + KV_RANK], gkv_ref[...])
    ct = ct_ref[...]
    st = st_ref[...]
    o = Q_RANK + KV_RANK
    kr = lat[:, o:o + 128] * ct + lat[:, o + 128:o + 256] * st
    c_ref[...] = c
    kr_ref[...] = kr[:, :ROPE_DIM]

    qa = _dot(cq, wqa_ref[...])
    qb = _dot(cq, wqb_ref[...])
    gqn = gqn_ref[...]
    gkn = gkn_ref[...]
    cb = c.astype(BF16)
    kn = _dot(cb, wuk_ref[...])
    vv = _dot(cb, wuv_ref[...])
    ssr = jnp.sum(kr * kr, axis=-1, keepdims=True)
    for h in range(MLA_HEADS):
        nope = qa[:, h * QK_PAD:h * QK_PAD + 128]
        rp = qa[:, h * QK_PAD + 128:(h + 1) * QK_PAD] * ct + qb[:, h * 128:(h + 1) * 128] * st
        ss = jnp.sum(nope * nope, axis=-1, keepdims=True) + jnp.sum(rp * rp, axis=-1, keepdims=True)
        rs = lax.rsqrt(ss * (1.0 / QK_DIM) + RMS_EPS)
        q_ref[h, :, 0:128] = (nope * rs * gqn[:, 0:128]).astype(BF16)
        q_ref[h, :, 128:256] = (rp * rs * gqn[:, 128:256]).astype(BF16)
        knh = kn[:, h * 128:(h + 1) * 128]
        rk = lax.rsqrt((jnp.sum(knh * knh, axis=-1, keepdims=True) + ssr) * (1.0 / QK_DIM) + RMS_EPS)
        k_ref[h, :, 0:128] = (knh * rk * gkn[:, 0:128]).astype(BF16)
        k_ref[h, :, 128:256] = (kr * rk * gkn[:, 128:256]).astype(BF16)
        v_ref[h] = vv[:, h * 128:(h + 1) * 128].astype(BF16)


def _rot_cols(w):
    half = ROPE_DIM // 2
    return jnp.concatenate([-w[..., half:], w[..., :half]], axis=-1)


def _pad_lanes(w, n):
    return jnp.pad(w, [(0, 0)] * (w.ndim - 1) + [(0, n - w.shape[-1])])


def _mla_weights(m, mla_w_down, mla_g_q_lat, mla_g_kv_lat, mla_w_uq, mla_w_uk, mla_w_uv, mla_g_qn, mla_g_kn):
    wd = mla_w_down[m]
    o = Q_RANK + KV_RANK
    wkr = wd[:, o:]
    wd_ext = jnp.concatenate([wd[:, :o], _pad_lanes(wkr, 128), _pad_lanes(_rot_cols(wkr), 128)], axis=1)
    wq = mla_w_uq[m].reshape(Q_RANK, MLA_HEADS, QK_DIM)
    wqa = _pad_lanes(wq, QK_PAD).reshape(Q_RANK, MLA_HEADS * QK_PAD)
    wqb = _pad_lanes(_rot_cols(wq[..., NOPE_DIM:]), 128).reshape(Q_RANK, MLA_HEADS * 128)
    return dict(
        wd=wd_ext.astype(BF16), gq=mla_g_q_lat[m].reshape(1, Q_RANK), gkv=mla_g_kv_lat[m].reshape(1, KV_RANK),
        wqa=wqa.astype(BF16), wqb=wqb.astype(BF16),
        wuk=mla_w_uk[m].reshape(KV_RANK, MLA_HEADS * NOPE_DIM).astype(BF16),
        wuv=mla_w_uv[m].reshape(KV_RANK, MLA_HEADS * V_DIM).astype(BF16),
        gqn=_pad_lanes(mla_g_qn[m] * ATTN_SCALE, QK_PAD).reshape(1, QK_PAD),
        gkn=_pad_lanes(mla_g_kn[m], QK_PAD).reshape(1, QK_PAD),
    )


def _rope_tables(pos):
    inv = ROPE_THETA ** (-jnp.arange(0, ROPE_DIM, 2, dtype=F32) / ROPE_DIM)
    ang = pos.astype(F32)[:, None] * inv[None, :]
    cos, sin = jnp.cos(ang), jnp.sin(ang)
    return (_pad_lanes(jnp.concatenate([cos, cos], axis=-1), 128),
            _pad_lanes(jnp.concatenate([sin, sin], axis=-1), 128))


def _mla_proj(x, g, w, ctab, stab, tm, tab_index):
    n = x.shape[0]
    full = lambda shape: pl.BlockSpec(shape, lambda i: (0,) * len(shape))
    tspec = pl.BlockSpec((tm, 128), lambda i: (tab_index(i), 0))
    return pl.pallas_call(
        _mla_proj_kernel,
        grid=(n // tm,),
        in_specs=[
            pl.BlockSpec((tm, D_MODEL), lambda i: (i, 0)),
            full((1, D_MODEL)), full(w["wd"].shape), full((1, Q_RANK)), full((1, KV_RANK)),
            full(w["wqa"].shape), full(w["wqb"].shape), full(w["wuk"].shape), full(w["wuv"].shape),
            full((1, QK_PAD)), full((1, QK_PAD)), tspec, tspec,
        ],
        out_specs=[
            pl.BlockSpec((MLA_HEADS, tm, QK_PAD), lambda i: (0, i, 0)),
            pl.BlockSpec((tm, KV_RANK), lambda i: (i, 0)),
            pl.BlockSpec((tm, ROPE_DIM), lambda i: (i, 0)),
            pl.BlockSpec((MLA_HEADS, tm, QK_PAD), lambda i: (0, i, 0)),
            pl.BlockSpec((MLA_HEADS, tm, V_DIM), lambda i: (0, i, 0)),
        ],
        out_shape=[
            jax.ShapeDtypeStruct((MLA_HEADS, n, QK_PAD), BF16),
            jax.ShapeDtypeStruct((n, KV_RANK), F32),
            jax.ShapeDtypeStruct((n, ROPE_DIM), F32),
            jax.ShapeDtypeStruct((MLA_HEADS, n, QK_PAD), BF16),
            jax.ShapeDtypeStruct((MLA_HEADS, n, V_DIM), BF16),
        ],
        compiler_params=_cparams(("parallel",)),
        name="mla_proj",
    )(x, g.reshape(1, D_MODEL), w["wd"], w["gq"], w["gkv"], w["wqa"], w["wqb"], w["wuk"], w["wuv"],
      w["gqn"], w["gkn"], ctab, stab)


def _prompt_attn_kernel(q_ref, k_ref, v_ref, o_ref, *, tq):
    l = q_ref.shape[1]
    nq = l // tq
    row = lax.broadcasted_iota(jnp.int32, (tq, tq), 0)
    col = lax.broadcasted_iota(jnp.int32, (tq, tq), 1)
    diag_mask = row >= col
    for qi in range(nq):
        q = q_ref[0, qi * tq:(qi + 1) * tq, :]
        m = l_sum = acc = None
        for ki in range(qi + 1):
            k = k_ref[0, ki * tq:(ki + 1) * tq, :]
            v = v_ref[0, ki * tq:(ki + 1) * tq, :]
            s = _dot_nt(q, k)
            if ki == qi:
                s = jnp.where(diag_mask, s, -jnp.inf)
            m_blk = jnp.max(s, axis=-1, keepdims=True)
            if ki == 0:
                m = m_blk
                p = jnp.exp(s - m)
                l_sum = jnp.sum(p, axis=-1, keepdims=True)
                acc = _dot(p.astype(BF16), v)
            else:
                m_new = jnp.maximum(m, m_blk)
                alpha = jnp.exp(m - m_new)
                p = jnp.exp(s - m_new)
                l_sum = alpha * l_sum + jnp.sum(p, axis=-1, keepdims=True)
                acc = alpha * acc + _dot(p.astype(BF16), v)
                m = m_new
        o_ref[qi * tq:(qi + 1) * tq, :] = (acc / l_sum).astype(o_ref.dtype)


def _prompt_attn(q, k, v, nb, l):
    tq = _pick(l, (512, 256, 128))
    return pl.pallas_call(
        functools.partial(_prompt_attn_kernel, tq=tq),
        grid=(nb, MLA_HEADS),
        in_specs=[
            pl.BlockSpec((1, l, QK_PAD), lambda b, h: (h, b, 0)),
            pl.BlockSpec((1, l, QK_PAD), lambda b, h: (h, b, 0)),
            pl.BlockSpec((1, l, V_DIM), lambda b, h: (h, b, 0)),
        ],
        out_specs=pl.BlockSpec((l, V_DIM), lambda b, h: (b, h)),
        out_shape=jax.ShapeDtypeStruct((nb * l, MLA_HEADS * V_DIM), BF16),
        compiler_params=_cparams(("parallel", "parallel")),
        name="prompt_attn",
    )(q, k, v)


def _sample_attn_kernel(pt_ref, q_ref, cn_ref, krn_ref, wukt_ref, wuk_ref, wuv_ref, gkn_ref, cc_hbm, ckr_hbm,
                        o_ref, wq_ref, cb_ref, cbuf, kbuf, sem, *, m, n_pages, tc):
    b = pl.program_id(0)
    nseq = pl.num_programs(0)
    slot = b % 2
    nq = q_ref.shape[1] // MLA_HEADS
    rows = q_ref.shape[1]
    n_nope = MLA_HEADS * NOPE_DIM

    def page_copies(page, slot_, j):
        return (pltpu.make_async_copy(cc_hbm.at[m, page], cbuf.at[slot_, j], sem.at[slot_]),
                pltpu.make_async_copy(ckr_hbm.at[m, page], kbuf.at[slot_, j], sem.at[slot_]))

    def start_pages(seq, slot_):
        for j in range(n_pages):
            for cp in page_copies(pt_ref[seq * n_pages + j], slot_, j):
                cp.start()

    def wait_pages(slot_):
        for j in range(n_pages):
            for cp in page_copies(0, slot_, j):
                cp.wait()

    @pl.when(b == 0)
    def _():
        start_pages(0, 0)
        wq_ref[0:n_nope, :] = wukt_ref[...]

    wait_pages(slot)
    start_pages(jnp.where(b + 1 == nseq, 0, b + 1), 1 - slot)

    gkn = gkn_ref[...]
    qts, qrs = [], []
    for h in range(MLA_HEADS):
        qh = q_ref[0, h * nq:(h + 1) * nq, :].astype(F32)
        qg = (qh[:, :NOPE_DIM] * gkn[:, :NOPE_DIM]).astype(BF16)
        qts.append(_dot_nt(qg, wuk_ref[:, h * NOPE_DIM:(h + 1) * NOPE_DIM]))
        qrs.append(qh[:, NOPE_DIM:QK_DIM] * gkn[:, NOPE_DIM:QK_DIM])
    wq_ref[n_nope:n_nope + rows, :] = jnp.concatenate(qts, axis=0).astype(BF16)
    qr = jnp.concatenate(qrs, axis=0).astype(BF16)

    def scores(cbc, ssr, s_rope):
        t = cbc.shape[0]
        big = _dot_nt(wq_ref[...], cbc)
        rs = []
        for h in range(MLA_HEADS):
            kh = big[h * NOPE_DIM:(h + 1) * NOPE_DIM]
            ssq = jnp.sum(kh * kh, axis=0, keepdims=True)
            rs_h = lax.rsqrt((ssq + ssr) * (1.0 / QK_DIM) + RMS_EPS)
            rs.append(jnp.broadcast_to(rs_h, (nq, t)))
        return (big[n_nope:] + s_rope) * jnp.concatenate(rs, axis=0)

    pages_per_chunk = tc // PAGE_SIZE
    for j in range(n_pages):
        cb_ref[j * PAGE_SIZE:(j + 1) * PAGE_SIZE, :] = cbuf[slot, j].astype(BF16)
    s_list, c_list = [], []
    for ci in range(n_pages // pages_per_chunk):
        cbc = cb_ref[ci * tc:(ci + 1) * tc, :]
        krt = jnp.concatenate([kbuf[slot, ci * pages_per_chunk + jj] for jj in range(pages_per_chunk)], axis=1)
        ssr = jnp.sum(krt * krt, axis=0, keepdims=True)
        s_list.append(scores(cbc, ssr, _dot(qr, krt.astype(BF16))))
        c_list.append(cbc)

    pad = PAGE_SIZE - nq
    cn = jnp.concatenate([cn_ref[...], jnp.zeros((pad, KV_RANK), F32)], axis=0).astype(BF16)
    krn = jnp.concatenate([krn_ref[...], jnp.zeros((pad, ROPE_DIM), F32)], axis=0)
    ssr = _dot_nt(jnp.ones((8, ROPE_DIM), BF16), (krn * krn).astype(BF16))[0:1]
    s = scores(cn, ssr, _dot_nt(qr, krn.astype(BF16)))
    qpos = lax.broadcasted_iota(jnp.int32, (rows, PAGE_SIZE), 0) % nq
    tok = lax.broadcasted_iota(jnp.int32, (rows, PAGE_SIZE), 1)
    s_list.append(jnp.where(tok <= qpos, s, -jnp.inf))
    c_list.append(cn)

    mx = jnp.max(s_list[0], axis=-1, keepdims=True)
    for s in s_list[1:]:
        mx = jnp.maximum(mx, jnp.max(s, axis=-1, keepdims=True))
    ps = [jnp.exp(s - mx) for s in s_list]
    den = jnp.sum(ps[0], axis=-1, keepdims=True)
    for p in ps[1:]:
        den = den + jnp.sum(p, axis=-1, keepdims=True)
    acc = _dot(ps[0].astype(BF16), c_list[0])
    for p, cbc in zip(ps[1:], c_list[1:]):
        acc = acc + _dot(p.astype(BF16), cbc)
    o_lat = (acc / den).astype(BF16)
    for h in range(MLA_HEADS):
        o_ref[:, h * V_DIM:(h + 1) * V_DIM] = _dot(o_lat[h * nq:(h + 1) * nq], wuv_ref[:, h * V_DIM:(h + 1) * V_DIM])

    @pl.when(b == nseq - 1)
    def _():
        wait_pages(1 - slot)


def _sample_attn(q_s, c_all, kr_all, row0, cache_c, cache_kr, m, page_table, wukt, wuk, wuv, gkn):
    ns, n_pages = page_table.shape
    rows = q_s.shape[1]
    nq = rows // MLA_HEADS
    assert row0 % nq == 0 and nq % 8 == 0
    tc = 2 * PAGE_SIZE
    assert n_pages % 2 == 0 and n_pages <= 64
    blk0 = row0 // nq
    full = lambda shape: pl.BlockSpec(shape, lambda b, pt: (0,) * len(shape))

    cache_kr_t = jnp.swapaxes(cache_kr, 2, 3)
    grid_spec = pltpu.PrefetchScalarGridSpec(
        num_scalar_prefetch=1,
        grid=(ns,),
        in_specs=[
            pl.BlockSpec((1, rows, QK_PAD), lambda b, pt: (b, 0, 0)),
            pl.BlockSpec((nq, KV_RANK), lambda b, pt: (blk0 + b, 0)),
            pl.BlockSpec((nq, ROPE_DIM), lambda b, pt: (blk0 + b, 0)),
            full(wukt.shape), full(wuk.shape), full(wuv.shape), full((1, QK_PAD)),
            pl.BlockSpec(memory_space=pl.ANY), pl.BlockSpec(memory_space=pl.ANY),
        ],
        out_specs=pl.BlockSpec((nq, MLA_HEADS * V_DIM), lambda b, pt: (b, 0)),
        scratch_shapes=[
            pltpu.VMEM((MLA_HEADS * NOPE_DIM + rows, KV_RANK), BF16),
            pltpu.VMEM((n_pages * PAGE_SIZE, KV_RANK), BF16),
            pltpu.VMEM((2, n_pages, PAGE_SIZE, KV_RANK), F32),
            pltpu.VMEM((2, n_pages, ROPE_DIM, PAGE_SIZE), F32),
            pltpu.SemaphoreType.DMA((2,)),
        ],
    )
    return pl.pallas_call(
        functools.partial(_sample_attn_kernel, m=m, n_pages=n_pages, tc=tc),
        grid_spec=grid_spec,
        out_shape=jax.ShapeDtypeStruct((ns * nq, MLA_HEADS * V_DIM), F32),
        compiler_params=_cparams(("arbitrary",)),
        name="sample_attn",
    )(page_table.reshape(-1), q_s, c_all, kr_all, wukt, wuk, wuv, gkn, cache_c, cache_kr_t)


def _softplus(y):
    return jnp.maximum(y, 0.0) + jnp.log(1.0 + jnp.exp(-jnp.abs(y)))


def _rwkv_proj_kernel(*refs, sample, has_vres, tiles_per_seq, nq):
    it = iter(refs)
    x_ref, g_ref, mu_ref = next(it), next(it), next(it)
    shift_ref = next(it) if sample else None
    wr, wk, wv, w0, w1, w2, a0, a1, a2, g1, g2 = (next(it) for _ in range(11))
    if has_vres:
        v0, v1, v2, vf_ref = (next(it) for _ in range(4))
    r_ref, k_ref, v_ref, lw_ref, alr_ref, gate_ref, hl_ref = (next(it) for _ in range(7))
    carry_ref = next(it)
    tm = x_ref.shape[0]
    i = pl.program_id(0)

    h = _rms(x_ref[...], g_ref[...])
    rolled = pltpu.roll(h, 1, axis=0)
    row = lax.broadcasted_iota(jnp.int32, (tm, 1), 0)
    if sample:
        prev = jnp.where(row % nq == 0, shift_ref[...], rolled)
        hl_ref[...] = h
    else:
        @pl.when(i == 0)
        def _():
            carry_ref[...] = jnp.zeros(carry_ref.shape, F32)

        first = jnp.where(i % tiles_per_seq == 0, 0.0, carry_ref[0:1, :])
        prev = jnp.where(row == 0, first, rolled)
        carry_ref[0:1, :] = h[tm - 1:tm, :]
        hl_ref[...] = h[tm - 8:tm, :]
    xx = prev - h
    mu = mu_ref[...]
    mix = lambda n: (h + xx * mu[n:n + 1, :]).astype(BF16)
    xr, xw, xk, xv, xa, xg = (mix(n) for n in range(6))
    r = _dot(xr, wr[...])
    k = _dot(xk, wk[...])
    v = _dot(xv, wv[...])
    wl = w0[...] + _dot(jnp.tanh(_dot(xw, w1[...])).astype(BF16), w2[...])
    lw = -jnp.exp(-_softplus(-wl) - 0.5)
    alr = jax.nn.sigmoid(a0[...] + _dot(_dot(xa, a1[...]).astype(BF16), a2[...]))
    gate_ref[...] = _dot(jax.nn.sigmoid(_dot(xg, g1[...])).astype(BF16), g2[...])
    if has_vres:
        vgate = jax.nn.sigmoid(v0[...] + _dot(_dot(xv, v1[...]).astype(BF16), v2[...]))
    for p in range(HEAD_PAIRS):
        sl = slice(p * PAIR_W, (p + 1) * PAIR_W)
        vp = v[:, sl]
        if has_vres:
            vp = vp + (vf_ref[p] - vp) * vgate[:, sl]
        r_ref[p] = r[:, sl]
        k_ref[p] = k[:, sl]
        v_ref[p] = vp
        lw_ref[p] = lw[:, sl]
        alr_ref[p] = alr[:, sl]


def _rwkv_weights(j, rwkv_mu, rwkv_w_r, rwkv_w_k, rwkv_w_v, rwkv_w0, rwkv_w1, rwkv_w2, rwkv_a0, rwkv_a1,
                  rwkv_a2, rwkv_v0, rwkv_v1, rwkv_v2, rwkv_g1, rwkv_g2):
    row = lambda a: a.reshape(1, D_MODEL)
    w = [rwkv_w_r[j].astype(BF16), rwkv_w_k[j].astype(BF16), rwkv_w_v[j].astype(BF16),
         row(rwkv_w0[j]), rwkv_w1[j].astype(BF16), rwkv_w2[j].astype(BF16),
         row(rwkv_a0[j]), rwkv_a1[j].astype(BF16), rwkv_a2[j].astype(BF16),
         rwkv_g1[j].astype(BF16), rwkv_g2[j].astype(BF16)]
    if j > 0:
        w += [row(rwkv_v0[j - 1]), rwkv_v1[j - 1].astype(BF16), rwkv_v2[j - 1].astype(BF16)]
    return rwkv_mu[j], w


def _rwkv_proj(x, row0, n, g, mu, weights, v_first, shift_exp, seq_len):
    sample = shift_exp is not None
    has_vres = v_first is not None
    tm = _pick(n if sample else seq_len, (256, 128, 64, 32, 16, 8))
    assert row0 % tm == 0 and n % tm == 0
    off = row0 // tm
    if sample:
        assert tm % seq_len == 0
        tiles_per_seq = 1
    else:
        assert seq_len % tm == 0
        tiles_per_seq = seq_len // tm
    full = lambda a: pl.BlockSpec(a.shape, lambda i: (0,) * a.ndim)
    pair_spec = pl.BlockSpec((HEAD_PAIRS, tm, PAIR_W), lambda i: (0, i, 0))
    row_spec = pl.BlockSpec((tm, D_MODEL), lambda i: (i, 0))
    g2d = g.reshape(1, D_MODEL)
    args = [x, g2d, mu]
    specs = [pl.BlockSpec((tm, D_MODEL), lambda i: (i + off, 0)), full(g2d), full(mu)]
    if sample:
        args.append(shift_exp)
        specs.append(row_spec)
    args += weights
    specs += [full(a) for a in weights]
    if has_vres:
        args.append(v_first)
        specs.append(pair_spec)
    pair_shape = jax.ShapeDtypeStruct((HEAD_PAIRS, n, PAIR_W), F32)
    if sample:
        hl_spec, hl_shape = row_spec, jax.ShapeDtypeStruct((n, D_MODEL), F32)
    else:
        hl_spec = pl.BlockSpec((8, D_MODEL), lambda i: (i, 0))
        hl_shape = jax.ShapeDtypeStruct((n // tm * 8, D_MODEL), F32)
    return pl.pallas_call(
        functools.partial(_rwkv_proj_kernel, sample=sample, has_vres=has_vres,
                          tiles_per_seq=tiles_per_seq, nq=seq_len),
        grid=(n // tm,),
        in_specs=specs,
        out_specs=[pair_spec] * 5 + [row_spec, hl_spec],
        out_shape=[pair_shape] * 5 + [jax.ShapeDtypeStruct((n, D_MODEL), F32), hl_shape],
        scratch_shapes=[pltpu.VMEM((8, D_MODEL), F32)],
        compiler_params=_cparams(("arbitrary",)),
        name="rwkv_proj",
    )(*args)


def _wkv_masks(c, seg):
    c2 = 2 * c
    ri = lax.broadcasted_iota(jnp.int32, (c2, c2), 0)
    cj = lax.broadcasted_iota(jnp.int32, (c2, c2), 1)
    t, s = ri % c, cj % c
    same = ((ri // c) == (cj // c)) & ((t // seg) == (s // seg))
    levels = []
    step = 1
    while step < seg:
        levels.append(same & ((t ^ s) < 2 * step) & ((t & step) != 0) & ((s & step) == 0))
        step *= 2
    ti = lax.broadcasted_iota(jnp.int32, (c, c), 0)
    tj = lax.broadcasted_iota(jnp.int32, (c, c), 1)
    same_seq = (ti // seg) == (tj // seg)
    lane = lax.broadcasted_iota(jnp.int32, (1, PAIR_W), 1)
    row = lax.broadcasted_iota(jnp.int32, (c2, 1), 0)
    ei = lax.broadcasted_iota(jnp.int32, (PAIR_W, PAIR_W), 0)
    ej = lax.broadcasted_iota(jnp.int32, (PAIR_W, PAIR_W), 1)
    return dict(
        strict=same & (t > s), incl=same & (t >= s), levels=levels,
        tri=jnp.where(same_seq & (ti >= tj), 1.0, 0.0).astype(BF16),
        seq_ones=jnp.where(same_seq, 1.0, 0.0).astype(BF16),
        m0=lane < HEAD_SIZE, own=(row < c) == (lane < HEAD_SIZE), eye=ei == ej,
    )


def _wkv_local(chunks, prm, msk, seg):
    kkp, kap, rkp, lnw, lnb = prm
    n = len(chunks)
    c = chunks[0][0].shape[0]
    c2 = 2 * c
    nseg = c // seg
    m0 = msk["m0"]
    tri = msk["tri"]
    b16 = lambda a: a.astype(BF16)
    cat0 = lambda a, b: jnp.concatenate([a, b], axis=0)
    cat1 = lambda a, b: jnp.concatenate([a, b], axis=1)

    def stack(a):
        return cat0(jnp.where(m0, a, 0.0), jnp.where(m0, 0.0, a))

    def seq_rows(a, g):
        return a if nseg == 1 else cat0(a[g * seg:(g + 1) * seg], a[c + g * seg:c + (g + 1) * seg])

    cums, tots = [], []
    for (_, _, _, lw, _) in chunks:
        hi = b16(lw)
        r1 = lw - hi.astype(F32)
        mid = b16(r1)
        lo = b16(r1 - mid.astype(F32))
        cum = _dot(tri, hi) + _dot(tri, mid) + _dot(tri, lo)
        cums.append(cum)
        if nseg == 1:
            tots.append(cum[c - 1:c, :])
        else:
            ones = msk["seq_ones"]
            tots.append(_dot(ones, hi) + _dot(ones, mid) + _dot(ones, lo))

    lhs, rhs, at, rt, bh, kh, v16, vs, bonus, gcol = ([] for _ in range(10))
    for (r, k, v, lw, alr), cum, tot in zip(chunks, cums, tots):
        e_in, e_ex = jnp.exp(cum), jnp.exp(cum - lw)
        e_inv, e_end = jnp.exp(-cum), jnp.exp(tot - cum)
        kk = stack(k * kkp)
        kk = kk * (1.0 / jnp.maximum(jnp.sqrt(jnp.sum(kk * kk, axis=-1, keepdims=True)), 1e-12))
        kmod = stack(k * (1.0 + (alr - 1.0) * kap))
        r_s, v_s = stack(r), stack(v)
        bb = kk * stack(alr)
        a_t, r_t = -kk * cat0(e_ex, e_ex), r_s * cat0(e_in, e_in)
        lhs.append(b16(cat0(a_t, r_t)))
        rhs.append(b16(cat0(bb * cat0(e_inv, e_inv), kmod * cat0(e_inv, e_inv))))
        at.append(a_t)
        rt.append(r_t)
        bh.append(bb * cat0(e_end, e_end))
        kh.append(kmod * cat0(e_end, e_end))
        v16.append(b16(v_s))
        vs.append(v_s)
        bonus.append(jnp.sum(r_s * kmod * rkp, axis=-1, keepdims=True) * v_s)
        gcol.append([jnp.sum(jnp.where(msk["eye"], jnp.exp(tot[g * seg:g * seg + 1, :]), 0.0), axis=1, keepdims=True)
                     for g in range(nseg)])

    yield
    gram = [_dot_nt(a, b) for a, b in zip(lhs, rhs)]
    yield
    ab = [jnp.where(msk["strict"], g[:c2, :c2], 0.0) for g in gram]
    ak = [b16(jnp.where(msk["strict"], g[:c2, c2:], 0.0)) for g in gram]
    rb = [b16(jnp.where(msk["incl"], g[c2:, :c2], 0.0)) for g in gram]
    rk = [b16(jnp.where(msk["incl"], g[c2:, c2:], 0.0)) for g in gram]
    akv = [_dot(a, v) for a, v in zip(ak, v16)]
    rkv = [_dot(a, v) for a, v in zip(rk, v16)]
    khv = [[_dot_tn(b16(seq_rows(a, g)), b16(seq_rows(v, g))) for g in range(nseg)] for a, v in zip(kh, vs)]

    levels = msk["levels"]
    e = [jnp.where(levels[0], a, 0.0) for a in ab]
    for lvl in levels[1:]:
        yield
        low = [jnp.where(lvl, a, 0.0) for a in ab]
        x = [lo_ + _dot(b16(e_), b16(lo_)) for e_, lo_ in zip(e, low)]
        yield
        e = [e_ + x_ + _dot(b16(x_), b16(e_)) for e_, x_ in zip(e, x)]

    yield
    w2 = [cat1(a, b) for a, b in zip(akv, at)]
    u2 = [w + _dot(b16(e_), b16(w)) for w, e_ in zip(w2, e)]
    yield
    y2 = [_dot(a, b16(u)) + cat1(b, r_) for a, u, b, r_ in zip(rb, u2, rkv, rt)]
    mj = [[_dot_tn(b16(seq_rows(a, g)), b16(seq_rows(u, g))) for g in range(nseg)] for a, u in zip(bh, u2)]
    return [dict(y2=y2[i], mj=mj[i], khv=khv[i], gcol=gcol[i], bonus=bonus[i]) for i in range(n)]


def _wkv_state_step(loc, prm, msk, seg, h_in):
    lnw, lnb = prm[3], prm[4]
    c2 = loc["y2"].shape[0]
    c = c2 // 2
    nseg = c // seg
    b16 = lambda a: a.astype(BF16)
    parts, h_out = [], []
    for g in range(nseg):
        h = h_in if nseg == 1 else h_in[g]
        h16 = b16(h)
        y2g = loc["y2"]
        if nseg > 1:
            y2g = jnp.concatenate([y2g[g * seg:(g + 1) * seg], y2g[c + g * seg:c + (g + 1) * seg]], axis=0)
        mj = loc["mj"][g]
        parts.append(y2g[:, :PAIR_W] + _dot(b16(y2g[:, PAIR_W:]), h16))
        h_out.append(loc["gcol"][g] * h + _dot(b16(mj[:, PAIR_W:]), h16) + (mj[:, :PAIR_W] + loc["khv"][g]))
    y = parts[0] if nseg == 1 else jnp.concatenate([p[:seg] for p in parts] + [p[seg:] for p in parts], axis=0)

    own = msk["own"]
    inv_n = 1.0 / HEAD_SIZE
    mean = jnp.sum(y, axis=-1, keepdims=True) * inv_n
    d = jnp.where(own, y - mean, 0.0)
    var = jnp.sum(d * d, axis=-1, keepdims=True) * inv_n
    z_st = d * lax.rsqrt(var + LNX_EPS) * lnw + jnp.where(own, lnb, 0.0) + loc["bonus"]
    return z_st[:c] + z_st[c:], (h_out[0] if nseg == 1 else h_out)


def _finish(gen):
    try:
        while True:
            next(gen)
    except StopIteration as stop:
        return stop.value


def _wkv_block(chunks, prm, msk, h0, seg):
    n = len(chunks)
    c = chunks[0][0].shape[0]
    zs, hs = [], []
    if seg < c:
        locs = _finish(_wkv_local(chunks, prm, msk, seg))
        for i in range(n):
            z, h_seq = _wkv_state_step(locs[i], prm, msk, seg, h0[i])
            zs.append(z)
            hs += h_seq
        return zs, hs
    half = n // 2
    h = h0
    second = _wkv_local(chunks[half:], prm, msk, seg)
    if half:
        first = _finish(_wkv_local(chunks[:half], prm, msk, seg))
        for i in range(half):
            next(second)
            z, h = _wkv_state_step(first[i], prm, msk, seg, h)
            zs.append(z)
            hs.append(h)
    for loc in _finish(second):
        z, h = _wkv_state_step(loc, prm, msk, seg, h)
        zs.append(z)
        hs.append(h)
    return zs, hs


def _state_in(s_ref, i):
    z = jnp.zeros((HEAD_SIZE, HEAD_SIZE), F32)
    top = jnp.concatenate([s_ref[i, 0], z], axis=1)
    bot = jnp.concatenate([z, s_ref[i, 1]], axis=1)
    return jnp.concatenate([top, bot], axis=0).T


def _state_out(s_ref, i, hbd):
    ht = hbd.T
    s_ref[i, 0] = ht[:HEAD_SIZE, :HEAD_SIZE]
    s_ref[i, 1] = ht[HEAD_SIZE:, HEAD_SIZE:]


def _wkv_kernel(*refs, c, seg, nb, blocks_per_seq, per_seq_state):
    data = refs[:5]
    prm = tuple(ref[0] for ref in refs[5:10])
    msk = _wkv_masks(c, seg)
    chunks = [tuple(ref[0, ci * c:(ci + 1) * c, :] for ref in data) for ci in range(nb)]
    if per_seq_state:
        s0_ref, z_ref, sout_ref = refs[10:]
        nseg = c // seg
        h0 = [[_state_in(s0_ref, ci * nseg + g) for g in range(nseg)] for ci in range(nb)]
        zs, hs = _wkv_block(chunks, prm, msk, h0, seg)
        for si in range(nb * nseg):
            _state_out(sout_ref, si, hs[si])
    else:
        z_ref, sout_ref, h_ref = refs[10:]
        b = pl.program_id(1)

        @pl.when(b % blocks_per_seq == 0)
        def _():
            h_ref[...] = jnp.zeros(h_ref.shape, F32)

        zs, hs = _wkv_block(chunks, prm, msk, h_ref[...], c)
        h_ref[...] = hs[-1]

        @pl.when(b % blocks_per_seq == blocks_per_seq - 1)
        def _():
            _state_out(sout_ref, 0, hs[-1])
    for ci in range(nb):
        z_ref[ci * c:(ci + 1) * c, :] = zs[ci]


def _pair_rows(a):
    return a.reshape(HEAD_PAIRS, 1, PAIR_W)


def _wkv(r, k, v, lw, alr, params, s0, n_seq, seq_len):
    c = WKV_CHUNK
    per_seq_state = s0 is not None
    n_chunks = n_seq * seq_len // c
    if per_seq_state:
        assert c % seq_len == 0 and (n_seq * seq_len) % c == 0
        seg, nb, blocks_per_seq = seq_len, _pick(n_chunks, (8, 4, 2, 1)), 1
        seqs = nb * (c // seg)
        sspec = pl.BlockSpec((seqs, 2, HEAD_SIZE, HEAD_SIZE), lambda p, b: (b, p, 0, 0))
        extra_in, extra_args, scratch = [sspec], [s0], []
    else:
        assert seq_len % c == 0
        seg, nb = c, _pick(seq_len // c, (16, 8, 4, 2, 1))
        blocks_per_seq = seq_len // c // nb
        sspec = pl.BlockSpec((1, 2, HEAD_SIZE, HEAD_SIZE), lambda p, b: (b // blocks_per_seq, p, 0, 0))
        extra_in, extra_args, scratch = [], [], [pltpu.VMEM((PAIR_W, PAIR_W), F32)]
    n_blocks = n_chunks // nb
    dspec = pl.BlockSpec((1, nb * c, PAIR_W), lambda p, b: (p, b, 0))
    pspec = pl.BlockSpec((1, 1, PAIR_W), lambda p, b: (p, 0, 0))
    return pl.pallas_call(
        functools.partial(_wkv_kernel, c=c, seg=seg, nb=nb, blocks_per_seq=blocks_per_seq,
                          per_seq_state=per_seq_state),
        grid=(HEAD_PAIRS, n_blocks),
        in_specs=[dspec] * 5 + [pspec] * 5 + extra_in,
        out_specs=[pl.BlockSpec((nb * c, PAIR_W), lambda p, b: (b, p)), sspec],
        out_shape=[jax.ShapeDtypeStruct((n_seq * seq_len, D_MODEL), F32),
                   jax.ShapeDtypeStruct((n_seq, RWKV_HEADS, HEAD_SIZE, HEAD_SIZE), F32)],
        scratch_shapes=scratch,
        compiler_params=_cparams(("parallel", "arbitrary")),
        name="wkv",
    )(r, k, v, lw, alr, *[_pair_rows(p) for p in params], *extra_args)


def kernel(x_prompt, x_sample, cache_kv_latent, cache_k_rope, state_wkv, state_shift, page_table,
           ffn_norm, ffn_w_gate, ffn_w_up, ffn_w_down, mix_norm,
           mla_w_down, mla_g_q_lat, mla_g_kv_lat, mla_w_uq, mla_w_uk, mla_w_uv, mla_g_qn, mla_g_kn, mla_w_o,
           rwkv_mu, rwkv_w_r, rwkv_w_k, rwkv_w_v, rwkv_w_o, rwkv_w0, rwkv_w1, rwkv_w2,
           rwkv_a0, rwkv_a1, rwkv_a2, rwkv_v0, rwkv_v1, rwkv_v2, rwkv_g1, rwkv_g2,
           rwkv_k_k, rwkv_k_a, rwkv_r_k, rwkv_lnx_w, rwkv_lnx_b):
    nb, l, _ = x_prompt.shape
    ns, nq, _ = x_sample.shape
    n_p, n_s = nb * l, ns * nq
    depth = ffn_norm.shape[0]
    past = page_table.shape[1] * PAGE_SIZE
    x = (x_prompt.reshape(n_p, D_MODEL), x_sample.reshape(n_s, D_MODEL))

    tm = _pick(n_s, (512, 256, 128, 64, 32, 16, 8))
    assert l % tm == 0 and tm % nq == 0
    cos_p, sin_p = _rope_tables(jnp.arange(l))
    cos_s, sin_s = _rope_tables(past + jnp.arange(nq))
    ctab = jnp.concatenate([cos_p, jnp.tile(cos_s, (tm // nq, 1))], axis=0)
    stab = jnp.concatenate([sin_p, jnp.tile(sin_s, (tm // nq, 1))], axis=0)
    n_ptiles, tiles_per_seq = n_p // tm, l // tm
    tab_index = lambda i: jnp.where(i < n_ptiles, i % tiles_per_seq, tiles_per_seq)

    lat_rows, rope_rows, p_wkv, p_shift, s_wkv, s_shift = [], [], [], [], [], []
    vf_p = vf_s = None
    for i in range(depth):
        x = _ffn(x, ffn_norm[i, 0], ffn_w_gate[i, 0].astype(BF16), ffn_w_up[i, 0].astype(BF16),
                 ffn_w_down[i, 0].astype(BF16))
        if i % 2 == 0:
            m = i // 2
            w = _mla_weights(m, mla_w_down, mla_g_q_lat, mla_g_kv_lat, mla_w_uq, mla_w_uk, mla_w_uv,
                             mla_g_qn, mla_g_kn)
            q, c, kr, k, v = _mla_proj(x, mix_norm[i], w, ctab, stab, tm, tab_index)
            o_p = _prompt_attn(q, k, v, nb, l)
            q_s = jnp.transpose(q[:, n_p:, :].reshape(MLA_HEADS, ns, nq, QK_PAD), (1, 0, 2, 3))
            q_s = q_s.reshape(ns, MLA_HEADS * nq, QK_PAD)
            o_s = _sample_attn(q_s, c, kr, n_p, cache_kv_latent, cache_k_rope, m, page_table,
                               jnp.transpose(w["wuk"]), w["wuk"], w["wuv"], w["gkn"])
            proj = (o_p, o_s, mla_w_o[m].astype(BF16), None, None)
            lat_rows.append(c)
            rope_rows.append(kr)
        else:
            j = i // 2
            mu, w = _rwkv_weights(j, rwkv_mu, rwkv_w_r, rwkv_w_k, rwkv_w_v, rwkv_w0, rwkv_w1, rwkv_w2,
                                  rwkv_a0, rwkv_a1, rwkv_a2, rwkv_v0, rwkv_v1, rwkv_v2, rwkv_g1, rwkv_g2)
            params = [rwkv_k_k[j], rwkv_k_a[j], rwkv_r_k[j], rwkv_lnx_w[j], rwkv_lnx_b[j]]
            r, k, v, lw, alr, gate_p, hl = _rwkv_proj(x, 0, n_p, mix_norm[i], mu, w, vf_p, None, l)
            if j == 0:
                vf_p = v
            z_p, st_p = _wkv(r, k, v, lw, alr, params, None, nb, l)
            p_wkv.append(st_p)
            p_shift.append(hl.reshape(nb, -1, 8, D_MODEL)[:, -1, 7])
            shift_exp = jnp.zeros((ns, nq, D_MODEL), F32).at[:, 0, :].set(state_shift[j]).reshape(n_s, D_MODEL)
            r, k, v, lw, alr, gate_s, h_s = _rwkv_proj(x, n_p, n_s, mix_norm[i], mu, w, vf_s, shift_exp, nq)
            if j == 0:
                vf_s = v
            z_s, st_s = _wkv(r, k, v, lw, alr, params, state_wkv[j], ns, nq)
            s_wkv.append(st_s)
            s_shift.append(h_s.reshape(ns, nq, D_MODEL)[:, -1])
            proj = (z_p, z_s, rwkv_w_o[j].astype(BF16), gate_p, gate_s)
        x = _ffn(x, ffn_norm[i, 1], ffn_w_gate[i, 1].astype(BF16), ffn_w_up[i, 1].astype(BF16),
                 ffn_w_down[i, 1].astype(BF16), proj)

    lat = jnp.stack(lat_rows)
    rope = jnp.stack(rope_rows)
    return (x[:n_p].reshape(nb, l, D_MODEL), x[n_p:].reshape(ns, nq, D_MODEL),
            lat[:, :n_p].reshape(-1, nb, l, KV_RANK), rope[:, :n_p].reshape(-1, nb, l, ROPE_DIM),
            jnp.stack(p_wkv), jnp.stack(p_shift),
            lat[:, n_p:].reshape(-1, ns, nq, KV_RANK), rope[:, n_p:].reshape(-1, ns, nq, ROPE_DIM),
            jnp.stack(s_wkv), jnp.stack(s_shift))
```

```python
import functools

import jax
import jax.numpy as jnp
from jax import lax
from jax.experimental import pallas as pl
from jax.experimental.pallas import tpu as pltpu

F32 = jnp.float32
BF16 = jnp.bfloat16

D_MODEL = 1024
D_FF = 2816
RMS_EPS = 1e-6
MLA_HEADS = 8
Q_RANK = 384
KV_RANK = 256
NOPE_DIM = 128
ROPE_DIM = 64
QK_DIM = NOPE_DIM + ROPE_DIM
QK_PAD = 256
V_DIM = 128
ROPE_THETA = 10000.0
ATTN_SCALE = QK_DIM ** -0.5
PAGE_SIZE = 128
HEAD_SIZE = 64
RWKV_HEADS = D_MODEL // HEAD_SIZE
HEAD_PAIRS = RWKV_HEADS // 2
PAIR_W = 2 * HEAD_SIZE
LNX_EPS = 64e-5
WKV_CHUNK = 64
WKV_GROUP = 8

VMEM_LIMIT = 48 * 1024 * 1024


def _cparams(sem):
    return pltpu.CompilerParams(dimension_semantics=sem, vmem_limit_bytes=VMEM_LIMIT)


def _pick(n, cands):
    for c in cands:
        if n % c == 0:
            return c
    raise ValueError(f"no tile in {cands} divides {n}")


def _rms(x, g):
    return x * lax.rsqrt(jnp.mean(x * x, axis=-1, keepdims=True) + RMS_EPS) * g


def _dot(a, b):
    return jnp.dot(a, b, preferred_element_type=F32)


def _dot_nt(a, b):
    return lax.dot_general(a, b, (((1,), (1,)), ((), ())), preferred_element_type=F32)


def _dot_tn(a, b):
    return lax.dot_general(a, b, (((0,), (0,)), ((), ())), preferred_element_type=F32)


FFN_CHUNK = 256


def _swiglu_res(x, g_ref, wg_ref, wu_ref, wd_ref):
    hb = _rms(x, g_ref[...]).astype(BF16)
    acc = x
    for c in range(D_FF // FFN_CHUNK):
        sl = slice(c * FFN_CHUNK, (c + 1) * FFN_CHUNK)
        a = _dot(hb, wg_ref[:, sl])
        u = _dot(hb, wu_ref[:, sl])
        act = (0.5 * a * jax.nn.sigmoid(a) * u).astype(BF16)
        acc = acc + _dot(act, wd_ref[sl, :])
    return acc


def _ffn_kernel(x_ref, g_ref, wg_ref, wu_ref, wd_ref, o_ref):
    o_ref[...] = _swiglu_res(x_ref[...], g_ref, wg_ref, wu_ref, wd_ref)


def _ffn_parts_kernel(*refs, npt, two_x, proj, gated):
    it = iter(refs)
    xp_ref = next(it)
    xs_ref = next(it) if two_x else xp_ref
    ap_ref = as_ref = gp_ref = gs_ref = wp_ref = None
    if proj:
        ap_ref, as_ref = next(it), next(it)
        if gated:
            gp_ref, gs_ref = next(it), next(it)
        wp_ref = next(it)
    g_ref, wg_ref, wu_ref, wd_ref = (next(it) for _ in range(4))
    o_ref = next(it)
    x1_ref = next(it)
    i = pl.program_id(0)

    def stage(x_ref, a_ref, gate_ref):
        x = x_ref[...]
        if proj:
            a = a_ref[...] if gate_ref is None else a_ref[...] * gate_ref[...]
            x = x + _dot(a.astype(BF16), wp_ref[...])
        x1_ref[...] = x

    @pl.when(i < npt)
    def _():
        stage(xp_ref, ap_ref, gp_ref)

    @pl.when(i >= npt)
    def _():
        stage(xs_ref, as_ref, gs_ref)

    o_ref[...] = _swiglu_res(x1_ref[...], g_ref, wg_ref, wu_ref, wd_ref)


def _resident(shape):
    return pl.BlockSpec(shape, lambda i: (0,) * len(shape), pipeline_mode=pl.Buffered(1))


def _ffn(x, g, wg, wu, wd, proj=None, split=False):
    two_x = isinstance(x, tuple)
    n = x[0].shape[0] + x[1].shape[0] if two_x else x.shape[0]
    wspecs = [_resident((1, D_MODEL)), _resident(wg.shape), _resident(wu.shape), _resident(wd.shape)]
    wargs = (g.reshape(1, D_MODEL), wg, wu, wd)
    cparams = pltpu.CompilerParams(dimension_semantics=("parallel",), vmem_limit_bytes=56 * 1024 * 1024)
    if not two_x and proj is None:
        tm = _pick(n, (512, 256, 128, 64, 32, 16, 8))
        xspec = pl.BlockSpec((tm, D_MODEL), lambda i: (i, 0))
        return pl.pallas_call(
            _ffn_kernel, grid=(n // tm,), in_specs=[xspec] + wspecs, out_specs=xspec,
            out_shape=jax.ShapeDtypeStruct((n, D_MODEL), F32), compiler_params=cparams, name="ffn",
        )(x, *wargs)

    if two_x:
        n_p, n_s = x[0].shape[0], x[1].shape[0]
    else:
        n_p, n_s = proj[0].shape[0], proj[1].shape[0]
    tm = _pick(n_s, (512, 256, 128, 64, 32, 16, 8))
    assert n == n_p + n_s and n_p % tm == 0
    npt = n_p // tm

    def run(off, cnt):
        row = lambda width: pl.BlockSpec((tm, width), lambda i: (i + off, 0))
        prm = lambda width: pl.BlockSpec((tm, width), lambda i: (jnp.minimum(i + off, npt - 1), 0))
        smp = lambda width: pl.BlockSpec((tm, width), lambda i: (jnp.maximum(i + off - npt, 0), 0))
        specs, args = ([prm(D_MODEL), smp(D_MODEL)], list(x)) if two_x else ([row(D_MODEL)], [x])
        gated = False
        if proj is not None:
            a_p, a_s, w, gate_p, gate_s = proj
            kin = a_p.shape[1]
            gated = gate_p is not None
            specs += [prm(kin), smp(kin)] + ([prm(kin), smp(kin)] if gated else []) + [_resident(w.shape)]
            args += [a_p, a_s] + ([gate_p, gate_s] if gated else []) + [w]
        return pl.pallas_call(
            functools.partial(_ffn_parts_kernel, npt=npt - off, two_x=two_x, proj=proj is not None, gated=gated),
            grid=(cnt,),
            in_specs=specs + wspecs,
            out_specs=pl.BlockSpec((tm, D_MODEL), lambda i: (i, 0)),
            out_shape=jax.ShapeDtypeStruct((cnt * tm, D_MODEL), F32),
            scratch_shapes=[pltpu.VMEM((tm, D_MODEL), F32)],
            compiler_params=cparams,
            name="ffn",
        )(*args, *wargs)

    if split:
        return run(0, npt), run(npt, n // tm - npt)
    return run(0, n // tm)


def _mla_proj_kernel(x_ref, g_ref, wd_ref, gq_ref, gkv_ref, wqa_ref, wqb_ref, wuk_ref, wuv_ref,
                     gqn_ref, gkn_ref, ct_ref, st_ref,
                     q_ref, c_ref, kr_ref, k_ref, v_ref):
    hb = _rms(x_ref[...], g_ref[...]).astype(BF16)
    lat = _dot(hb, wd_ref[...])
    cq = _rms(lat[:, :Q_RANK], gq_ref[...]).astype(BF16)
    c = _rms(lat[:, Q_RANK:Q_RANK + KV_RANK], gkv_ref[...])
    ct = ct_ref[...]
    st = st_ref[...]
    o = Q_RANK + KV_RANK
    kr = lat[:, o:o + 128] * ct + lat[:, o + 128:o + 256] * st
    c_ref[...] = c
    kr_ref[...] = kr[:, :ROPE_DIM]

    qa = _dot(cq, wqa_ref[...])
    qb = _dot(cq, wqb_ref[...])
    gqn = gqn_ref[...]
    gkn = gkn_ref[...]
    cb = c.astype(BF16)
    kn = _dot(cb, wuk_ref[...])
    vv = _dot(cb, wuv_ref[...])
    ssr = jnp.sum(kr * kr, axis=-1, keepdims=True)
    for h in range(MLA_HEADS):
        nope = qa[:, h * QK_PAD:h * QK_PAD + 128]
        rp = qa[:, h * QK_PAD + 128:(h + 1) * QK_PAD] * ct + qb[:, h * 128:(h + 1) * 128] * st
        ss = jnp.sum(nope * nope, axis=-1, keepdims=True) + jnp.sum(rp * rp, axis=-1, keepdims=True)
        rs = lax.rsqrt(ss * (1.0 / QK_DIM) + RMS_EPS)
        q_ref[h, :, 0:128] = (nope * rs * gqn[:, 0:128]).astype(BF16)
        q_ref[h, :, 128:256] = (rp * rs * gqn[:, 128:256]).astype(BF16)
        knh = kn[:, h * 128:(h + 1) * 128]
        rk = lax.rsqrt((jnp.sum(knh * knh, axis=-1, keepdims=True) + ssr) * (1.0 / QK_DIM) + RMS_EPS)
        k_ref[h, :, 0:128] = (knh * rk * gkn[:, 0:128]).astype(BF16)
        k_ref[h, :, 128:256] = (kr * rk * gkn[:, 128:256]).astype(BF16)
        v_ref[h] = vv[:, h * 128:(h + 1) * 128].astype(BF16)


def _rot_cols(w):
    half = ROPE_DIM // 2
    return jnp.concatenate([-w[..., half:], w[..., :half]], axis=-1)


def _pad_lanes(w, n):
    return jnp.pad(w, [(0, 0)] * (w.ndim - 1) + [(0, n - w.shape[-1])])


def _mla_weights(m, mla_w_down, mla_g_q_lat, mla_g_kv_lat, mla_w_uq, mla_w_uk, mla_w_uv, mla_g_qn, mla_g_kn):
    wd = mla_w_down[m]
    o = Q_RANK + KV_RANK
    wkr = wd[:, o:]
    wd_ext = jnp.concatenate([wd[:, :o], _pad_lanes(wkr, 128), _pad_lanes(_rot_cols(wkr), 128)], axis=1)
    wq = mla_w_uq[m].reshape(Q_RANK, MLA_HEADS, QK_DIM)
    wqa = _pad_lanes(wq, QK_PAD).reshape(Q_RANK, MLA_HEADS * QK_PAD)
    wqb = _pad_lanes(_rot_cols(wq[..., NOPE_DIM:]), 128).reshape(Q_RANK, MLA_HEADS * 128)
    return dict(
        wd=wd_ext.astype(BF16), gq=mla_g_q_lat[m].reshape(1, Q_RANK), gkv=mla_g_kv_lat[m].reshape(1, KV_RANK),
        wqa=wqa.astype(BF16), wqb=wqb.astype(BF16),
        wuk=mla_w_uk[m].reshape(KV_RANK, MLA_HEADS * NOPE_DIM).astype(BF16),
        wuv=mla_w_uv[m].reshape(KV_RANK, MLA_HEADS * V_DIM).astype(BF16),
        gqn=_pad_lanes(mla_g_qn[m] * ATTN_SCALE, QK_PAD).reshape(1, QK_PAD),
        gkn=_pad_lanes(mla_g_kn[m], QK_PAD).reshape(1, QK_PAD),
    )


def _rope_tables(pos):
    inv = ROPE_THETA ** (-jnp.arange(0, ROPE_DIM, 2, dtype=F32) / ROPE_DIM)
    ang = pos.astype(F32)[:, None] * inv[None, :]
    cos, sin = jnp.cos(ang), jnp.sin(ang)
    return (_pad_lanes(jnp.concatenate([cos, cos], axis=-1), 128),
            _pad_lanes(jnp.concatenate([sin, sin], axis=-1), 128))


def _mla_proj(x, g, w, ctab, stab, tm, tab_index):
    n = x.shape[0]
    full = lambda shape: pl.BlockSpec(shape, lambda i: (0,) * len(shape))
    tspec = pl.BlockSpec((tm, 128), lambda i: (tab_index(i), 0))
    return pl.pallas_call(
        _mla_proj_kernel,
        grid=(n // tm,),
        in_specs=[
            pl.BlockSpec((tm, D_MODEL), lambda i: (i, 0)),
            full((1, D_MODEL)), full(w["wd"].shape), full((1, Q_RANK)), full((1, KV_RANK)),
            full(w["wqa"].shape), full(w["wqb"].shape), full(w["wuk"].shape), full(w["wuv"].shape),
            full((1, QK_PAD)), full((1, QK_PAD)), tspec, tspec,
        ],
        out_specs=[
            pl.BlockSpec((MLA_HEADS, tm, QK_PAD), lambda i: (0, i, 0)),
            pl.BlockSpec((tm, KV_RANK), lambda i: (i, 0)),
            pl.BlockSpec((tm, ROPE_DIM), lambda i: (i, 0)),
            pl.BlockSpec((MLA_HEADS, tm, QK_PAD), lambda i: (0, i, 0)),
            pl.BlockSpec((MLA_HEADS, tm, V_DIM), lambda i: (0, i, 0)),
        ],
        out_shape=[
            jax.ShapeDtypeStruct((MLA_HEADS, n, QK_PAD), BF16),
            jax.ShapeDtypeStruct((n, KV_RANK), F32),
            jax.ShapeDtypeStruct((n, ROPE_DIM), F32),
            jax.ShapeDtypeStruct((MLA_HEADS, n, QK_PAD), BF16),
            jax.ShapeDtypeStruct((MLA_HEADS, n, V_DIM), BF16),
        ],
        compiler_params=_cparams(("parallel",)),
        name="mla_proj",
    )(x, g.reshape(1, D_MODEL), w["wd"], w["gq"], w["gkv"], w["wqa"], w["wqb"], w["wuk"], w["wuv"],
      w["gqn"], w["gkn"], ctab, stab)


def _prompt_attn_kernel(q_ref, k_ref, v_ref, o_ref, *, tq):
    l = q_ref.shape[1]
    nq = l // tq
    row = lax.broadcasted_iota(jnp.int32, (tq, tq), 0)
    col = lax.broadcasted_iota(jnp.int32, (tq, tq), 1)
    diag_mask = row >= col
    for qi in range(nq):
        q = q_ref[0, qi * tq:(qi + 1) * tq, :]
        m = l_sum = acc = None
        for ki in range(qi + 1):
            k = k_ref[0, ki * tq:(ki + 1) * tq, :]
            v = v_ref[0, ki * tq:(ki + 1) * tq, :]
            s = _dot_nt(q, k)
            if ki == qi:
                s = jnp.where(diag_mask, s, -jnp.inf)
            m_blk = jnp.max(s, axis=-1, keepdims=True)
            if ki == 0:
                m = m_blk
                p = jnp.exp(s - m)
                l_sum = jnp.sum(p, axis=-1, keepdims=True)
                acc = _dot(p.astype(BF16), v)
            else:
                m_new = jnp.maximum(m, m_blk)
                alpha = jnp.exp(m - m_new)
                p = jnp.exp(s - m_new)
                l_sum = alpha * l_sum + jnp.sum(p, axis=-1, keepdims=True)
                acc = alpha * acc + _dot(p.astype(BF16), v)
                m = m_new
        o_ref[qi * tq:(qi + 1) * tq, :] = (acc / l_sum).astype(o_ref.dtype)


def _prompt_attn(q, k, v, nb, l):
    tq = _pick(l, (512, 256, 128))
    return pl.pallas_call(
        functools.partial(_prompt_attn_kernel, tq=tq),
        grid=(nb, MLA_HEADS),
        in_specs=[
            pl.BlockSpec((1, l, QK_PAD), lambda b, h: (h, b, 0)),
            pl.BlockSpec((1, l, QK_PAD), lambda b, h: (h, b, 0)),
            pl.BlockSpec((1, l, V_DIM), lambda b, h: (h, b, 0)),
        ],
        out_specs=pl.BlockSpec((l, V_DIM), lambda b, h: (b, h)),
        out_shape=jax.ShapeDtypeStruct((nb * l, MLA_HEADS * V_DIM), BF16),
        compiler_params=_cparams(("parallel", "parallel")),
        name="prompt_attn",
    )(q, k, v)


def _sample_attn_kernel(pt_ref, q_ref, cn_ref, krn_ref, wukt_ref, wuk_ref, wuv_ref, gkn_ref, cc_hbm, ckr_hbm,
                        o_ref, wq_ref, cb_ref, cbuf, kbuf, sem, *, m, n_pages, tc):
    b = pl.program_id(0)
    nseq = pl.num_programs(0)
    slot = b % 2
    nq = q_ref.shape[1] // MLA_HEADS
    rows = q_ref.shape[1]
    n_nope = MLA_HEADS * NOPE_DIM

    def page_copies(page, slot_, j):
        return (pltpu.make_async_copy(cc_hbm.at[m, page], cbuf.at[slot_, j], sem.at[slot_]),
                pltpu.make_async_copy(ckr_hbm.at[m, page], kbuf.at[slot_, j], sem.at[slot_]))

    def start_pages(seq, slot_):
        for j in range(n_pages):
            for cp in page_copies(pt_ref[seq * n_pages + j], slot_, j):
                cp.start()

    def wait_pages(slot_):
        for j in range(n_pages):
            for cp in page_copies(0, slot_, j):
                cp.wait()

    @pl.when(b == 0)
    def _():
        start_pages(0, 0)
        wq_ref[0:n_nope, :] = wukt_ref[...]

    wait_pages(slot)
    start_pages(jnp.where(b + 1 == nseq, 0, b + 1), 1 - slot)

    gkn = gkn_ref[...]
    qts, qrs = [], []
    for h in range(MLA_HEADS):
        qh = q_ref[0, h * nq:(h + 1) * nq, :].astype(F32)
        qg = (qh[:, :NOPE_DIM] * gkn[:, :NOPE_DIM]).astype(BF16)
        qts.append(_dot_nt(qg, wuk_ref[:, h * NOPE_DIM:(h + 1) * NOPE_DIM]))
        qrs.append(qh[:, NOPE_DIM:QK_DIM] * gkn[:, NOPE_DIM:QK_DIM])
    wq_ref[n_nope:n_nope + rows, :] = jnp.concatenate(qts, axis=0).astype(BF16)
    qr = jnp.concatenate(qrs, axis=0).astype(BF16)

    def scores(cbc, ssr, s_rope):
        t = cbc.shape[0]
        big = _dot_nt(wq_ref[...], cbc)
        rs = []
        for h in range(MLA_HEADS):
            kh = big[h * NOPE_DIM:(h + 1) * NOPE_DIM]
            ssq = jnp.sum(kh * kh, axis=0, keepdims=True)
            rs_h = lax.rsqrt((ssq + ssr) * (1.0 / QK_DIM) + RMS_EPS)
            rs.append(jnp.broadcast_to(rs_h, (nq, t)))
        return (big[n_nope:] + s_rope) * jnp.concatenate(rs, axis=0)

    pages_per_chunk = tc // PAGE_SIZE
    for j in range(n_pages):
        cb_ref[j * PAGE_SIZE:(j + 1) * PAGE_SIZE, :] = cbuf[slot, j].astype(BF16)
    s_list, c_list = [], []
    for ci in range(n_pages // pages_per_chunk):
        cbc = cb_ref[ci * tc:(ci + 1) * tc, :]
        krt = jnp.concatenate([kbuf[slot, ci * pages_per_chunk + jj] for jj in range(pages_per_chunk)], axis=1)
        ssr = jnp.sum(krt * krt, axis=0, keepdims=True)
        s_list.append(scores(cbc, ssr, _dot(qr, krt.astype(BF16))))
        c_list.append(cbc)

    pad = PAGE_SIZE - nq
    cn = jnp.concatenate([cn_ref[...], jnp.zeros((pad, KV_RANK), F32)], axis=0).astype(BF16)
    krn = jnp.concatenate([krn_ref[...], jnp.zeros((pad, ROPE_DIM), F32)], axis=0)
    ssr = _dot_nt(jnp.ones((8, ROPE_DIM), BF16), (krn * krn).astype(BF16))[0:1]
    s = scores(cn, ssr, _dot_nt(qr, krn.astype(BF16)))
    qpos = lax.broadcasted_iota(jnp.int32, (rows, PAGE_SIZE), 0) % nq
    tok = lax.broadcasted_iota(jnp.int32, (rows, PAGE_SIZE), 1)
    s_list.append(jnp.where(tok <= qpos, s, -jnp.inf))
    c_list.append(cn)

    mx = jnp.max(s_list[0], axis=-1, keepdims=True)
    for s in s_list[1:]:
        mx = jnp.maximum(mx, jnp.max(s, axis=-1, keepdims=True))
    ps = [jnp.exp(s - mx) for s in s_list]
    den = jnp.sum(ps[0], axis=-1, keepdims=True)
    for p in ps[1:]:
        den = den + jnp.sum(p, axis=-1, keepdims=True)
    acc = _dot(ps[0].astype(BF16), c_list[0])
    for p, cbc in zip(ps[1:], c_list[1:]):
        acc = acc + _dot(p.astype(BF16), cbc)
    o_lat = (acc / den).astype(BF16)
    for h in range(MLA_HEADS):
        o_ref[:, h * V_DIM:(h + 1) * V_DIM] = _dot(o_lat[h * nq:(h + 1) * nq], wuv_ref[:, h * V_DIM:(h + 1) * V_DIM])

    @pl.when(b == nseq - 1)
    def _():
        wait_pages(1 - slot)


def _sample_attn(q_s, c_all, kr_all, row0, cache_c, cache_kr, m, page_table, wukt, wuk, wuv, gkn):
    ns, n_pages = page_table.shape
    rows = q_s.shape[1]
    nq = rows // MLA_HEADS
    assert row0 % nq == 0 and nq % 8 == 0
    tc = 2 * PAGE_SIZE
    assert n_pages % 2 == 0 and n_pages <= 64
    blk0 = row0 // nq
    full = lambda shape: pl.BlockSpec(shape, lambda b, pt: (0,) * len(shape))

    cache_kr_t = jnp.swapaxes(cache_kr, 2, 3)
    grid_spec = pltpu.PrefetchScalarGridSpec(
        num_scalar_prefetch=1,
        grid=(ns,),
        in_specs=[
            pl.BlockSpec((1, rows, QK_PAD), lambda b, pt: (b, 0, 0)),
            pl.BlockSpec((nq, KV_RANK), lambda b, pt: (blk0 + b, 0)),
            pl.BlockSpec((nq, ROPE_DIM), lambda b, pt: (blk0 + b, 0)),
            full(wukt.shape), full(wuk.shape), full(wuv.shape), full((1, QK_PAD)),
            pl.BlockSpec(memory_space=pl.ANY), pl.BlockSpec(memory_space=pl.ANY),
        ],
        out_specs=pl.BlockSpec((nq, MLA_HEADS * V_DIM), lambda b, pt: (b, 0)),
        scratch_shapes=[
            pltpu.VMEM((MLA_HEADS * NOPE_DIM + rows, KV_RANK), BF16),
            pltpu.VMEM((n_pages * PAGE_SIZE, KV_RANK), BF16),
            pltpu.VMEM((2, n_pages, PAGE_SIZE, KV_RANK), F32),
            pltpu.VMEM((2, n_pages, ROPE_DIM, PAGE_SIZE), F32),
            pltpu.SemaphoreType.DMA((2,)),
        ],
    )
    return pl.pallas_call(
        functools.partial(_sample_attn_kernel, m=m, n_pages=n_pages, tc=tc),
        grid_spec=grid_spec,
        out_shape=jax.ShapeDtypeStruct((ns * nq, MLA_HEADS * V_DIM), F32),
        compiler_params=_cparams(("arbitrary",)),
        name="sample_attn",
    )(page_table.reshape(-1), q_s, c_all, kr_all, wukt, wuk, wuv, gkn, cache_c, cache_kr_t)


DECAY_SCALE = 0.6065306597126334


def _rwkv_proj_kernel(*refs, sample, has_vres, tiles_per_seq, nq):
    it = iter(refs)
    x_ref, g_ref, mu_ref = next(it), next(it), next(it)
    shift_ref = next(it) if sample else None
    wr, wk, wv, w0, w1, w2, a0, a1, a2, g1, g2 = (next(it) for _ in range(11))
    if has_vres:
        v0, v1, v2, vf_ref = (next(it) for _ in range(4))
    r_ref, k_ref, v_ref, lw_ref, alr_ref, gate_ref, hl_ref = (next(it) for _ in range(7))
    carry_ref = next(it)
    tm = x_ref.shape[0]
    i = pl.program_id(0)

    h = _rms(x_ref[...], g_ref[...])
    rolled = pltpu.roll(h, 1, axis=0)
    row = lax.broadcasted_iota(jnp.int32, (tm, 1), 0)
    if sample:
        prev = jnp.where(row % nq == 0, shift_ref[...], rolled)
        hl_ref[...] = h
    else:
        @pl.when(i == 0)
        def _():
            carry_ref[...] = jnp.zeros(carry_ref.shape, F32)

        first = jnp.where(i % tiles_per_seq == 0, 0.0, carry_ref[0:1, :])
        prev = jnp.where(row == 0, first, rolled)
        carry_ref[0:1, :] = h[tm - 1:tm, :]
        hl_ref[...] = h[tm - 8:tm, :]
    xx = prev - h
    mu = mu_ref[...]
    mix = lambda n: (h + xx * mu[n:n + 1, :]).astype(BF16)
    xr, xw, xk, xv, xa, xg = (mix(n) for n in range(6))
    r = _dot(xr, wr[...])
    k = _dot(xk, wk[...])
    v = _dot(xv, wv[...])
    wl = w0[...] + _dot(jnp.tanh(_dot(xw, w1[...])).astype(BF16), w2[...])
    lw = (-DECAY_SCALE) * jax.nn.sigmoid(wl)
    alr = jax.nn.sigmoid(a0[...] + _dot(_dot(xa, a1[...]).astype(BF16), a2[...]))
    gate_ref[...] = _dot(jax.nn.sigmoid(_dot(xg, g1[...])).astype(BF16), g2[...])
    if has_vres:
        vgate = jax.nn.sigmoid(v0[...] + _dot(_dot(xv, v1[...]).astype(BF16), v2[...]))
    for p in range(HEAD_PAIRS):
        sl = slice(p * PAIR_W, (p + 1) * PAIR_W)
        vp = v[:, sl]
        if has_vres:
            vp = vp + (vf_ref[p] - vp) * vgate[:, sl]
        r_ref[p] = r[:, sl]
        k_ref[p] = k[:, sl]
        v_ref[p] = vp
        lw_ref[p] = lw[:, sl]
        alr_ref[p] = alr[:, sl]


def _rwkv_weights(j, rwkv_mu, rwkv_w_r, rwkv_w_k, rwkv_w_v, rwkv_w0, rwkv_w1, rwkv_w2, rwkv_a0, rwkv_a1,
                  rwkv_a2, rwkv_v0, rwkv_v1, rwkv_v2, rwkv_g1, rwkv_g2):
    row = lambda a: a.reshape(1, D_MODEL)
    w = [rwkv_w_r[j].astype(BF16), rwkv_w_k[j].astype(BF16), rwkv_w_v[j].astype(BF16),
         row(rwkv_w0[j]), rwkv_w1[j].astype(BF16), rwkv_w2[j].astype(BF16),
         row(rwkv_a0[j]), rwkv_a1[j].astype(BF16), rwkv_a2[j].astype(BF16),
         rwkv_g1[j].astype(BF16), rwkv_g2[j].astype(BF16)]
    if j > 0:
        w += [row(rwkv_v0[j - 1]), rwkv_v1[j - 1].astype(BF16), rwkv_v2[j - 1].astype(BF16)]
    return rwkv_mu[j], w


def _rwkv_proj(x, row0, n, g, mu, weights, v_first, shift_exp, seq_len):
    sample = shift_exp is not None
    has_vres = v_first is not None
    tm = _pick(n if sample else seq_len, (256, 128, 64, 32, 16, 8))
    assert row0 % tm == 0 and n % tm == 0
    off = row0 // tm
    if sample:
        assert tm % seq_len == 0
        tiles_per_seq = 1
    else:
        assert seq_len % tm == 0
        tiles_per_seq = seq_len // tm
    full = lambda a: pl.BlockSpec(a.shape, lambda i: (0,) * a.ndim)
    pair_spec = pl.BlockSpec((HEAD_PAIRS, tm, PAIR_W), lambda i: (0, i, 0))
    row_spec = pl.BlockSpec((tm, D_MODEL), lambda i: (i, 0))
    g2d = g.reshape(1, D_MODEL)
    args = [x, g2d, mu]
    specs = [pl.BlockSpec((tm, D_MODEL), lambda i: (i + off, 0)), full(g2d), full(mu)]
    if sample:
        args.append(shift_exp)
        specs.append(row_spec)
    args += weights
    specs += [full(a) for a in weights]
    if has_vres:
        args.append(v_first)
        specs.append(pair_spec)
    pair_shape = jax.ShapeDtypeStruct((HEAD_PAIRS, n, PAIR_W), F32)
    if sample:
        hl_spec, hl_shape = row_spec, jax.ShapeDtypeStruct((n, D_MODEL), F32)
    else:
        hl_spec = pl.BlockSpec((8, D_MODEL), lambda i: (i, 0))
        hl_shape = jax.ShapeDtypeStruct((n // tm * 8, D_MODEL), F32)
    return pl.pallas_call(
        functools.partial(_rwkv_proj_kernel, sample=sample, has_vres=has_vres,
                          tiles_per_seq=tiles_per_seq, nq=seq_len),
        grid=(n // tm,),
        in_specs=specs,
        out_specs=[pair_spec] * 5 + [row_spec, hl_spec],
        out_shape=[pair_shape] * 5 + [jax.ShapeDtypeStruct((n, D_MODEL), F32), hl_shape],
        scratch_shapes=[pltpu.VMEM((8, D_MODEL), F32)],
        compiler_params=_cparams(("arbitrary",)),
        name="rwkv_proj",
    )(*args)


def _wkv_masks(c, seg):
    c2 = 2 * c
    ri = lax.broadcasted_iota(jnp.int32, (c2, c2), 0)
    cj = lax.broadcasted_iota(jnp.int32, (c2, c2), 1)
    t, s = ri % c, cj % c
    same = ((ri // c) == (cj // c)) & ((t // seg) == (s // seg))
    levels = []
    step = 1
    while step < seg:
        levels.append(same & ((t ^ s) < 2 * step) & ((t & step) != 0) & ((s & step) == 0))
        step *= 2
    ti = lax.broadcasted_iota(jnp.int32, (c, c), 0)
    tj = lax.broadcasted_iota(jnp.int32, (c, c), 1)
    same_seq = (ti // seg) == (tj // seg)
    lane = lax.broadcasted_iota(jnp.int32, (1, PAIR_W), 1)
    row = lax.broadcasted_iota(jnp.int32, (c2, 1), 0)
    ei = lax.broadcasted_iota(jnp.int32, (PAIR_W, PAIR_W), 0)
    ej = lax.broadcasted_iota(jnp.int32, (PAIR_W, PAIR_W), 1)
    return dict(
        strict=same & (t > s), incl=same & (t >= s), levels=levels,
        tri=jnp.where(same_seq & (ti >= tj), 1.0, 0.0).astype(BF16),
        seq_ones=jnp.where(same_seq, 1.0, 0.0).astype(BF16),
        m0=lane < HEAD_SIZE, own=(row < c) == (lane < HEAD_SIZE), eye=ei == ej,
    )


def _wkv_local(chunks, prm, msk, seg):
    kkp, kap, rkp, lnw, lnb = prm
    n = len(chunks)
    c = chunks[0][0].shape[0]
    c2 = 2 * c
    nseg = c // seg
    m0 = msk["m0"]
    tri = msk["tri"]
    b16 = lambda a: a.astype(BF16)
    cat0 = lambda a, b: jnp.concatenate([a, b], axis=0)
    cat1 = lambda a, b: jnp.concatenate([a, b], axis=1)

    def stack(a):
        return cat0(jnp.where(m0, a, 0.0), jnp.where(m0, 0.0, a))

    def seq_rows(a, g):
        return a if nseg == 1 else cat0(a[g * seg:(g + 1) * seg], a[c + g * seg:c + (g + 1) * seg])

    cums, tots = [], []
    for (_, _, _, lw, _) in chunks:
        hi = b16(lw)
        r1 = lw - hi.astype(F32)
        mid = b16(r1)
        lo = b16(r1 - mid.astype(F32))
        cum = _dot(tri, hi) + _dot(tri, mid) + _dot(tri, lo)
        cums.append(cum)
        if nseg == 1:
            tots.append(cum[c - 1:c, :])
        else:
            ones = msk["seq_ones"]
            tots.append(_dot(ones, hi) + _dot(ones, mid) + _dot(ones, lo))

    lhs, rhs, at, rt, bh, kh, v16, vs, bonus, gcol = ([] for _ in range(10))
    for (r, k, v, lw, alr), cum, tot in zip(chunks, cums, tots):
        e_in, e_ex = jnp.exp(cum), jnp.exp(cum - lw)
        e_inv, e_end = jnp.exp(-cum), jnp.exp(tot - cum)
        kk = stack(k * kkp)
        kk = kk * (1.0 / jnp.maximum(jnp.sqrt(jnp.sum(kk * kk, axis=-1, keepdims=True)), 1e-12))
        kmod = stack(k * (1.0 + (alr - 1.0) * kap))
        r_s, v_s = stack(r), stack(v)
        bb = kk * stack(alr)
        a_t, r_t = -kk * cat0(e_ex, e_ex), r_s * cat0(e_in, e_in)
        lhs.append(b16(cat0(a_t, r_t)))
        rhs.append(b16(cat0(bb * cat0(e_inv, e_inv), kmod * cat0(e_inv, e_inv))))
        at.append(a_t)
        rt.append(r_t)
        bh.append(bb * cat0(e_end, e_end))
        kh.append(kmod * cat0(e_end, e_end))
        v16.append(b16(v_s))
        vs.append(v_s)
        bonus.append(jnp.sum(r_s * kmod * rkp, axis=-1, keepdims=True) * v_s)
        gcol.append([jnp.sum(jnp.where(msk["eye"], jnp.exp(tot[g * seg:g * seg + 1, :]), 0.0), axis=1, keepdims=True)
                     for g in range(nseg)])

    yield
    gram = [_dot_nt(a, b) for a, b in zip(lhs, rhs)]
    yield
    ab = [jnp.where(msk["strict"], g[:c2, :c2], 0.0) for g in gram]
    ak = [b16(jnp.where(msk["strict"], g[:c2, c2:], 0.0)) for g in gram]
    rb = [b16(jnp.where(msk["incl"], g[c2:, :c2], 0.0)) for g in gram]
    rk = [b16(jnp.where(msk["incl"], g[c2:, c2:], 0.0)) for g in gram]
    akv = [_dot(a, v) for a, v in zip(ak, v16)]
    rkv = [_dot(a, v) for a, v in zip(rk, v16)]
    khv = [[_dot_tn(b16(seq_rows(a, g)), b16(seq_rows(v, g))) for g in range(nseg)] for a, v in zip(kh, vs)]

    levels = msk["levels"]
    e = [jnp.where(levels[0], a, 0.0) for a in ab]
    for lvl in levels[1:]:
        yield
        low = [jnp.where(lvl, a, 0.0) for a in ab]
        x = [lo_ + _dot(b16(e_), b16(lo_)) for e_, lo_ in zip(e, low)]
        yield
        e = [e_ + x_ + _dot(b16(x_), b16(e_)) for e_, x_ in zip(e, x)]

    yield
    w2 = [cat1(a, b) for a, b in zip(akv, at)]
    u2 = [w + _dot(b16(e_), b16(w)) for w, e_ in zip(w2, e)]
    yield
    y2 = [_dot(a, b16(u)) + cat1(b, r_) for a, u, b, r_ in zip(rb, u2, rkv, rt)]
    mj = [[_dot_tn(b16(seq_rows(a, g)), b16(seq_rows(u, g))) for g in range(nseg)] for a, u in zip(bh, u2)]
    return [dict(y2=y2[i], mj=mj[i], khv=khv[i], gcol=gcol[i], bonus=bonus[i]) for i in range(n)]


def _wkv_state_step(loc, prm, msk, seg, h_in):
    lnw, lnb = prm[3], prm[4]
    c2 = loc["y2"].shape[0]
    c = c2 // 2
    nseg = c // seg
    b16 = lambda a: a.astype(BF16)
    parts, h_out = [], []
    for g in range(nseg):
        h = h_in if nseg == 1 else h_in[g]
        h16 = b16(h)
        y2g = loc["y2"]
        if nseg > 1:
            y2g = jnp.concatenate([y2g[g * seg:(g + 1) * seg], y2g[c + g * seg:c + (g + 1) * seg]], axis=0)
        mj = loc["mj"][g]
        parts.append(y2g[:, :PAIR_W] + _dot(b16(y2g[:, PAIR_W:]), h16))
        h_out.append(loc["gcol"][g] * h + _dot(b16(mj[:, PAIR_W:]), h16) + (mj[:, :PAIR_W] + loc["khv"][g]))
    y = parts[0] if nseg == 1 else jnp.concatenate([p[:seg] for p in parts] + [p[seg:] for p in parts], axis=0)

    own = msk["own"]
    inv_n = 1.0 / HEAD_SIZE
    mean = jnp.sum(y, axis=-1, keepdims=True) * inv_n
    d = jnp.where(own, y - mean, 0.0)
    var = jnp.sum(d * d, axis=-1, keepdims=True) * inv_n
    z_st = d * lax.rsqrt(var + LNX_EPS) * lnw + jnp.where(own, lnb, 0.0) + loc["bonus"]
    return z_st[:c] + z_st[c:], (h_out[0] if nseg == 1 else h_out)


def _finish(gen):
    try:
        while True:
            next(gen)
    except StopIteration as stop:
        return stop.value


def _wkv_block(chunks, prm, msk, h0, seg):
    n = len(chunks)
    c = chunks[0][0].shape[0]
    zs, hs = [], []
    if seg < c:
        locs = _finish(_wkv_local(chunks, prm, msk, seg))
        for i in range(n):
            z, h_seq = _wkv_state_step(locs[i], prm, msk, seg, h0[i])
            zs.append(z)
            hs += h_seq
        return zs, hs
    h = h0
    prev = []
    for start in range(0, n, WKV_GROUP):
        local = _wkv_local(chunks[start:start + WKV_GROUP], prm, msk, seg)
        for loc in prev:
            next(local)
            z, h = _wkv_state_step(loc, prm, msk, seg, h)
            zs.append(z)
            hs.append(h)
        prev = _finish(local)
    for loc in prev:
        z, h = _wkv_state_step(loc, prm, msk, seg, h)
        zs.append(z)
        hs.append(h)
    return zs, hs


def _state_in(s_ref, i):
    z = jnp.zeros((HEAD_SIZE, HEAD_SIZE), F32)
    top = jnp.concatenate([s_ref[i, 0], z], axis=1)
    bot = jnp.concatenate([z, s_ref[i, 1]], axis=1)
    return jnp.concatenate([top, bot], axis=0).T


def _state_out(s_ref, i, hbd):
    ht = hbd.T
    s_ref[i, 0] = ht[:HEAD_SIZE, :HEAD_SIZE]
    s_ref[i, 1] = ht[HEAD_SIZE:, HEAD_SIZE:]


def _wkv_kernel(*refs, c, seg, nb, blocks_per_seq, per_seq_state):
    data = refs[:5]
    prm = tuple(ref[0] for ref in refs[5:10])
    msk = _wkv_masks(c, seg)
    chunks = [tuple(ref[0, ci * c:(ci + 1) * c, :] for ref in data) for ci in range(nb)]
    if per_seq_state:
        s0_ref, z_ref, sout_ref = refs[10:]
        nseg = c // seg
        h0 = [[_state_in(s0_ref, ci * nseg + g) for g in range(nseg)] for ci in range(nb)]
        zs, hs = _wkv_block(chunks, prm, msk, h0, seg)
        for si in range(nb * nseg):
            _state_out(sout_ref, si, hs[si])
    else:
        z_ref, sout_ref, h_ref = refs[10:]
        b = pl.program_id(1)

        @pl.when(b % blocks_per_seq == 0)
        def _():
            h_ref[...] = jnp.zeros(h_ref.shape, F32)

        zs, hs = _wkv_block(chunks, prm, msk, h_ref[...], c)
        h_ref[...] = hs[-1]

        @pl.when(b % blocks_per_seq == blocks_per_seq - 1)
        def _():
            _state_out(sout_ref, 0, hs[-1])
    for ci in range(nb):
        z_ref[ci * c:(ci + 1) * c, :] = zs[ci]


def _pair_rows(a):
    return a.reshape(HEAD_PAIRS, 1, PAIR_W)


def _wkv(r, k, v, lw, alr, params, s0, n_seq, seq_len):
    c = WKV_CHUNK
    per_seq_state = s0 is not None
    n_chunks = n_seq * seq_len // c
    if per_seq_state:
        assert c % seq_len == 0 and (n_seq * seq_len) % c == 0
        seg, nb, blocks_per_seq = seq_len, _pick(n_chunks, (8, 4, 2, 1)), 1
        seqs = nb * (c // seg)
        sspec = pl.BlockSpec((seqs, 2, HEAD_SIZE, HEAD_SIZE), lambda p, b: (b, p, 0, 0))
        extra_in, extra_args, scratch = [sspec], [s0], []
    else:
        assert seq_len % c == 0
        seg, nb = c, _pick(seq_len // c, (32, 16, 8, 4, 2, 1))
        blocks_per_seq = seq_len // c // nb
        sspec = pl.BlockSpec((1, 2, HEAD_SIZE, HEAD_SIZE), lambda p, b: (b // blocks_per_seq, p, 0, 0))
        extra_in, extra_args, scratch = [], [], [pltpu.VMEM((PAIR_W, PAIR_W), F32)]
    n_blocks = n_chunks // nb
    dspec = pl.BlockSpec((1, nb * c, PAIR_W), lambda p, b: (p, b, 0))
    pspec = pl.BlockSpec((1, 1, PAIR_W), lambda p, b: (p, 0, 0))
    return pl.pallas_call(
        functools.partial(_wkv_kernel, c=c, seg=seg, nb=nb, blocks_per_seq=blocks_per_seq,
                          per_seq_state=per_seq_state),
        grid=(HEAD_PAIRS, n_blocks),
        in_specs=[dspec] * 5 + [pspec] * 5 + extra_in,
        out_specs=[pl.BlockSpec((nb * c, PAIR_W), lambda p, b: (b, p)), sspec],
        out_shape=[jax.ShapeDtypeStruct((n_seq * seq_len, D_MODEL), F32),
                   jax.ShapeDtypeStruct((n_seq, RWKV_HEADS, HEAD_SIZE, HEAD_SIZE), F32)],
        scratch_shapes=scratch,
        compiler_params=_cparams(("parallel", "arbitrary")),
        name="wkv",
    )(r, k, v, lw, alr, *[_pair_rows(p) for p in params], *extra_args)


def kernel(x_prompt, x_sample, cache_kv_latent, cache_k_rope, state_wkv, state_shift, page_table,
           ffn_norm, ffn_w_gate, ffn_w_up, ffn_w_down, mix_norm,
           mla_w_down, mla_g_q_lat, mla_g_kv_lat, mla_w_uq, mla_w_uk, mla_w_uv, mla_g_qn, mla_g_kn, mla_w_o,
           rwkv_mu, rwkv_w_r, rwkv_w_k, rwkv_w_v, rwkv_w_o, rwkv_w0, rwkv_w1, rwkv_w2,
           rwkv_a0, rwkv_a1, rwkv_a2, rwkv_v0, rwkv_v1, rwkv_v2, rwkv_g1, rwkv_g2,
           rwkv_k_k, rwkv_k_a, rwkv_r_k, rwkv_lnx_w, rwkv_lnx_b):
    nb, l, _ = x_prompt.shape
    ns, nq, _ = x_sample.shape
    n_p, n_s = nb * l, ns * nq
    depth = ffn_norm.shape[0]
    past = page_table.shape[1] * PAGE_SIZE
    x = (x_prompt.reshape(n_p, D_MODEL), x_sample.reshape(n_s, D_MODEL))

    tm = _pick(n_s, (512, 256, 128, 64, 32, 16, 8))
    assert l % tm == 0 and tm % nq == 0
    cos_p, sin_p = _rope_tables(jnp.arange(l))
    cos_s, sin_s = _rope_tables(past + jnp.arange(nq))
    ctab = jnp.concatenate([cos_p, jnp.tile(cos_s, (tm // nq, 1))], axis=0)
    stab = jnp.concatenate([sin_p, jnp.tile(sin_s, (tm // nq, 1))], axis=0)
    n_ptiles, tiles_per_seq = n_p // tm, l // tm
    tab_index = lambda i: jnp.where(i < n_ptiles, i % tiles_per_seq, tiles_per_seq)

    lat_rows, rope_rows, p_wkv, p_shift, s_wkv, s_shift = [], [], [], [], [], []
    vf_p = vf_s = None
    for i in range(depth):
        x = _ffn(x, ffn_norm[i, 0], ffn_w_gate[i, 0].astype(BF16), ffn_w_up[i, 0].astype(BF16),
                 ffn_w_down[i, 0].astype(BF16))
        if i % 2 == 0:
            m = i // 2
            w = _mla_weights(m, mla_w_down, mla_g_q_lat, mla_g_kv_lat, mla_w_uq, mla_w_uk, mla_w_uv,
                             mla_g_qn, mla_g_kn)
            q, c, kr, k, v = _mla_proj(x, mix_norm[i], w, ctab, stab, tm, tab_index)
            o_p = _prompt_attn(q, k, v, nb, l)
            q_s = jnp.transpose(q[:, n_p:, :].reshape(MLA_HEADS, ns, nq, QK_PAD), (1, 0, 2, 3))
            q_s = q_s.reshape(ns, MLA_HEADS * nq, QK_PAD)
            o_s = _sample_attn(q_s, c, kr, n_p, cache_kv_latent, cache_k_rope, m, page_table,
                               jnp.transpose(w["wuk"]), w["wuk"], w["wuv"], w["gkn"])
            proj = (o_p, o_s, mla_w_o[m].astype(BF16), None, None)
            lat_rows.append(c)
            rope_rows.append(kr)
        else:
            j = i // 2
            mu, w = _rwkv_weights(j, rwkv_mu, rwkv_w_r, rwkv_w_k, rwkv_w_v, rwkv_w0, rwkv_w1, rwkv_w2,
                                  rwkv_a0, rwkv_a1, rwkv_a2, rwkv_v0, rwkv_v1, rwkv_v2, rwkv_g1, rwkv_g2)
            params = [rwkv_k_k[j], rwkv_k_a[j], rwkv_r_k[j], rwkv_lnx_w[j], rwkv_lnx_b[j]]
            r, k, v, lw, alr, gate_p, hl = _rwkv_proj(x, 0, n_p, mix_norm[i], mu, w, vf_p, None, l)
            if j == 0:
                vf_p = v
            z_p, st_p = _wkv(r, k, v, lw, alr, params, None, nb, l)
            p_wkv.append(st_p)
            p_shift.append(hl.reshape(nb, -1, 8, D_MODEL)[:, -1, 7])
            shift_exp = jnp.zeros((ns, nq, D_MODEL), F32).at[:, 0, :].set(state_shift[j]).reshape(n_s, D_MODEL)
            r, k, v, lw, alr, gate_s, h_s = _rwkv_proj(x, n_p, n_s, mix_norm[i], mu, w, vf_s, shift_exp, nq)
            if j == 0:
                vf_s = v
            z_s, st_s = _wkv(r, k, v, lw, alr, params, state_wkv[j], ns, nq)
            s_wkv.append(st_s)
            s_shift.append(h_s.reshape(ns, nq, D_MODEL)[:, -1])
            proj = (z_p, z_s, rwkv_w_o[j].astype(BF16), gate_p, gate_s)
        x = _ffn(x, ffn_norm[i, 1], ffn_w_gate[i, 1].astype(BF16), ffn_w_up[i, 1].astype(BF16),
                 ffn_w_down[i, 1].astype(BF16), proj, split=i == depth - 1)

    lat = jnp.stack(lat_rows)
    rope = jnp.stack(rope_rows)
    return (x[0].reshape(nb, l, D_MODEL), x[1].reshape(ns, nq, D_MODEL),
            lat[:, :n_p].reshape(-1, nb, l, KV_RANK), rope[:, :n_p].reshape(-1, nb, l, ROPE_DIM),
            jnp.stack(p_wkv), jnp.stack(p_shift),
            lat[:, n_p:].reshape(-1, ns, nq, KV_RANK), rope[:, n_p:].reshape(-1, ns, nq, ROPE_DIM),
            jnp.stack(s_wkv), jnp.stack(s_shift))
```

```python
import functools

import jax
import jax.numpy as jnp
from jax import lax
from jax.experimental import pallas as pl
from jax.experimental.pallas import tpu as pltpu

F32 = jnp.float32
BF16 = jnp.bfloat16

D_MODEL = 1024
D_FF = 2816
RMS_EPS = 1e-6
MLA_HEADS = 8
Q_RANK = 384
KV_RANK = 256
NOPE_DIM = 128
ROPE_DIM = 64
QK_DIM = NOPE_DIM + ROPE_DIM
QK_PAD = 256
V_DIM = 128
ROPE_THETA = 10000.0
ATTN_SCALE = QK_DIM ** -0.5
PAGE_SIZE = 128
HEAD_SIZE = 64
RWKV_HEADS = D_MODEL // HEAD_SIZE
HEAD_PAIRS = RWKV_HEADS // 2
PAIR_W = 2 * HEAD_SIZE
LNX_EPS = 64e-5
WKV_CHUNK = 64
WKV_GROUP = 8

VMEM_LIMIT = 48 * 1024 * 1024


def _cparams(sem):
    return pltpu.CompilerParams(dimension_semantics=sem, vmem_limit_bytes=VMEM_LIMIT)


def _pick(n, cands):
    for c in cands:
        if n % c == 0:
            return c
    raise ValueError(f"no tile in {cands} divides {n}")


def _rms(x, g):
    return x * lax.rsqrt(jnp.mean(x * x, axis=-1, keepdims=True) + RMS_EPS) * g


def _dot(a, b):
    return jnp.dot(a, b, preferred_element_type=F32)


def _dot_nt(a, b):
    return lax.dot_general(a, b, (((1,), (1,)), ((), ())), preferred_element_type=F32)


def _dot_tn(a, b):
    return lax.dot_general(a, b, (((0,), (0,)), ((), ())), preferred_element_type=F32)


FFN_CHUNK = 256


def _swiglu_res(x, g_ref, wg_ref, wu_ref, wd_ref):
    hb = _rms(x, g_ref[...]).astype(BF16)
    acc = x
    for c in range(D_FF // FFN_CHUNK):
        sl = slice(c * FFN_CHUNK, (c + 1) * FFN_CHUNK)
        a = _dot(hb, wg_ref[:, sl])
        u = _dot(hb, wu_ref[:, sl])
        act = (0.5 * a * jax.nn.sigmoid(a) * u).astype(BF16)
        acc = acc + _dot(act, wd_ref[sl, :])
    return acc


def _ffn_kernel(x_ref, g_ref, wg_ref, wu_ref, wd_ref, o_ref):
    o_ref[...] = _swiglu_res(x_ref[...], g_ref, wg_ref, wu_ref, wd_ref)


def _ffn_parts_kernel(*refs, npt, two_x, proj, gated):
    it = iter(refs)
    xp_ref = next(it)
    xs_ref = next(it) if two_x else xp_ref
    ap_ref = as_ref = gp_ref = gs_ref = wp_ref = None
    if proj:
        ap_ref, as_ref = next(it), next(it)
        if gated:
            gp_ref, gs_ref = next(it), next(it)
        wp_ref = next(it)
    g_ref, wg_ref, wu_ref, wd_ref = (next(it) for _ in range(4))
    o_ref = next(it)
    x1_ref = next(it)
    i = pl.program_id(0)

    def stage(x_ref, a_ref, gate_ref):
        x = x_ref[...]
        if proj:
            a = a_ref[...] if gate_ref is None else a_ref[...] * gate_ref[...]
            x = x + _dot(a.astype(BF16), wp_ref[...])
        x1_ref[...] = x

    @pl.when(i < npt)
    def _():
        stage(xp_ref, ap_ref, gp_ref)

    @pl.when(i >= npt)
    def _():
        stage(xs_ref, as_ref, gs_ref)

    o_ref[...] = _swiglu_res(x1_ref[...], g_ref, wg_ref, wu_ref, wd_ref)


def _resident(shape):
    return pl.BlockSpec(shape, lambda i: (0,) * len(shape), pipeline_mode=pl.Buffered(1))


def _ffn(x, g, wg, wu, wd, proj=None, split=False):
    two_x = isinstance(x, tuple)
    n = x[0].shape[0] + x[1].shape[0] if two_x else x.shape[0]
    wspecs = [_resident((1, D_MODEL)), _resident(wg.shape), _resident(wu.shape), _resident(wd.shape)]
    wargs = (g.reshape(1, D_MODEL), wg, wu, wd)
    cparams = pltpu.CompilerParams(dimension_semantics=("parallel",), vmem_limit_bytes=56 * 1024 * 1024)
    if not two_x and proj is None:
        tm = _pick(n, (512, 256, 128, 64, 32, 16, 8))
        xspec = pl.BlockSpec((tm, D_MODEL), lambda i: (i, 0))
        return pl.pallas_call(
            _ffn_kernel, grid=(n // tm,), in_specs=[xspec] + wspecs, out_specs=xspec,
            out_shape=jax.ShapeDtypeStruct((n, D_MODEL), F32), compiler_params=cparams, name="ffn",
        )(x, *wargs)

    if two_x:
        n_p, n_s = x[0].shape[0], x[1].shape[0]
    else:
        n_p, n_s = proj[0].shape[0], proj[1].shape[0]
    tm = _pick(n_s, (512, 256, 128, 64, 32, 16, 8))
    assert n == n_p + n_s and n_p % tm == 0
    npt = n_p // tm

    def run(off, cnt):
        row = lambda width: pl.BlockSpec((tm, width), lambda i: (i + off, 0))
        prm = lambda width: pl.BlockSpec((tm, width), lambda i: (jnp.minimum(i + off, npt - 1), 0))
        smp = lambda width: pl.BlockSpec((tm, width), lambda i: (jnp.maximum(i + off - npt, 0), 0))
        specs, args = ([prm(D_MODEL), smp(D_MODEL)], list(x)) if two_x else ([row(D_MODEL)], [x])
        gated = False
        if proj is not None:
            a_p, a_s, w, gate_p, gate_s = proj
            kin = a_p.shape[1]
            gated = gate_p is not None
            specs += [prm(kin), smp(kin)] + ([prm(kin), smp(kin)] if gated else []) + [_resident(w.shape)]
            args += [a_p, a_s] + ([gate_p, gate_s] if gated else []) + [w]
        return pl.pallas_call(
            functools.partial(_ffn_parts_kernel, npt=npt - off, two_x=two_x, proj=proj is not None, gated=gated),
            grid=(cnt,),
            in_specs=specs + wspecs,
            out_specs=pl.BlockSpec((tm, D_MODEL), lambda i: (i, 0)),
            out_shape=jax.ShapeDtypeStruct((cnt * tm, D_MODEL), F32),
            scratch_shapes=[pltpu.VMEM((tm, D_MODEL), F32)],
            compiler_params=cparams,
            name="ffn",
        )(*args, *wargs)

    if split:
        return run(0, npt), run(npt, n // tm - npt)
    return run(0, n // tm)


def _mla_proj_kernel(x_ref, g_ref, wd_ref, gq_ref, gkv_ref, wqa_ref, wqb_ref, wuk_ref, wuv_ref,
                     gqn_ref, gkn_ref, ct_ref, st_ref,
                     q_ref, c_ref, kr_ref, k_ref, v_ref):
    hb = _rms(x_ref[...], g_ref[...]).astype(BF16)
    lat = _dot(hb, wd_ref[...])
    cq = _rms(lat[:, :Q_RANK], gq_ref[...]).astype(BF16)
    c = _rms(lat[:, Q_RANK:Q_RANK + KV_RANK], gkv_ref[...])
    ct = ct_ref[...]
    st = st_ref[...]
    o = Q_RANK + KV_RANK
    kr = lat[:, o:o + 128] * ct + lat[:, o + 128:o + 256] * st
    c_ref[...] = c
    kr_ref[...] = kr[:, :ROPE_DIM]

    qa = _dot(cq, wqa_ref[...])
    qb = _dot(cq, wqb_ref[...])
    gqn = gqn_ref[...]
    gkn = gkn_ref[...]
    cb = c.astype(BF16)
    kn = _dot(cb, wuk_ref[...])
    vv = _dot(cb, wuv_ref[...])
    ssr = jnp.sum(kr * kr, axis=-1, keepdims=True)
    for h in range(MLA_HEADS):
        nope = qa[:, h * QK_PAD:h * QK_PAD + 128]
        rp = qa[:, h * QK_PAD + 128:(h + 1) * QK_PAD] * ct + qb[:, h * 128:(h + 1) * 128] * st
        ss = jnp.sum(nope * nope, axis=-1, keepdims=True) + jnp.sum(rp * rp, axis=-1, keepdims=True)
        rs = lax.rsqrt(ss * (1.0 / QK_DIM) + RMS_EPS)
        q_ref[h, :, 0:128] = (nope * rs * gqn[:, 0:128]).astype(BF16)
        q_ref[h, :, 128:256] = (rp * rs * gqn[:, 128:256]).astype(BF16)
        knh = kn[:, h * 128:(h + 1) * 128]
        rk = lax.rsqrt((jnp.sum(knh * knh, axis=-1, keepdims=True) + ssr) * (1.0 / QK_DIM) + RMS_EPS)
        k_ref[h, :, 0:128] = (knh * rk * gkn[:, 0:128]).astype(BF16)
        k_ref[h, :, 128:256] = (kr * rk * gkn[:, 128:256]).astype(BF16)
        v_ref[h] = vv[:, h * 128:(h + 1) * 128].astype(BF16)


def _rot_cols(w):
    half = ROPE_DIM // 2
    return jnp.concatenate([-w[..., half:], w[..., :half]], axis=-1)


def _pad_lanes(w, n):
    return jnp.pad(w, [(0, 0)] * (w.ndim - 1) + [(0, n - w.shape[-1])])


def _mla_weights(m, mla_w_down, mla_g_q_lat, mla_g_kv_lat, mla_w_uq, mla_w_uk, mla_w_uv, mla_g_qn, mla_g_kn):
    wd = mla_w_down[m]
    o = Q_RANK + KV_RANK
    wkr = wd[:, o:]
    wd_ext = jnp.concatenate([wd[:, :o], _pad_lanes(wkr, 128), _pad_lanes(_rot_cols(wkr), 128)], axis=1)
    wq = mla_w_uq[m].reshape(Q_RANK, MLA_HEADS, QK_DIM)
    wqa = _pad_lanes(wq, QK_PAD).reshape(Q_RANK, MLA_HEADS * QK_PAD)
    wqb = _pad_lanes(_rot_cols(wq[..., NOPE_DIM:]), 128).reshape(Q_RANK, MLA_HEADS * 128)
    return dict(
        wd=wd_ext.astype(BF16), gq=mla_g_q_lat[m].reshape(1, Q_RANK), gkv=mla_g_kv_lat[m].reshape(1, KV_RANK),
        wqa=wqa.astype(BF16), wqb=wqb.astype(BF16),
        wuk=mla_w_uk[m].reshape(KV_RANK, MLA_HEADS * NOPE_DIM).astype(BF16),
        wuv=mla_w_uv[m].reshape(KV_RANK, MLA_HEADS * V_DIM).astype(BF16),
        gqn=_pad_lanes(mla_g_qn[m] * ATTN_SCALE, QK_PAD).reshape(1, QK_PAD),
        gkn=_pad_lanes(mla_g_kn[m], QK_PAD).reshape(1, QK_PAD),
    )


def _rope_tables(pos):
    inv = ROPE_THETA ** (-jnp.arange(0, ROPE_DIM, 2, dtype=F32) / ROPE_DIM)
    ang = pos.astype(F32)[:, None] * inv[None, :]
    cos, sin = jnp.cos(ang), jnp.sin(ang)
    return (_pad_lanes(jnp.concatenate([cos, cos], axis=-1), 128),
            _pad_lanes(jnp.concatenate([sin, sin], axis=-1), 128))


def _mla_proj(x, g, w, ctab, stab, tm, tab_index):
    n = x.shape[0]
    full = lambda shape: pl.BlockSpec(shape, lambda i: (0,) * len(shape))
    tspec = pl.BlockSpec((tm, 128), lambda i: (tab_index(i), 0))
    return pl.pallas_call(
        _mla_proj_kernel,
        grid=(n // tm,),
        in_specs=[
            pl.BlockSpec((tm, D_MODEL), lambda i: (i, 0)),
            full((1, D_MODEL)), full(w["wd"].shape), full((1, Q_RANK)), full((1, KV_RANK)),
            full(w["wqa"].shape), full(w["wqb"].shape), full(w["wuk"].shape), full(w["wuv"].shape),
            full((1, QK_PAD)), full((1, QK_PAD)), tspec, tspec,
        ],
        out_specs=[
            pl.BlockSpec((MLA_HEADS, tm, QK_PAD), lambda i: (0, i, 0)),
            pl.BlockSpec((tm, KV_RANK), lambda i: (i, 0)),
            pl.BlockSpec((tm, ROPE_DIM), lambda i: (i, 0)),
            pl.BlockSpec((MLA_HEADS, tm, QK_PAD), lambda i: (0, i, 0)),
            pl.BlockSpec((MLA_HEADS, tm, V_DIM), lambda i: (0, i, 0)),
        ],
        out_shape=[
            jax.ShapeDtypeStruct((MLA_HEADS, n, QK_PAD), BF16),
            jax.ShapeDtypeStruct((n, KV_RANK), F32),
            jax.ShapeDtypeStruct((n, ROPE_DIM), F32),
            jax.ShapeDtypeStruct((MLA_HEADS, n, QK_PAD), BF16),
            jax.ShapeDtypeStruct((MLA_HEADS, n, V_DIM), BF16),
        ],
        compiler_params=_cparams(("parallel",)),
        name="mla_proj",
    )(x, g.reshape(1, D_MODEL), w["wd"], w["gq"], w["gkv"], w["wqa"], w["wqb"], w["wuk"], w["wuv"],
      w["gqn"], w["gkn"], ctab, stab)


def _prompt_attn_kernel(q_ref, k_ref, v_ref, o_ref, *, tq):
    l = q_ref.shape[1]
    nq = l // tq
    row = lax.broadcasted_iota(jnp.int32, (tq, tq), 0)
    col = lax.broadcasted_iota(jnp.int32, (tq, tq), 1)
    diag_mask = row >= col
    for qi in range(nq):
        q = q_ref[0, qi * tq:(qi + 1) * tq, :]
        m = l_sum = acc = None
        for ki in range(qi + 1):
            k = k_ref[0, ki * tq:(ki + 1) * tq, :]
            v = v_ref[0, ki * tq:(ki + 1) * tq, :]
            s = _dot_nt(q, k)
            if ki == qi:
                s = jnp.where(diag_mask, s, -jnp.inf)
            m_blk = jnp.max(s, axis=-1, keepdims=True)
            if ki == 0:
                m = m_blk
                p = jnp.exp(s - m)
                l_sum = jnp.sum(p, axis=-1, keepdims=True)
                acc = _dot(p.astype(BF16), v)
            else:
                m_new = jnp.maximum(m, m_blk)
                alpha = jnp.exp(m - m_new)
                p = jnp.exp(s - m_new)
                l_sum = alpha * l_sum + jnp.sum(p, axis=-1, keepdims=True)
                acc = alpha * acc + _dot(p.astype(BF16), v)
                m = m_new
        o_ref[qi * tq:(qi + 1) * tq, :] = (acc / l_sum).astype(o_ref.dtype)


def _prompt_attn(q, k, v, nb, l):
    tq = _pick(l, (512, 256, 128))
    return pl.pallas_call(
        functools.partial(_prompt_attn_kernel, tq=tq),
        grid=(nb, MLA_HEADS),
        in_specs=[
            pl.BlockSpec((1, l, QK_PAD), lambda b, h: (h, b, 0)),
            pl.BlockSpec((1, l, QK_PAD), lambda b, h: (h, b, 0)),
            pl.BlockSpec((1, l, V_DIM), lambda b, h: (h, b, 0)),
        ],
        out_specs=pl.BlockSpec((l, V_DIM), lambda b, h: (b, h)),
        out_shape=jax.ShapeDtypeStruct((nb * l, MLA_HEADS * V_DIM), BF16),
        compiler_params=_cparams(("parallel", "parallel")),
        name="prompt_attn",
    )(q, k, v)


def _sample_attn_kernel(pt_ref, q_ref, cn_ref, krn_ref, wukt_ref, wuk_ref, wuv_ref, gkn_ref, cc_hbm, ckr_hbm,
                        o_ref, wq_ref, cb_ref, cbuf, kbuf, sem, *, m, n_pages, tc):
    b = pl.program_id(0)
    nseq = pl.num_programs(0)
    slot = b % 2
    nq = q_ref.shape[1] // MLA_HEADS
    rows = q_ref.shape[1]
    n_nope = MLA_HEADS * NOPE_DIM

    def page_copies(page, slot_, j):
        return (pltpu.make_async_copy(cc_hbm.at[m, page], cbuf.at[slot_, j], sem.at[slot_]),
                pltpu.make_async_copy(ckr_hbm.at[m, page], kbuf.at[slot_, j], sem.at[slot_]))

    def start_pages(seq, slot_):
        for j in range(n_pages):
            for cp in page_copies(pt_ref[seq * n_pages + j], slot_, j):
                cp.start()

    def wait_pages(slot_):
        for j in range(n_pages):
            for cp in page_copies(0, slot_, j):
                cp.wait()

    @pl.when(b == 0)
    def _():
        start_pages(0, 0)
        wq_ref[0:n_nope, :] = wukt_ref[...]

    wait_pages(slot)
    start_pages(jnp.where(b + 1 == nseq, 0, b + 1), 1 - slot)

    gkn = gkn_ref[...]
    qts, qrs = [], []
    for h in range(MLA_HEADS):
        qh = q_ref[0, h * nq:(h + 1) * nq, :].astype(F32)
        qg = (qh[:, :NOPE_DIM] * gkn[:, :NOPE_DIM]).astype(BF16)
        qts.append(_dot_nt(qg, wuk_ref[:, h * NOPE_DIM:(h + 1) * NOPE_DIM]))
        qrs.append(qh[:, NOPE_DIM:QK_DIM] * gkn[:, NOPE_DIM:QK_DIM])
    wq_ref[n_nope:n_nope + rows, :] = jnp.concatenate(qts, axis=0).astype(BF16)
    qr = jnp.concatenate(qrs, axis=0).astype(BF16)

    def scores(cbc, ssr, s_rope):
        t = cbc.shape[0]
        big = _dot_nt(wq_ref[...], cbc)
        rs = []
        for h in range(MLA_HEADS):
            kh = big[h * NOPE_DIM:(h + 1) * NOPE_DIM]
            ssq = jnp.sum(kh * kh, axis=0, keepdims=True)
            rs_h = lax.rsqrt((ssq + ssr) * (1.0 / QK_DIM) + RMS_EPS)
            rs.append(jnp.broadcast_to(rs_h, (nq, t)))
        return (big[n_nope:] + s_rope) * jnp.concatenate(rs, axis=0)

    pages_per_chunk = tc // PAGE_SIZE
    for j in range(n_pages):
        cb_ref[j * PAGE_SIZE:(j + 1) * PAGE_SIZE, :] = cbuf[slot, j].astype(BF16)
    s_list, c_list = [], []
    for ci in range(n_pages // pages_per_chunk):
        cbc = cb_ref[ci * tc:(ci + 1) * tc, :]
        krt = jnp.concatenate([kbuf[slot, ci * pages_per_chunk + jj] for jj in range(pages_per_chunk)], axis=1)
        ssr = jnp.sum(krt * krt, axis=0, keepdims=True)
        s_list.append(scores(cbc, ssr, _dot(qr, krt.astype(BF16))))
        c_list.append(cbc)

    pad = PAGE_SIZE - nq
    cn = jnp.concatenate([cn_ref[...], jnp.zeros((pad, KV_RANK), F32)], axis=0).astype(BF16)
    krn = jnp.concatenate([krn_ref[...], jnp.zeros((pad, ROPE_DIM), F32)], axis=0)
    ssr = _dot_nt(jnp.ones((8, ROPE_DIM), BF16), (krn * krn).astype(BF16))[0:1]
    s = scores(cn, ssr, _dot_nt(qr, krn.astype(BF16)))
    qpos = lax.broadcasted_iota(jnp.int32, (rows, PAGE_SIZE), 0) % nq
    tok = lax.broadcasted_iota(jnp.int32, (rows, PAGE_SIZE), 1)
    s_list.append(jnp.where(tok <= qpos, s, -jnp.inf))
    c_list.append(cn)

    mx = jnp.max(s_list[0], axis=-1, keepdims=True)
    for s in s_list[1:]:
        mx = jnp.maximum(mx, jnp.max(s, axis=-1, keepdims=True))
    ps = [jnp.exp(s - mx) for s in s_list]
    den = jnp.sum(ps[0], axis=-1, keepdims=True)
    for p in ps[1:]:
        den = den + jnp.sum(p, axis=-1, keepdims=True)
    acc = _dot(ps[0].astype(BF16), c_list[0])
    for p, cbc in zip(ps[1:], c_list[1:]):
        acc = acc + _dot(p.astype(BF16), cbc)
    o_lat = (acc / den).astype(BF16)
    for h in range(MLA_HEADS):
        o_ref[:, h * V_DIM:(h + 1) * V_DIM] = _dot(o_lat[h * nq:(h + 1) * nq], wuv_ref[:, h * V_DIM:(h + 1) * V_DIM])

    @pl.when(b == nseq - 1)
    def _():
        wait_pages(1 - slot)


def _sample_attn(q_s, c_all, kr_all, row0, cache_c, cache_kr, m, page_table, wukt, wuk, wuv, gkn):
    ns, n_pages = page_table.shape
    rows = q_s.shape[1]
    nq = rows // MLA_HEADS
    assert row0 % nq == 0 and nq % 8 == 0
    tc = 2 * PAGE_SIZE
    assert n_pages % 2 == 0 and n_pages <= 64
    blk0 = row0 // nq
    full = lambda shape: pl.BlockSpec(shape, lambda b, pt: (0,) * len(shape))

    cache_kr_t = jnp.swapaxes(cache_kr, 2, 3)
    grid_spec = pltpu.PrefetchScalarGridSpec(
        num_scalar_prefetch=1,
        grid=(ns,),
        in_specs=[
            pl.BlockSpec((1, rows, QK_PAD), lambda b, pt: (b, 0, 0)),
            pl.BlockSpec((nq, KV_RANK), lambda b, pt: (blk0 + b, 0)),
            pl.BlockSpec((nq, ROPE_DIM), lambda b, pt: (blk0 + b, 0)),
            full(wukt.shape), full(wuk.shape), full(wuv.shape), full((1, QK_PAD)),
            pl.BlockSpec(memory_space=pl.ANY), pl.BlockSpec(memory_space=pl.ANY),
        ],
        out_specs=pl.BlockSpec((nq, MLA_HEADS * V_DIM), lambda b, pt: (b, 0)),
        scratch_shapes=[
            pltpu.VMEM((MLA_HEADS * NOPE_DIM + rows, KV_RANK), BF16),
            pltpu.VMEM((n_pages * PAGE_SIZE, KV_RANK), BF16),
            pltpu.VMEM((2, n_pages, PAGE_SIZE, KV_RANK), F32),
            pltpu.VMEM((2, n_pages, ROPE_DIM, PAGE_SIZE), F32),
            pltpu.SemaphoreType.DMA((2,)),
        ],
    )
    return pl.pallas_call(
        functools.partial(_sample_attn_kernel, m=m, n_pages=n_pages, tc=tc),
        grid_spec=grid_spec,
        out_shape=jax.ShapeDtypeStruct((ns * nq, MLA_HEADS * V_DIM), F32),
        compiler_params=_cparams(("arbitrary",)),
        name="sample_attn",
    )(page_table.reshape(-1), q_s, c_all, kr_all, wukt, wuk, wuv, gkn, cache_c, cache_kr_t)


DECAY_SCALE = 0.6065306597126334


def _rwkv_proj_kernel(*refs, sample, has_vres, tiles_per_seq, nq):
    it = iter(refs)
    x_ref, g_ref, mu_ref = next(it), next(it), next(it)
    shift_ref = next(it) if sample else None
    wr, wk, wv, w0, w1, w2, a0, a1, a2, g1, g2 = (next(it) for _ in range(11))
    if has_vres:
        v0, v1, v2, vf_ref = (next(it) for _ in range(4))
    r_ref, k_ref, v_ref, lw_ref, alr_ref, gate_ref, hl_ref = (next(it) for _ in range(7))
    carry_ref = next(it)
    tm = x_ref.shape[0]
    i = pl.program_id(0)

    h = _rms(x_ref[...], g_ref[...])
    rolled = pltpu.roll(h, 1, axis=0)
    row = lax.broadcasted_iota(jnp.int32, (tm, 1), 0)
    if sample:
        prev = jnp.where(row % nq == 0, shift_ref[...], rolled)
        hl_ref[...] = h
    else:
        @pl.when(i == 0)
        def _():
            carry_ref[...] = jnp.zeros(carry_ref.shape, F32)

        first = jnp.where(i % tiles_per_seq == 0, 0.0, carry_ref[0:1, :])
        prev = jnp.where(row == 0, first, rolled)
        carry_ref[0:1, :] = h[tm - 1:tm, :]
        hl_ref[...] = h[tm - 8:tm, :]
    xx = prev - h
    mu = mu_ref[...]
    mix = lambda n: (h + xx * mu[n:n + 1, :]).astype(BF16)
    xr, xw, xk, xv, xa, xg = (mix(n) for n in range(6))
    r = _dot(xr, wr[...])
    k = _dot(xk, wk[...])
    v = _dot(xv, wv[...])
    wl = w0[...] + _dot(jnp.tanh(_dot(xw, w1[...])).astype(BF16), w2[...])
    lw = (-DECAY_SCALE) * jax.nn.sigmoid(wl)
    alr = jax.nn.sigmoid(a0[...] + _dot(_dot(xa, a1[...]).astype(BF16), a2[...]))
    gate_ref[...] = _dot(jax.nn.sigmoid(_dot(xg, g1[...])).astype(BF16), g2[...])
    if has_vres:
        vgate = jax.nn.sigmoid(v0[...] + _dot(_dot(xv, v1[...]).astype(BF16), v2[...]))
    for p in range(HEAD_PAIRS):
        sl = slice(p * PAIR_W, (p + 1) * PAIR_W)
        vp = v[:, sl]
        if has_vres:
            vp = vp + (vf_ref[p] - vp) * vgate[:, sl]
        r_ref[p] = r[:, sl]
        k_ref[p] = k[:, sl]
        v_ref[p] = vp
        lw_ref[p] = lw[:, sl]
        alr_ref[p] = alr[:, sl]


def _rwkv_weights(j, rwkv_mu, rwkv_w_r, rwkv_w_k, rwkv_w_v, rwkv_w0, rwkv_w1, rwkv_w2, rwkv_a0, rwkv_a1,
                  rwkv_a2, rwkv_v0, rwkv_v1, rwkv_v2, rwkv_g1, rwkv_g2):
    row = lambda a: a.reshape(1, D_MODEL)
    w = [rwkv_w_r[j].astype(BF16), rwkv_w_k[j].astype(BF16), rwkv_w_v[j].astype(BF16),
         row(rwkv_w0[j]), rwkv_w1[j].astype(BF16), rwkv_w2[j].astype(BF16),
         row(rwkv_a0[j]), rwkv_a1[j].astype(BF16), rwkv_a2[j].astype(BF16),
         rwkv_g1[j].astype(BF16), rwkv_g2[j].astype(BF16)]
    if j > 0:
        w += [row(rwkv_v0[j - 1]), rwkv_v1[j - 1].astype(BF16), rwkv_v2[j - 1].astype(BF16)]
    return rwkv_mu[j], w


def _rwkv_proj(x, row0, n, g, mu, weights, v_first, shift_exp, seq_len):
    sample = shift_exp is not None
    has_vres = v_first is not None
    tm = _pick(n if sample else seq_len, (256, 128, 64, 32, 16, 8))
    assert row0 % tm == 0 and n % tm == 0
    off = row0 // tm
    if sample:
        assert tm % seq_len == 0
        tiles_per_seq = 1
    else:
        assert seq_len % tm == 0
        tiles_per_seq = seq_len // tm
    full = lambda a: pl.BlockSpec(a.shape, lambda i: (0,) * a.ndim)
    pair_spec = pl.BlockSpec((HEAD_PAIRS, tm, PAIR_W), lambda i: (0, i, 0))
    row_spec = pl.BlockSpec((tm, D_MODEL), lambda i: (i, 0))
    g2d = g.reshape(1, D_MODEL)
    args = [x, g2d, mu]
    specs = [pl.BlockSpec((tm, D_MODEL), lambda i: (i + off, 0)), full(g2d), full(mu)]
    if sample:
        args.append(shift_exp)
        specs.append(row_spec)
    args += weights
    specs += [full(a) for a in weights]
    if has_vres:
        args.append(v_first)
        specs.append(pair_spec)
    pair_shape = jax.ShapeDtypeStruct((HEAD_PAIRS, n, PAIR_W), F32)
    if sample:
        hl_spec, hl_shape = row_spec, jax.ShapeDtypeStruct((n, D_MODEL), F32)
    else:
        hl_spec = pl.BlockSpec((8, D_MODEL), lambda i: (i, 0))
        hl_shape = jax.ShapeDtypeStruct((n // tm * 8, D_MODEL), F32)
    return pl.pallas_call(
        functools.partial(_rwkv_proj_kernel, sample=sample, has_vres=has_vres,
                          tiles_per_seq=tiles_per_seq, nq=seq_len),
        grid=(n // tm,),
        in_specs=specs,
        out_specs=[pair_spec] * 5 + [row_spec, hl_spec],
        out_shape=[pair_shape] * 5 + [jax.ShapeDtypeStruct((n, D_MODEL), F32), hl_shape],
        scratch_shapes=[pltpu.VMEM((8, D_MODEL), F32)],
        compiler_params=_cparams(("arbitrary",)),
        name="rwkv_proj",
    )(*args)


def _wkv_masks(c, seg):
    c2 = 2 * c
    ri = lax.broadcasted_iota(jnp.int32, (c2, c2), 0)
    cj = lax.broadcasted_iota(jnp.int32, (c2, c2), 1)
    t, s = ri % c, cj % c
    same = ((ri // c) == (cj // c)) & ((t // seg) == (s // seg))
    levels = []
    step = 1
    while step < seg:
        levels.append(same & ((t ^ s) < 2 * step) & ((t & step) != 0) & ((s & step) == 0))
        step *= 2
    ti = lax.broadcasted_iota(jnp.int32, (c, c), 0)
    tj = lax.broadcasted_iota(jnp.int32, (c, c), 1)
    same_seq = (ti // seg) == (tj // seg)
    lane = lax.broadcasted_iota(jnp.int32, (1, PAIR_W), 1)
    row = lax.broadcasted_iota(jnp.int32, (c2, 1), 0)
    ei = lax.broadcasted_iota(jnp.int32, (PAIR_W, PAIR_W), 0)
    ej = lax.broadcasted_iota(jnp.int32, (PAIR_W, PAIR_W), 1)
    return dict(
        strict=same & (t > s), incl=same & (t >= s), levels=levels,
        tri=jnp.where(same_seq & (ti >= tj), 1.0, 0.0).astype(BF16),
        seq_ones=jnp.where(same_seq, 1.0, 0.0).astype(BF16),
        step_row=lax.broadcasted_iota(jnp.int32, (c, 1), 0),
        m0=lane < HEAD_SIZE, own=(row < c) == (lane < HEAD_SIZE), eye=ei == ej,
    )


def _wkv_local(chunks, prm, msk, seg):
    kkp, kap, rkp, lnw, lnb = prm
    n = len(chunks)
    c = chunks[0][0].shape[0]
    c2 = 2 * c
    nseg = c // seg
    m0 = msk["m0"]
    tri = msk["tri"]
    b16 = lambda a: a.astype(BF16)
    cat0 = lambda a, b: jnp.concatenate([a, b], axis=0)
    cat1 = lambda a, b: jnp.concatenate([a, b], axis=1)

    def stack(a):
        return cat0(jnp.where(m0, a, 0.0), jnp.where(m0, 0.0, a))

    def seq_rows(a, g):
        return a if nseg == 1 else cat0(a[g * seg:(g + 1) * seg], a[c + g * seg:c + (g + 1) * seg])

    cums, tots = [], []
    for (_, _, _, lw, _) in chunks:
        if nseg == 1:
            cum = lw
            step = 1
            while step < c:
                cum = cum + jnp.where(msk["step_row"] >= step, pltpu.roll(cum, step, axis=0), 0.0)
                step *= 2
            cums.append(cum)
            tots.append(cum[c - 1:c, :])
            continue
        hi = b16(lw)
        r1 = lw - hi.astype(F32)
        mid = b16(r1)
        lo = b16(r1 - mid.astype(F32))
        cums.append(_dot(tri, hi) + _dot(tri, mid) + _dot(tri, lo))
        ones = msk["seq_ones"]
        tots.append(_dot(ones, hi) + _dot(ones, mid) + _dot(ones, lo))

    lhs, rhs, at, rt, bh, kh, v16, vs, bonus, gcol = ([] for _ in range(10))
    for (r, k, v, lw, alr), cum, tot in zip(chunks, cums, tots):
        e_in, e_ex = jnp.exp(cum), jnp.exp(cum - lw)
        e_inv, e_end = jnp.exp(-cum), jnp.exp(tot - cum)
        kk = stack(k * kkp)
        kk = kk * (1.0 / jnp.maximum(jnp.sqrt(jnp.sum(kk * kk, axis=-1, keepdims=True)), 1e-12))
        kmod = stack(k * (1.0 + (alr - 1.0) * kap))
        r_s, v_s = stack(r), stack(v)
        bb = kk * stack(alr)
        a_t, r_t = -kk * cat0(e_ex, e_ex), r_s * cat0(e_in, e_in)
        lhs.append(b16(cat0(a_t, r_t)))
        rhs.append(b16(cat0(bb * cat0(e_inv, e_inv), kmod * cat0(e_inv, e_inv))))
        at.append(a_t)
        rt.append(r_t)
        bh.append(bb * cat0(e_end, e_end))
        kh.append(kmod * cat0(e_end, e_end))
        v16.append(b16(v_s))
        vs.append(v_s)
        bonus.append(jnp.sum(r_s * kmod * rkp, axis=-1, keepdims=True) * v_s)
        gcol.append([jnp.sum(jnp.where(msk["eye"], jnp.exp(tot[g * seg:g * seg + 1, :]), 0.0), axis=1, keepdims=True)
                     for g in range(nseg)])

    yield
    gram = [_dot_nt(a, b) for a, b in zip(lhs, rhs)]
    yield
    ab = [jnp.where(msk["strict"], g[:c2, :c2], 0.0) for g in gram]
    ak = [b16(jnp.where(msk["strict"], g[:c2, c2:], 0.0)) for g in gram]
    rb = [b16(jnp.where(msk["incl"], g[c2:, :c2], 0.0)) for g in gram]
    rk = [b16(jnp.where(msk["incl"], g[c2:, c2:], 0.0)) for g in gram]
    akv = [_dot(a, v) for a, v in zip(ak, v16)]
    rkv = [_dot(a, v) for a, v in zip(rk, v16)]
    khv = [[_dot_tn(b16(seq_rows(a, g)), b16(seq_rows(v, g))) for g in range(nseg)] for a, v in zip(kh, vs)]

    levels = msk["levels"]
    e = [jnp.where(levels[0], a, 0.0) for a in ab]
    for lvl in levels[1:]:
        yield
        low = [jnp.where(lvl, a, 0.0) for a in ab]
        x = [lo_ + _dot(b16(e_), b16(lo_)) for e_, lo_ in zip(e, low)]
        yield
        e = [e_ + x_ + _dot(b16(x_), b16(e_)) for e_, x_ in zip(e, x)]

    yield
    w2 = [cat1(a, b) for a, b in zip(akv, at)]
    u2 = [w + _dot(b16(e_), b16(w)) for w, e_ in zip(w2, e)]
    yield
    y2 = [_dot(a, b16(u)) + cat1(b, r_) for a, u, b, r_ in zip(rb, u2, rkv, rt)]
    mj = [[_dot_tn(b16(seq_rows(a, g)), b16(seq_rows(u, g))) for g in range(nseg)] for a, u in zip(bh, u2)]
    return [dict(y2=y2[i], mj=mj[i], khv=khv[i], gcol=gcol[i], bonus=bonus[i]) for i in range(n)]


def _wkv_state_step(loc, prm, msk, seg, h_in):
    lnw, lnb = prm[3], prm[4]
    c2 = loc["y2"].shape[0]
    c = c2 // 2
    nseg = c // seg
    b16 = lambda a: a.astype(BF16)
    parts, h_out = [], []
    for g in range(nseg):
        h = h_in if nseg == 1 else h_in[g]
        h16 = b16(h)
        y2g = loc["y2"]
        if nseg > 1:
            y2g = jnp.concatenate([y2g[g * seg:(g + 1) * seg], y2g[c + g * seg:c + (g + 1) * seg]], axis=0)
        mj = loc["mj"][g]
        parts.append(y2g[:, :PAIR_W] + _dot(b16(y2g[:, PAIR_W:]), h16))
        h_out.append(loc["gcol"][g] * h + _dot(b16(mj[:, PAIR_W:]), h16) + (mj[:, :PAIR_W] + loc["khv"][g]))
    y = parts[0] if nseg == 1 else jnp.concatenate([p[:seg] for p in parts] + [p[seg:] for p in parts], axis=0)

    own = msk["own"]
    inv_n = 1.0 / HEAD_SIZE
    mean = jnp.sum(y, axis=-1, keepdims=True) * inv_n
    d = jnp.where(own, y - mean, 0.0)
    var = jnp.sum(d * d, axis=-1, keepdims=True) * inv_n
    z_st = d * lax.rsqrt(var + LNX_EPS) * lnw + jnp.where(own, lnb, 0.0) + loc["bonus"]
    return z_st[:c] + z_st[c:], (h_out[0] if nseg == 1 else h_out)


def _finish(gen):
    try:
        while True:
            next(gen)
    except StopIteration as stop:
        return stop.value


def _wkv_block(chunks, prm, msk, h0, seg):
    n = len(chunks)
    c = chunks[0][0].shape[0]
    zs, hs = [], []
    if seg < c:
        locs = _finish(_wkv_local(chunks, prm, msk, seg))
        for i in range(n):
            z, h_seq = _wkv_state_step(locs[i], prm, msk, seg, h0[i])
            zs.append(z)
            hs += h_seq
        return zs, hs
    h = h0
    prev = []
    for start in range(0, n, WKV_GROUP):
        local = _wkv_local(chunks[start:start + WKV_GROUP], prm, msk, seg)
        for loc in prev:
            next(local)
            z, h = _wkv_state_step(loc, prm, msk, seg, h)
            zs.append(z)
            hs.append(h)
        prev = _finish(local)
    for loc in prev:
        z, h = _wkv_state_step(loc, prm, msk, seg, h)
        zs.append(z)
        hs.append(h)
    return zs, hs


def _state_in(s_ref, i):
    z = jnp.zeros((HEAD_SIZE, HEAD_SIZE), F32)
    top = jnp.concatenate([s_ref[i, 0], z], axis=1)
    bot = jnp.concatenate([z, s_ref[i, 1]], axis=1)
    return jnp.concatenate([top, bot], axis=0).T


def _state_out(s_ref, i, hbd):
    ht = hbd.T
    s_ref[i, 0] = ht[:HEAD_SIZE, :HEAD_SIZE]
    s_ref[i, 1] = ht[HEAD_SIZE:, HEAD_SIZE:]


def _wkv_kernel(*refs, c, seg, nb, blocks_per_seq, per_seq_state):
    data = refs[:5]
    prm = tuple(ref[0] for ref in refs[5:10])
    msk = _wkv_masks(c, seg)
    chunks = [tuple(ref[0, ci * c:(ci + 1) * c, :] for ref in data) for ci in range(nb)]
    if per_seq_state:
        s0_ref, z_ref, sout_ref = refs[10:]
        nseg = c // seg
        h0 = [[_state_in(s0_ref, ci * nseg + g) for g in range(nseg)] for ci in range(nb)]
        zs, hs = _wkv_block(chunks, prm, msk, h0, seg)
        for si in range(nb * nseg):
            _state_out(sout_ref, si, hs[si])
    else:
        z_ref, sout_ref, h_ref = refs[10:]
        b = pl.program_id(1)

        @pl.when(b % blocks_per_seq == 0)
        def _():
            h_ref[...] = jnp.zeros(h_ref.shape, F32)

        zs, hs = _wkv_block(chunks, prm, msk, h_ref[...], c)
        h_ref[...] = hs[-1]

        @pl.when(b % blocks_per_seq == blocks_per_seq - 1)
        def _():
            _state_out(sout_ref, 0, hs[-1])
    for ci in range(nb):
        z_ref[ci * c:(ci + 1) * c, :] = zs[ci]


def _pair_rows(a):
    return a.reshape(HEAD_PAIRS, 1, PAIR_W)


def _wkv(r, k, v, lw, alr, params, s0, n_seq, seq_len):
    c = WKV_CHUNK
    per_seq_state = s0 is not None
    n_chunks = n_seq * seq_len // c
    if per_seq_state:
        assert c % seq_len == 0 and (n_seq * seq_len) % c == 0
        seg, nb, blocks_per_seq = seq_len, _pick(n_chunks, (8, 4, 2, 1)), 1
        seqs = nb * (c // seg)
        sspec = pl.BlockSpec((seqs, 2, HEAD_SIZE, HEAD_SIZE), lambda p, b: (b, p, 0, 0))
        extra_in, extra_args, scratch = [sspec], [s0], []
    else:
        assert seq_len % c == 0
        seg, nb = c, _pick(seq_len // c, (32, 16, 8, 4, 2, 1))
        blocks_per_seq = seq_len // c // nb
        sspec = pl.BlockSpec((1, 2, HEAD_SIZE, HEAD_SIZE), lambda p, b: (b // blocks_per_seq, p, 0, 0))
        extra_in, extra_args, scratch = [], [], [pltpu.VMEM((PAIR_W, PAIR_W), F32)]
    n_blocks = n_chunks // nb
    dspec = pl.BlockSpec((1, nb * c, PAIR_W), lambda p, b: (p, b, 0))
    pspec = pl.BlockSpec((1, 1, PAIR_W), lambda p, b: (p, 0, 0))
    return pl.pallas_call(
        functools.partial(_wkv_kernel, c=c, seg=seg, nb=nb, blocks_per_seq=blocks_per_seq,
                          per_seq_state=per_seq_state),
        grid=(HEAD_PAIRS, n_blocks),
        in_specs=[dspec] * 5 + [pspec] * 5 + extra_in,
        out_specs=[pl.BlockSpec((nb * c, PAIR_W), lambda p, b: (b, p)), sspec],
        out_shape=[jax.ShapeDtypeStruct((n_seq * seq_len, D_MODEL), F32),
                   jax.ShapeDtypeStruct((n_seq, RWKV_HEADS, HEAD_SIZE, HEAD_SIZE), F32)],
        scratch_shapes=scratch,
        compiler_params=_cparams(("parallel", "arbitrary")),
        name="wkv",
    )(r, k, v, lw, alr, *[_pair_rows(p) for p in params], *extra_args)


def kernel(x_prompt, x_sample, cache_kv_latent, cache_k_rope, state_wkv, state_shift, page_table,
           ffn_norm, ffn_w_gate, ffn_w_up, ffn_w_down, mix_norm,
           mla_w_down, mla_g_q_lat, mla_g_kv_lat, mla_w_uq, mla_w_uk, mla_w_uv, mla_g_qn, mla_g_kn, mla_w_o,
           rwkv_mu, rwkv_w_r, rwkv_w_k, rwkv_w_v, rwkv_w_o, rwkv_w0, rwkv_w1, rwkv_w2,
           rwkv_a0, rwkv_a1, rwkv_a2, rwkv_v0, rwkv_v1, rwkv_v2, rwkv_g1, rwkv_g2,
           rwkv_k_k, rwkv_k_a, rwkv_r_k, rwkv_lnx_w, rwkv_lnx_b):
    nb, l, _ = x_prompt.shape
    ns, nq, _ = x_sample.shape
    n_p, n_s = nb * l, ns * nq
    depth = ffn_norm.shape[0]
    past = page_table.shape[1] * PAGE_SIZE
    x = (x_prompt.reshape(n_p, D_MODEL), x_sample.reshape(n_s, D_MODEL))

    tm = _pick(n_s, (512, 256, 128, 64, 32, 16, 8))
    assert l % tm == 0 and tm % nq == 0
    cos_p, sin_p = _rope_tables(jnp.arange(l))
    cos_s, sin_s = _rope_tables(past + jnp.arange(nq))
    ctab = jnp.concatenate([cos_p, jnp.tile(cos_s, (tm // nq, 1))], axis=0)
    stab = jnp.concatenate([sin_p, jnp.tile(sin_s, (tm // nq, 1))], axis=0)
    n_ptiles, tiles_per_seq = n_p // tm, l // tm
    tab_index = lambda i: jnp.where(i < n_ptiles, i % tiles_per_seq, tiles_per_seq)

    lat_rows, rope_rows, p_wkv, p_shift, s_wkv, s_shift = [], [], [], [], [], []
    vf_p = vf_s = None
    for i in range(depth):
        x = _ffn(x, ffn_norm[i, 0], ffn_w_gate[i, 0].astype(BF16), ffn_w_up[i, 0].astype(BF16),
                 ffn_w_down[i, 0].astype(BF16))
        if i % 2 == 0:
            m = i // 2
            w = _mla_weights(m, mla_w_down, mla_g_q_lat, mla_g_kv_lat, mla_w_uq, mla_w_uk, mla_w_uv,
                             mla_g_qn, mla_g_kn)
            q, c, kr, k, v = _mla_proj(x, mix_norm[i], w, ctab, stab, tm, tab_index)
            o_p = _prompt_attn(q, k, v, nb, l)
            q_s = jnp.transpose(q[:, n_p:, :].reshape(MLA_HEADS, ns, nq, QK_PAD), (1, 0, 2, 3))
            q_s = q_s.reshape(ns, MLA_HEADS * nq, QK_PAD)
            o_s = _sample_attn(q_s, c, kr, n_p, cache_kv_latent, cache_k_rope, m, page_table,
                               jnp.transpose(w["wuk"]), w["wuk"], w["wuv"], w["gkn"])
            proj = (o_p, o_s, mla_w_o[m].astype(BF16), None, None)
            lat_rows.append(c)
            rope_rows.append(kr)
        else:
            j = i // 2
            mu, w = _rwkv_weights(j, rwkv_mu, rwkv_w_r, rwkv_w_k, rwkv_w_v, rwkv_w0, rwkv_w1, rwkv_w2,
                                  rwkv_a0, rwkv_a1, rwkv_a2, rwkv_v0, rwkv_v1, rwkv_v2, rwkv_g1, rwkv_g2)
            params = [rwkv_k_k[j], rwkv_k_a[j], rwkv_r_k[j], rwkv_lnx_w[j], rwkv_lnx_b[j]]
            r, k, v, lw, alr, gate_p, hl = _rwkv_proj(x, 0, n_p, mix_norm[i], mu, w, vf_p, None, l)
            if j == 0:
                vf_p = v
            z_p, st_p = _wkv(r, k, v, lw, alr, params, None, nb, l)
            p_wkv.append(st_p)
            p_shift.append(hl.reshape(nb, -1, 8, D_MODEL)[:, -1, 7])
            shift_exp = jnp.zeros((ns, nq, D_MODEL), F32).at[:, 0, :].set(state_shift[j]).reshape(n_s, D_MODEL)
            r, k, v, lw, alr, gate_s, h_s = _rwkv_proj(x, n_p, n_s, mix_norm[i], mu, w, vf_s, shift_exp, nq)
            if j == 0:
                vf_s = v
            z_s, st_s = _wkv(r, k, v, lw, alr, params, state_wkv[j], ns, nq)
            s_wkv.append(st_s)
            s_shift.append(h_s.reshape(ns, nq, D_MODEL)[:, -1])
            proj = (z_p, z_s, rwkv_w_o[j].astype(BF16), gate_p, gate_s)
        x = _ffn(x, ffn_norm[i, 1], ffn_w_gate[i, 1].astype(BF16), ffn_w_up[i, 1].astype(BF16),
                 ffn_w_down[i, 1].astype(BF16), proj, split=i == depth - 1)

    lat = jnp.stack(lat_rows)
    rope = jnp.stack(rope_rows)
    return (x[0].reshape(nb, l, D_MODEL), x[1].reshape(ns, nq, D_MODEL),
            lat[:, :n_p].reshape(-1, nb, l, KV_RANK), rope[:, :n_p].reshape(-1, nb, l, ROPE_DIM),
            jnp.stack(p_wkv), jnp.stack(p_shift),
            lat[:, n_p:].reshape(-1, ns, nq, KV_RANK), rope[:, n_p:].reshape(-1, ns, nq, ROPE_DIM),
            jnp.stack(s_wkv), jnp.stack(s_shift))
```

```python
import functools

import jax
import jax.numpy as jnp
from jax import lax
from jax.experimental import pallas as pl
from jax.experimental.pallas import tpu as pltpu

F32 = jnp.float32
BF16 = jnp.bfloat16

D_MODEL = 1024
D_FF = 2816
RMS_EPS = 1e-6
MLA_HEADS = 8
Q_RANK = 384
KV_RANK = 256
NOPE_DIM = 128
ROPE_DIM = 64
QK_DIM = NOPE_DIM + ROPE_DIM
QK_PAD = 256
V_DIM = 128
ROPE_THETA = 10000.0
ATTN_SCALE = QK_DIM ** -0.5
PAGE_SIZE = 128
HEAD_SIZE = 64
RWKV_HEADS = D_MODEL // HEAD_SIZE
HEAD_PAIRS = RWKV_HEADS // 2
PAIR_W = 2 * HEAD_SIZE
LNX_EPS = 64e-5
WKV_CHUNK = 64
WKV_GROUP = 8

VMEM_LIMIT = 48 * 1024 * 1024


def _cparams(sem):
    return pltpu.CompilerParams(dimension_semantics=sem, vmem_limit_bytes=VMEM_LIMIT)


def _pick(n, cands):
    for c in cands:
        if n % c == 0:
            return c
    raise ValueError(f"no tile in {cands} divides {n}")


def _rms(x, g):
    return x * lax.rsqrt(jnp.mean(x * x, axis=-1, keepdims=True) + RMS_EPS) * g


def _dot(a, b):
    return jnp.dot(a, b, preferred_element_type=F32)


def _dot_nt(a, b):
    return lax.dot_general(a, b, (((1,), (1,)), ((), ())), preferred_element_type=F32)


def _dot_tn(a, b):
    return lax.dot_general(a, b, (((0,), (0,)), ((), ())), preferred_element_type=F32)


FFN_CHUNK = 256


def _swiglu_res(x, g_ref, wg_ref, wu_ref, wd_ref):
    hb = _rms(x, g_ref[...]).astype(BF16)
    acc = x
    for c in range(D_FF // FFN_CHUNK):
        sl = slice(c * FFN_CHUNK, (c + 1) * FFN_CHUNK)
        a = _dot(hb, wg_ref[:, sl])
        u = _dot(hb, wu_ref[:, sl])
        act = (0.5 * a * jax.nn.sigmoid(a) * u).astype(BF16)
        acc = acc + _dot(act, wd_ref[sl, :])
    return acc


def _ffn_kernel(x_ref, g_ref, wg_ref, wu_ref, wd_ref, o_ref):
    o_ref[...] = _swiglu_res(x_ref[...], g_ref, wg_ref, wu_ref, wd_ref)


def _ffn_parts_kernel(*refs, npt, two_x, proj, gated):
    it = iter(refs)
    xp_ref = next(it)
    xs_ref = next(it) if two_x else xp_ref
    ap_ref = as_ref = gp_ref = gs_ref = wp_ref = None
    if proj:
        ap_ref, as_ref = next(it), next(it)
        if gated:
            gp_ref, gs_ref = next(it), next(it)
        wp_ref = next(it)
    g_ref, wg_ref, wu_ref, wd_ref = (next(it) for _ in range(4))
    o_ref = next(it)
    x1_ref = next(it)
    i = pl.program_id(0)

    def stage(x_ref, a_ref, gate_ref):
        x = x_ref[...]
        if proj:
            a = a_ref[...] if gate_ref is None else a_ref[...] * gate_ref[...]
            x = x + _dot(a.astype(BF16), wp_ref[...])
        x1_ref[...] = x

    @pl.when(i < npt)
    def _():
        stage(xp_ref, ap_ref, gp_ref)

    @pl.when(i >= npt)
    def _():
        stage(xs_ref, as_ref, gs_ref)

    o_ref[...] = _swiglu_res(x1_ref[...], g_ref, wg_ref, wu_ref, wd_ref)


def _resident(shape):
    return pl.BlockSpec(shape, lambda i: (0,) * len(shape), pipeline_mode=pl.Buffered(1))


def _ffn(x, g, wg, wu, wd, proj=None, split=False):
    two_x = isinstance(x, tuple)
    n = x[0].shape[0] + x[1].shape[0] if two_x else x.shape[0]
    wspecs = [_resident((1, D_MODEL)), _resident(wg.shape), _resident(wu.shape), _resident(wd.shape)]
    wargs = (g.reshape(1, D_MODEL), wg, wu, wd)
    cparams = pltpu.CompilerParams(dimension_semantics=("parallel",), vmem_limit_bytes=56 * 1024 * 1024)
    if not two_x and proj is None:
        tm = _pick(n, (512, 256, 128, 64, 32, 16, 8))
        xspec = pl.BlockSpec((tm, D_MODEL), lambda i: (i, 0))
        return pl.pallas_call(
            _ffn_kernel, grid=(n // tm,), in_specs=[xspec] + wspecs, out_specs=xspec,
            out_shape=jax.ShapeDtypeStruct((n, D_MODEL), F32), compiler_params=cparams, name="ffn",
        )(x, *wargs)

    if two_x:
        n_p, n_s = x[0].shape[0], x[1].shape[0]
    else:
        n_p, n_s = proj[0].shape[0], proj[1].shape[0]
    tm = _pick(n_s, (512, 256, 128, 64, 32, 16, 8))
    assert n == n_p + n_s and n_p % tm == 0
    npt = n_p // tm

    def run(off, cnt):
        row = lambda width: pl.BlockSpec((tm, width), lambda i: (i + off, 0))
        prm = lambda width: pl.BlockSpec((tm, width), lambda i: (jnp.minimum(i + off, npt - 1), 0))
        smp = lambda width: pl.BlockSpec((tm, width), lambda i: (jnp.maximum(i + off - npt, 0), 0))
        specs, args = ([prm(D_MODEL), smp(D_MODEL)], list(x)) if two_x else ([row(D_MODEL)], [x])
        gated = False
        if proj is not None:
            a_p, a_s, w, gate_p, gate_s = proj
            kin = a_p.shape[1]
            gated = gate_p is not None
            specs += [prm(kin), smp(kin)] + ([prm(kin), smp(kin)] if gated else []) + [_resident(w.shape)]
            args += [a_p, a_s] + ([gate_p, gate_s] if gated else []) + [w]
        return pl.pallas_call(
            functools.partial(_ffn_parts_kernel, npt=npt - off, two_x=two_x, proj=proj is not None, gated=gated),
            grid=(cnt,),
            in_specs=specs + wspecs,
            out_specs=pl.BlockSpec((tm, D_MODEL), lambda i: (i, 0)),
            out_shape=jax.ShapeDtypeStruct((cnt * tm, D_MODEL), F32),
            scratch_shapes=[pltpu.VMEM((tm, D_MODEL), F32)],
            compiler_params=cparams,
            name="ffn",
        )(*args, *wargs)

    if split:
        return run(0, npt), run(npt, n // tm - npt)
    return run(0, n // tm)


def _mla_proj_kernel(x_ref, g_ref, wd_ref, gq_ref, gkv_ref, wqa_ref, wqb_ref, wuk_ref, wuv_ref,
                     gqn_ref, gkn_ref, ct_ref, st_ref,
                     q_ref, c_ref, kr_ref, k_ref, v_ref):
    hb = _rms(x_ref[...], g_ref[...]).astype(BF16)
    lat = _dot(hb, wd_ref[...])
    cq = _rms(lat[:, :Q_RANK], gq_ref[...]).astype(BF16)
    c = _rms(lat[:, Q_RANK:Q_RANK + KV_RANK], gkv_ref[...])
    ct = ct_ref[...]
    st = st_ref[...]
    o = Q_RANK + KV_RANK
    kr = lat[:, o:o + 128] * ct + lat[:, o + 128:o + 256] * st
    c_ref[...] = c
    kr_ref[...] = kr[:, :ROPE_DIM]

    qa = _dot(cq, wqa_ref[...])
    qb = _dot(cq, wqb_ref[...])
    gqn = gqn_ref[...]
    gkn = gkn_ref[...]
    cb = c.astype(BF16)
    kn = _dot(cb, wuk_ref[...])
    vv = _dot(cb, wuv_ref[...])
    ssr = jnp.sum(kr * kr, axis=-1, keepdims=True)
    for h in range(MLA_HEADS):
        nope = qa[:, h * QK_PAD:h * QK_PAD + 128]
        rp = qa[:, h * QK_PAD + 128:(h + 1) * QK_PAD] * ct + qb[:, h * 128:(h + 1) * 128] * st
        ss = jnp.sum(nope * nope, axis=-1, keepdims=True) + jnp.sum(rp * rp, axis=-1, keepdims=True)
        rs = lax.rsqrt(ss * (1.0 / QK_DIM) + RMS_EPS)
        q_ref[h, :, 0:128] = (nope * rs * gqn[:, 0:128]).astype(BF16)
        q_ref[h, :, 128:256] = (rp * rs * gqn[:, 128:256]).astype(BF16)
        knh = kn[:, h * 128:(h + 1) * 128]
        rk = lax.rsqrt((jnp.sum(knh * knh, axis=-1, keepdims=True) + ssr) * (1.0 / QK_DIM) + RMS_EPS)
        k_ref[h, :, 0:128] = (knh * rk * gkn[:, 0:128]).astype(BF16)
        k_ref[h, :, 128:256] = (kr * rk * gkn[:, 128:256]).astype(BF16)
        v_ref[h] = vv[:, h * 128:(h + 1) * 128].astype(BF16)


def _rot_cols(w):
    half = ROPE_DIM // 2
    return jnp.concatenate([-w[..., half:], w[..., :half]], axis=-1)


def _pad_lanes(w, n):
    return jnp.pad(w, [(0, 0)] * (w.ndim - 1) + [(0, n - w.shape[-1])])


def _mla_weights(m, mla_w_down, mla_g_q_lat, mla_g_kv_lat, mla_w_uq, mla_w_uk, mla_w_uv, mla_g_qn, mla_g_kn):
    wd = mla_w_down[m]
    o = Q_RANK + KV_RANK
    wkr = wd[:, o:]
    wd_ext = jnp.concatenate([wd[:, :o], _pad_lanes(wkr, 128), _pad_lanes(_rot_cols(wkr), 128)], axis=1)
    wq = mla_w_uq[m].reshape(Q_RANK, MLA_HEADS, QK_DIM)
    wqa = _pad_lanes(wq, QK_PAD).reshape(Q_RANK, MLA_HEADS * QK_PAD)
    wqb = _pad_lanes(_rot_cols(wq[..., NOPE_DIM:]), 128).reshape(Q_RANK, MLA_HEADS * 128)
    return dict(
        wd=wd_ext.astype(BF16), gq=mla_g_q_lat[m].reshape(1, Q_RANK), gkv=mla_g_kv_lat[m].reshape(1, KV_RANK),
        wqa=wqa.astype(BF16), wqb=wqb.astype(BF16),
        wuk=mla_w_uk[m].reshape(KV_RANK, MLA_HEADS * NOPE_DIM).astype(BF16),
        wuv=mla_w_uv[m].reshape(KV_RANK, MLA_HEADS * V_DIM).astype(BF16),
        gqn=_pad_lanes(mla_g_qn[m] * ATTN_SCALE, QK_PAD).reshape(1, QK_PAD),
        gkn=_pad_lanes(mla_g_kn[m], QK_PAD).reshape(1, QK_PAD),
    )


def _rope_tables(pos):
    inv = ROPE_THETA ** (-jnp.arange(0, ROPE_DIM, 2, dtype=F32) / ROPE_DIM)
    ang = pos.astype(F32)[:, None] * inv[None, :]
    cos, sin = jnp.cos(ang), jnp.sin(ang)
    return (_pad_lanes(jnp.concatenate([cos, cos], axis=-1), 128),
            _pad_lanes(jnp.concatenate([sin, sin], axis=-1), 128))


def _mla_proj(x, g, w, ctab, stab, tm, tab_index):
    n = x.shape[0]
    full = lambda shape: pl.BlockSpec(shape, lambda i: (0,) * len(shape))
    tspec = pl.BlockSpec((tm, 128), lambda i: (tab_index(i), 0))
    return pl.pallas_call(
        _mla_proj_kernel,
        grid=(n // tm,),
        in_specs=[
            pl.BlockSpec((tm, D_MODEL), lambda i: (i, 0)),
            full((1, D_MODEL)), full(w["wd"].shape), full((1, Q_RANK)), full((1, KV_RANK)),
            full(w["wqa"].shape), full(w["wqb"].shape), full(w["wuk"].shape), full(w["wuv"].shape),
            full((1, QK_PAD)), full((1, QK_PAD)), tspec, tspec,
        ],
        out_specs=[
            pl.BlockSpec((MLA_HEADS, tm, QK_PAD), lambda i: (0, i, 0)),
            pl.BlockSpec((tm, KV_RANK), lambda i: (i, 0)),
            pl.BlockSpec((tm, ROPE_DIM), lambda i: (i, 0)),
            pl.BlockSpec((MLA_HEADS, tm, QK_PAD), lambda i: (0, i, 0)),
            pl.BlockSpec((MLA_HEADS, tm, V_DIM), lambda i: (0, i, 0)),
        ],
        out_shape=[
            jax.ShapeDtypeStruct((MLA_HEADS, n, QK_PAD), BF16),
            jax.ShapeDtypeStruct((n, KV_RANK), F32),
            jax.ShapeDtypeStruct((n, ROPE_DIM), F32),
            jax.ShapeDtypeStruct((MLA_HEADS, n, QK_PAD), BF16),
            jax.ShapeDtypeStruct((MLA_HEADS, n, V_DIM), BF16),
        ],
        compiler_params=_cparams(("parallel",)),
        name="mla_proj",
    )(x, g.reshape(1, D_MODEL), w["wd"], w["gq"], w["gkv"], w["wqa"], w["wqb"], w["wuk"], w["wuv"],
      w["gqn"], w["gkn"], ctab, stab)


def _prompt_attn_kernel(q_ref, k_ref, v_ref, o_ref, *, tq):
    l = q_ref.shape[1]
    nq = l // tq
    row = lax.broadcasted_iota(jnp.int32, (tq, tq), 0)
    col = lax.broadcasted_iota(jnp.int32, (tq, tq), 1)
    diag_mask = row >= col
    for qi in range(nq):
        q = q_ref[0, qi * tq:(qi + 1) * tq, :]
        m = l_sum = acc = None
        for ki in range(qi + 1):
            k = k_ref[0, ki * tq:(ki + 1) * tq, :]
            v = v_ref[0, ki * tq:(ki + 1) * tq, :]
            s = _dot_nt(q, k)
            if ki == qi:
                s = jnp.where(diag_mask, s, -jnp.inf)
            m_blk = jnp.max(s, axis=-1, keepdims=True)
            if ki == 0:
                m = m_blk
                p = jnp.exp(s - m)
                l_sum = jnp.sum(p, axis=-1, keepdims=True)
                acc = _dot(p.astype(BF16), v)
            else:
                m_new = jnp.maximum(m, m_blk)
                alpha = jnp.exp(m - m_new)
                p = jnp.exp(s - m_new)
                l_sum = alpha * l_sum + jnp.sum(p, axis=-1, keepdims=True)
                acc = alpha * acc + _dot(p.astype(BF16), v)
                m = m_new
        o_ref[qi * tq:(qi + 1) * tq, :] = (acc / l_sum).astype(o_ref.dtype)


def _prompt_attn(q, k, v, nb, l):
    tq = _pick(l, (512, 256, 128))
    return pl.pallas_call(
        functools.partial(_prompt_attn_kernel, tq=tq),
        grid=(nb, MLA_HEADS),
        in_specs=[
            pl.BlockSpec((1, l, QK_PAD), lambda b, h: (h, b, 0)),
            pl.BlockSpec((1, l, QK_PAD), lambda b, h: (h, b, 0)),
            pl.BlockSpec((1, l, V_DIM), lambda b, h: (h, b, 0)),
        ],
        out_specs=pl.BlockSpec((l, V_DIM), lambda b, h: (b, h)),
        out_shape=jax.ShapeDtypeStruct((nb * l, MLA_HEADS * V_DIM), BF16),
        compiler_params=_cparams(("parallel", "parallel")),
        name="prompt_attn",
    )(q, k, v)


def _sample_attn_kernel(pt_ref, q_ref, cn_ref, krn_ref, wukt_ref, wuk_ref, wuv_ref, gkn_ref, cc_hbm, ckr_hbm,
                        o_ref, wq_ref, cb_ref, cbuf, kbuf, sem, *, m, n_pages, tc):
    b = pl.program_id(0)
    nseq = pl.num_programs(0)
    slot = b % 2
    nq = q_ref.shape[1] // MLA_HEADS
    rows = q_ref.shape[1]
    n_nope = MLA_HEADS * NOPE_DIM

    def page_copies(page, slot_, j):
        return (pltpu.make_async_copy(cc_hbm.at[m, page], cbuf.at[slot_, j], sem.at[slot_]),
                pltpu.make_async_copy(ckr_hbm.at[m, page], kbuf.at[slot_, j], sem.at[slot_]))

    def start_pages(seq, slot_):
        for j in range(n_pages):
            for cp in page_copies(pt_ref[seq * n_pages + j], slot_, j):
                cp.start()

    def wait_pages(slot_):
        for j in range(n_pages):
            for cp in page_copies(0, slot_, j):
                cp.wait()

    @pl.when(b == 0)
    def _():
        start_pages(0, 0)
        wq_ref[0:n_nope, :] = wukt_ref[...]

    wait_pages(slot)
    start_pages(jnp.where(b + 1 == nseq, 0, b + 1), 1 - slot)

    gkn = gkn_ref[...]
    qts, qrs = [], []
    for h in range(MLA_HEADS):
        qh = q_ref[0, h * nq:(h + 1) * nq, :].astype(F32)
        qg = (qh[:, :NOPE_DIM] * gkn[:, :NOPE_DIM]).astype(BF16)
        qts.append(_dot_nt(qg, wuk_ref[:, h * NOPE_DIM:(h + 1) * NOPE_DIM]))
        qrs.append(qh[:, NOPE_DIM:QK_DIM] * gkn[:, NOPE_DIM:QK_DIM])
    wq_ref[n_nope:n_nope + rows, :] = jnp.concatenate(qts, axis=0).astype(BF16)
    qr = jnp.concatenate(qrs, axis=0).astype(BF16)

    def scores(cbc, ssr, s_rope):
        t = cbc.shape[0]
        big = _dot_nt(wq_ref[...], cbc)
        rs = []
        for h in range(MLA_HEADS):
            kh = big[h * NOPE_DIM:(h + 1) * NOPE_DIM]
            ssq = jnp.sum(kh * kh, axis=0, keepdims=True)
            rs_h = lax.rsqrt((ssq + ssr) * (1.0 / QK_DIM) + RMS_EPS)
            rs.append(jnp.broadcast_to(rs_h, (nq, t)))
        return (big[n_nope:] + s_rope) * jnp.concatenate(rs, axis=0)

    pages_per_chunk = tc // PAGE_SIZE
    for j in range(n_pages):
        cb_ref[j * PAGE_SIZE:(j + 1) * PAGE_SIZE, :] = cbuf[slot, j].astype(BF16)
    s_list, c_list = [], []
    for ci in range(n_pages // pages_per_chunk):
        cbc = cb_ref[ci * tc:(ci + 1) * tc, :]
        krt = jnp.concatenate([kbuf[slot, ci * pages_per_chunk + jj] for jj in range(pages_per_chunk)], axis=1)
        ssr = jnp.sum(krt * krt, axis=0, keepdims=True)
        s_list.append(scores(cbc, ssr, _dot(qr, krt.astype(BF16))))
        c_list.append(cbc)

    pad = PAGE_SIZE - nq
    cn = jnp.concatenate([cn_ref[...], jnp.zeros((pad, KV_RANK), F32)], axis=0).astype(BF16)
    krn = jnp.concatenate([krn_ref[...], jnp.zeros((pad, ROPE_DIM), F32)], axis=0)
    ssr = _dot_nt(jnp.ones((8, ROPE_DIM), BF16), (krn * krn).astype(BF16))[0:1]
    s = scores(cn, ssr, _dot_nt(qr, krn.astype(BF16)))
    qpos = lax.broadcasted_iota(jnp.int32, (rows, PAGE_SIZE), 0) % nq
    tok = lax.broadcasted_iota(jnp.int32, (rows, PAGE_SIZE), 1)
    s_list.append(jnp.where(tok <= qpos, s, -jnp.inf))
    c_list.append(cn)

    mx = jnp.max(s_list[0], axis=-1, keepdims=True)
    for s in s_list[1:]:
        mx = jnp.maximum(mx, jnp.max(s, axis=-1, keepdims=True))
    ps = [jnp.exp(s - mx) for s in s_list]
    den = jnp.sum(ps[0], axis=-1, keepdims=True)
    for p in ps[1:]:
        den = den + jnp.sum(p, axis=-1, keepdims=True)
    acc = _dot(ps[0].astype(BF16), c_list[0])
    for p, cbc in zip(ps[1:], c_list[1:]):
        acc = acc + _dot(p.astype(BF16), cbc)
    o_lat = (acc / den).astype(BF16)
    for h in range(MLA_HEADS):
        o_ref[:, h * V_DIM:(h + 1) * V_DIM] = _dot(o_lat[h * nq:(h + 1) * nq], wuv_ref[:, h * V_DIM:(h + 1) * V_DIM])

    @pl.when(b == nseq - 1)
    def _():
        wait_pages(1 - slot)


def _sample_attn(q_s, c_all, kr_all, row0, cache_c, cache_kr, m, page_table, wukt, wuk, wuv, gkn):
    ns, n_pages = page_table.shape
    rows = q_s.shape[1]
    nq = rows // MLA_HEADS
    assert row0 % nq == 0 and nq % 8 == 0
    tc = 2 * PAGE_SIZE
    assert n_pages % 2 == 0 and n_pages <= 64
    blk0 = row0 // nq
    full = lambda shape: pl.BlockSpec(shape, lambda b, pt: (0,) * len(shape))

    cache_kr_t = jnp.swapaxes(cache_kr, 2, 3)
    grid_spec = pltpu.PrefetchScalarGridSpec(
        num_scalar_prefetch=1,
        grid=(ns,),
        in_specs=[
            pl.BlockSpec((1, rows, QK_PAD), lambda b, pt: (b, 0, 0)),
            pl.BlockSpec((nq, KV_RANK), lambda b, pt: (blk0 + b, 0)),
            pl.BlockSpec((nq, ROPE_DIM), lambda b, pt: (blk0 + b, 0)),
            full(wukt.shape), full(wuk.shape), full(wuv.shape), full((1, QK_PAD)),
            pl.BlockSpec(memory_space=pl.ANY), pl.BlockSpec(memory_space=pl.ANY),
        ],
        out_specs=pl.BlockSpec((nq, MLA_HEADS * V_DIM), lambda b, pt: (b, 0)),
        scratch_shapes=[
            pltpu.VMEM((MLA_HEADS * NOPE_DIM + rows, KV_RANK), BF16),
            pltpu.VMEM((n_pages * PAGE_SIZE, KV_RANK), BF16),
            pltpu.VMEM((2, n_pages, PAGE_SIZE, KV_RANK), F32),
            pltpu.VMEM((2, n_pages, ROPE_DIM, PAGE_SIZE), F32),
            pltpu.SemaphoreType.DMA((2,)),
        ],
    )
    return pl.pallas_call(
        functools.partial(_sample_attn_kernel, m=m, n_pages=n_pages, tc=tc),
        grid_spec=grid_spec,
        out_shape=jax.ShapeDtypeStruct((ns * nq, MLA_HEADS * V_DIM), F32),
        compiler_params=_cparams(("arbitrary",)),
        name="sample_attn",
    )(page_table.reshape(-1), q_s, c_all, kr_all, wukt, wuk, wuv, gkn, cache_c, cache_kr_t)


DECAY_SCALE = 0.6065306597126334


def _rwkv_proj_kernel(*refs, sample, has_vres, tiles_per_seq, nq):
    it = iter(refs)
    x_ref, g_ref, mu_ref = next(it), next(it), next(it)
    shift_ref = next(it) if sample else None
    wr, wk, wv, w0, w1, w2, a0, a1, a2, g1, g2 = (next(it) for _ in range(11))
    if has_vres:
        v0, v1, v2, vf_ref = (next(it) for _ in range(4))
    r_ref, k_ref, v_ref, lw_ref, alr_ref, gate_ref, hl_ref = (next(it) for _ in range(7))
    carry_ref = next(it)
    tm = x_ref.shape[0]
    i = pl.program_id(0)

    h = _rms(x_ref[...], g_ref[...])
    rolled = pltpu.roll(h, 1, axis=0)
    row = lax.broadcasted_iota(jnp.int32, (tm, 1), 0)
    if sample:
        prev = jnp.where(row % nq == 0, shift_ref[...], rolled)
        hl_ref[...] = h
    else:
        @pl.when(i == 0)
        def _():
            carry_ref[...] = jnp.zeros(carry_ref.shape, F32)

        first = jnp.where(i % tiles_per_seq == 0, 0.0, carry_ref[0:1, :])
        prev = jnp.where(row == 0, first, rolled)
        carry_ref[0:1, :] = h[tm - 1:tm, :]
        hl_ref[...] = h[tm - 8:tm, :]
    xx = prev - h
    mu = mu_ref[...]
    mix = lambda n: (h + xx * mu[n:n + 1, :]).astype(BF16)
    xr, xw, xk, xv, xa, xg = (mix(n) for n in range(6))
    r = _dot(xr, wr[...])
    k = _dot(xk, wk[...])
    v = _dot(xv, wv[...])
    wl = w0[...] + _dot(jnp.tanh(_dot(xw, w1[...])).astype(BF16), w2[...])
    lw = (-DECAY_SCALE) * jax.nn.sigmoid(wl)
    alr = jax.nn.sigmoid(a0[...] + _dot(_dot(xa, a1[...]).astype(BF16), a2[...]))
    gate_ref[...] = _dot(jax.nn.sigmoid(_dot(xg, g1[...])).astype(BF16), g2[...])
    if has_vres:
        vgate = jax.nn.sigmoid(v0[...] + _dot(_dot(xv, v1[...]).astype(BF16), v2[...]))
    for p in range(HEAD_PAIRS):
        sl = slice(p * PAIR_W, (p + 1) * PAIR_W)
        vp = v[:, sl]
        if has_vres:
            vp = vp + (vf_ref[p] - vp) * vgate[:, sl]
        r_ref[p] = r[:, sl]
        k_ref[p] = k[:, sl]
        v_ref[p] = vp
        lw_ref[p] = lw[:, sl]
        alr_ref[p] = alr[:, sl]


def _rwkv_weights(j, rwkv_mu, rwkv_w_r, rwkv_w_k, rwkv_w_v, rwkv_w0, rwkv_w1, rwkv_w2, rwkv_a0, rwkv_a1,
                  rwkv_a2, rwkv_v0, rwkv_v1, rwkv_v2, rwkv_g1, rwkv_g2):
    row = lambda a: a.reshape(1, D_MODEL)
    w = [rwkv_w_r[j].astype(BF16), rwkv_w_k[j].astype(BF16), rwkv_w_v[j].astype(BF16),
         row(rwkv_w0[j]), rwkv_w1[j].astype(BF16), rwkv_w2[j].astype(BF16),
         row(rwkv_a0[j]), rwkv_a1[j].astype(BF16), rwkv_a2[j].astype(BF16),
         rwkv_g1[j].astype(BF16), rwkv_g2[j].astype(BF16)]
    if j > 0:
        w += [row(rwkv_v0[j - 1]), rwkv_v1[j - 1].astype(BF16), rwkv_v2[j - 1].astype(BF16)]
    return rwkv_mu[j], w


def _rwkv_proj(x, row0, n, g, mu, weights, v_first, shift_exp, seq_len):
    sample = shift_exp is not None
    has_vres = v_first is not None
    tm = _pick(n if sample else seq_len, (512, 256, 128, 64, 32, 16, 8))
    assert row0 % tm == 0 and n % tm == 0
    off = row0 // tm
    if sample:
        assert tm % seq_len == 0
        tiles_per_seq = 1
    else:
        assert seq_len % tm == 0
        tiles_per_seq = seq_len // tm
    full = lambda a: pl.BlockSpec(a.shape, lambda i: (0,) * a.ndim)
    pair_spec = pl.BlockSpec((HEAD_PAIRS, tm, PAIR_W), lambda i: (0, i, 0))
    row_spec = pl.BlockSpec((tm, D_MODEL), lambda i: (i, 0))
    g2d = g.reshape(1, D_MODEL)
    args = [x, g2d, mu]
    specs = [pl.BlockSpec((tm, D_MODEL), lambda i: (i + off, 0)), full(g2d), full(mu)]
    if sample:
        args.append(shift_exp)
        specs.append(row_spec)
    args += weights
    specs += [_resident(a.shape) for a in weights]
    if has_vres:
        args.append(v_first)
        specs.append(pair_spec)
    pair_shape = jax.ShapeDtypeStruct((HEAD_PAIRS, n, PAIR_W), F32)
    if sample:
        hl_spec, hl_shape = row_spec, jax.ShapeDtypeStruct((n, D_MODEL), F32)
    else:
        hl_spec = pl.BlockSpec((8, D_MODEL), lambda i: (i, 0))
        hl_shape = jax.ShapeDtypeStruct((n // tm * 8, D_MODEL), F32)
    return pl.pallas_call(
        functools.partial(_rwkv_proj_kernel, sample=sample, has_vres=has_vres,
                          tiles_per_seq=tiles_per_seq, nq=seq_len),
        grid=(n // tm,),
        in_specs=specs,
        out_specs=[pair_spec] * 5 + [row_spec, hl_spec],
        out_shape=[pair_shape] * 5 + [jax.ShapeDtypeStruct((n, D_MODEL), F32), hl_shape],
        scratch_shapes=[pltpu.VMEM((8, D_MODEL), F32)],
        compiler_params=_cparams(("arbitrary",)),
        name="rwkv_proj",
    )(*args)


def _wkv_masks(c, seg):
    c2 = 2 * c
    ri = lax.broadcasted_iota(jnp.int32, (c2, c2), 0)
    cj = lax.broadcasted_iota(jnp.int32, (c2, c2), 1)
    t, s = ri % c, cj % c
    same = ((ri // c) == (cj // c)) & ((t // seg) == (s // seg))
    levels = []
    step = 1
    while step < seg:
        levels.append(same & ((t ^ s) < 2 * step) & ((t & step) != 0) & ((s & step) == 0))
        step *= 2
    ti = lax.broadcasted_iota(jnp.int32, (c, c), 0)
    tj = lax.broadcasted_iota(jnp.int32, (c, c), 1)
    same_seq = (ti // seg) == (tj // seg)
    lane = lax.broadcasted_iota(jnp.int32, (1, PAIR_W), 1)
    row = lax.broadcasted_iota(jnp.int32, (c2, 1), 0)
    ei = lax.broadcasted_iota(jnp.int32, (PAIR_W, PAIR_W), 0)
    ej = lax.broadcasted_iota(jnp.int32, (PAIR_W, PAIR_W), 1)
    return dict(
        strict=same & (t > s), incl=same & (t >= s), levels=levels,
        tri=jnp.where(same_seq & (ti >= tj), 1.0, 0.0).astype(BF16),
        seq_ones=jnp.where(same_seq, 1.0, 0.0).astype(BF16),
        step_row=lax.broadcasted_iota(jnp.int32, (c, 1), 0),
        m0=lane < HEAD_SIZE, own=(row < c) == (lane < HEAD_SIZE), eye=ei == ej,
    )


def _wkv_local(chunks, prm, msk, seg):
    kkp, kap, rkp, lnw, lnb = prm
    n = len(chunks)
    c = chunks[0][0].shape[0]
    c2 = 2 * c
    nseg = c // seg
    m0 = msk["m0"]
    tri = msk["tri"]
    b16 = lambda a: a.astype(BF16)
    cat0 = lambda a, b: jnp.concatenate([a, b], axis=0)
    cat1 = lambda a, b: jnp.concatenate([a, b], axis=1)

    def stack(a):
        return cat0(jnp.where(m0, a, 0.0), jnp.where(m0, 0.0, a))

    def seq_rows(a, g):
        return a if nseg == 1 else cat0(a[g * seg:(g + 1) * seg], a[c + g * seg:c + (g + 1) * seg])

    cums, tots = [], []
    for (_, _, _, lw, _) in chunks:
        if nseg == 1:
            cum = lw
            step = 1
            while step < c:
                cum = cum + jnp.where(msk["step_row"] >= step, pltpu.roll(cum, step, axis=0), 0.0)
                step *= 2
            cums.append(cum)
            tots.append(cum[c - 1:c, :])
            continue
        hi = b16(lw)
        r1 = lw - hi.astype(F32)
        mid = b16(r1)
        lo = b16(r1 - mid.astype(F32))
        cums.append(_dot(tri, hi) + _dot(tri, mid) + _dot(tri, lo))
        ones = msk["seq_ones"]
        tots.append(_dot(ones, hi) + _dot(ones, mid) + _dot(ones, lo))

    lhs, rhs, at, rt, bh, kh, v16, vs, bonus, gcol = ([] for _ in range(10))
    for (r, k, v, lw, alr), cum, tot in zip(chunks, cums, tots):
        e_in, e_ex = jnp.exp(cum), jnp.exp(cum - lw)
        e_inv, e_end = jnp.exp(-cum), jnp.exp(tot - cum)
        kk = stack(k * kkp)
        kk = kk * (1.0 / jnp.maximum(jnp.sqrt(jnp.sum(kk * kk, axis=-1, keepdims=True)), 1e-12))
        kmod = stack(k * (1.0 + (alr - 1.0) * kap))
        r_s, v_s = stack(r), stack(v)
        bb = kk * stack(alr)
        a_t, r_t = -kk * cat0(e_ex, e_ex), r_s * cat0(e_in, e_in)
        lhs.append(b16(cat0(a_t, r_t)))
        rhs.append(b16(cat0(bb * cat0(e_inv, e_inv), kmod * cat0(e_inv, e_inv))))
        at.append(a_t)
        rt.append(r_t)
        bh.append(bb * cat0(e_end, e_end))
        kh.append(kmod * cat0(e_end, e_end))
        v16.append(b16(v_s))
        vs.append(v_s)
        bonus.append(jnp.sum(r_s * kmod * rkp, axis=-1, keepdims=True) * v_s)
        gcol.append([jnp.sum(jnp.where(msk["eye"], jnp.exp(tot[g * seg:g * seg + 1, :]), 0.0), axis=1, keepdims=True)
                     for g in range(nseg)])

    yield
    gram = [_dot_nt(a, b) for a, b in zip(lhs, rhs)]
    yield
    ab = [jnp.where(msk["strict"], g[:c2, :c2], 0.0) for g in gram]
    ak = [b16(jnp.where(msk["strict"], g[:c2, c2:], 0.0)) for g in gram]
    rb = [b16(jnp.where(msk["incl"], g[c2:, :c2], 0.0)) for g in gram]
    rk = [b16(jnp.where(msk["incl"], g[c2:, c2:], 0.0)) for g in gram]
    akv = [_dot(a, v) for a, v in zip(ak, v16)]
    rkv = [_dot(a, v) for a, v in zip(rk, v16)]
    khv = [[_dot_tn(b16(seq_rows(a, g)), b16(seq_rows(v, g))) for g in range(nseg)] for a, v in zip(kh, vs)]

    levels = msk["levels"]
    e = [jnp.where(levels[0], a, 0.0) for a in ab]
    for lvl in levels[1:]:
        yield
        low = [jnp.where(lvl, a, 0.0) for a in ab]
        x = [lo_ + _dot(b16(e_), b16(lo_)) for e_, lo_ in zip(e, low)]
        yield
        e = [e_ + x_ + _dot(b16(x_), b16(e_)) for e_, x_ in zip(e, x)]

    yield
    w2 = [cat1(a, b) for a, b in zip(akv, at)]
    u2 = [w + _dot(b16(e_), b16(w)) for w, e_ in zip(w2, e)]
    yield
    y2 = [_dot(a, b16(u)) + cat1(b, r_) for a, u, b, r_ in zip(rb, u2, rkv, rt)]
    mj = [[_dot_tn(b16(seq_rows(a, g)), b16(seq_rows(u, g))) for g in range(nseg)] for a, u in zip(bh, u2)]
    return [dict(y2=y2[i], mj=mj[i], khv=khv[i], gcol=gcol[i], bonus=bonus[i]) for i in range(n)]


def _wkv_state_step(loc, prm, msk, seg, h_in):
    lnw, lnb = prm[3], prm[4]
    c2 = loc["y2"].shape[0]
    c = c2 // 2
    nseg = c // seg
    b16 = lambda a: a.astype(BF16)
    parts, h_out = [], []
    for g in range(nseg):
        h = h_in if nseg == 1 else h_in[g]
        h16 = b16(h)
        y2g = loc["y2"]
        if nseg > 1:
            y2g = jnp.concatenate([y2g[g * seg:(g + 1) * seg], y2g[c + g * seg:c + (g + 1) * seg]], axis=0)
        mj = loc["mj"][g]
        parts.append(y2g[:, :PAIR_W] + _dot(b16(y2g[:, PAIR_W:]), h16))
        h_out.append(loc["gcol"][g] * h + _dot(b16(mj[:, PAIR_W:]), h16) + (mj[:, :PAIR_W] + loc["khv"][g]))
    y = parts[0] if nseg == 1 else jnp.concatenate([p[:seg] for p in parts] + [p[seg:] for p in parts], axis=0)

    own = msk["own"]
    inv_n = 1.0 / HEAD_SIZE
    mean = jnp.sum(y, axis=-1, keepdims=True) * inv_n
    d = jnp.where(own, y - mean, 0.0)
    var = jnp.sum(d * d, axis=-1, keepdims=True) * inv_n
    z_st = d * lax.rsqrt(var + LNX_EPS) * lnw + jnp.where(own, lnb, 0.0) + loc["bonus"]
    return z_st[:c] + z_st[c:], (h_out[0] if nseg == 1 else h_out)


def _finish(gen):
    try:
        while True:
            next(gen)
    except StopIteration as stop:
        return stop.value


def _wkv_block(chunks, prm, msk, h0, seg):
    n = len(chunks)
    c = chunks[0][0].shape[0]
    zs, hs = [], []
    if seg < c:
        locs = _finish(_wkv_local(chunks, prm, msk, seg))
        for i in range(n):
            z, h_seq = _wkv_state_step(locs[i], prm, msk, seg, h0[i])
            zs.append(z)
            hs += h_seq
        return zs, hs
    h = h0
    prev = []
    for start in range(0, n, WKV_GROUP):
        local = _wkv_local(chunks[start:start + WKV_GROUP], prm, msk, seg)
        for loc in prev:
            next(local)
            z, h = _wkv_state_step(loc, prm, msk, seg, h)
            zs.append(z)
            hs.append(h)
        prev = _finish(local)
    for loc in prev:
        z, h = _wkv_state_step(loc, prm, msk, seg, h)
        zs.append(z)
        hs.append(h)
    return zs, hs


def _state_in(s_ref, i):
    z = jnp.zeros((HEAD_SIZE, HEAD_SIZE), F32)
    top = jnp.concatenate([s_ref[i, 0], z], axis=1)
    bot = jnp.concatenate([z, s_ref[i, 1]], axis=1)
    return jnp.concatenate([top, bot], axis=0).T


def _state_out(s_ref, i, hbd):
    ht = hbd.T
    s_ref[i, 0] = ht[:HEAD_SIZE, :HEAD_SIZE]
    s_ref[i, 1] = ht[HEAD_SIZE:, HEAD_SIZE:]


def _wkv_kernel(*refs, c, seg, nb, blocks_per_seq, per_seq_state):
    data = refs[:5]
    prm = tuple(ref[0] for ref in refs[5:10])
    msk = _wkv_masks(c, seg)
    chunks = [tuple(ref[0, ci * c:(ci + 1) * c, :] for ref in data) for ci in range(nb)]
    if per_seq_state:
        s0_ref, z_ref, sout_ref = refs[10:]
        nseg = c // seg
        h0 = [[_state_in(s0_ref, ci * nseg + g) for g in range(nseg)] for ci in range(nb)]
        zs, hs = _wkv_block(chunks, prm, msk, h0, seg)
        for si in range(nb * nseg):
            _state_out(sout_ref, si, hs[si])
    else:
        z_ref, sout_ref, h_ref = refs[10:]
        b = pl.program_id(1)

        @pl.when(b % blocks_per_seq == 0)
        def _():
            h_ref[...] = jnp.zeros(h_ref.shape, F32)

        zs, hs = _wkv_block(chunks, prm, msk, h_ref[...], c)
        h_ref[...] = hs[-1]

        @pl.when(b % blocks_per_seq == blocks_per_seq - 1)
        def _():
            _state_out(sout_ref, 0, hs[-1])
    for ci in range(nb):
        z_ref[ci * c:(ci + 1) * c, :] = zs[ci]


def _pair_rows(a):
    return a.reshape(HEAD_PAIRS, 1, PAIR_W)


def _wkv(r, k, v, lw, alr, params, s0, n_seq, seq_len):
    c = WKV_CHUNK
    per_seq_state = s0 is not None
    n_chunks = n_seq * seq_len // c
    if per_seq_state:
        assert c % seq_len == 0 and (n_seq * seq_len) % c == 0
        seg, nb, blocks_per_seq = seq_len, _pick(n_chunks, (8, 4, 2, 1)), 1
        seqs = nb * (c // seg)
        sspec = pl.BlockSpec((seqs, 2, HEAD_SIZE, HEAD_SIZE), lambda p, b: (b, p, 0, 0))
        extra_in, extra_args, scratch = [sspec], [s0], []
    else:
        assert seq_len % c == 0
        seg, nb = c, _pick(seq_len // c, (32, 16, 8, 4, 2, 1))
        blocks_per_seq = seq_len // c // nb
        sspec = pl.BlockSpec((1, 2, HEAD_SIZE, HEAD_SIZE), lambda p, b: (b // blocks_per_seq, p, 0, 0))
        extra_in, extra_args, scratch = [], [], [pltpu.VMEM((PAIR_W, PAIR_W), F32)]
    n_blocks = n_chunks // nb
    dspec = pl.BlockSpec((1, nb * c, PAIR_W), lambda p, b: (p, b, 0))
    pspec = pl.BlockSpec((1, 1, PAIR_W), lambda p, b: (p, 0, 0))
    return pl.pallas_call(
        functools.partial(_wkv_kernel, c=c, seg=seg, nb=nb, blocks_per_seq=blocks_per_seq,
                          per_seq_state=per_seq_state),
        grid=(HEAD_PAIRS, n_blocks),
        in_specs=[dspec] * 5 + [pspec] * 5 + extra_in,
        out_specs=[pl.BlockSpec((nb * c, PAIR_W), lambda p, b: (b, p)), sspec],
        out_shape=[jax.ShapeDtypeStruct((n_seq * seq_len, D_MODEL), F32),
                   jax.ShapeDtypeStruct((n_seq, RWKV_HEADS, HEAD_SIZE, HEAD_SIZE), F32)],
        scratch_shapes=scratch,
        compiler_params=_cparams(("parallel", "arbitrary")),
        name="wkv",
    )(r, k, v, lw, alr, *[_pair_rows(p) for p in params], *extra_args)


def kernel(x_prompt, x_sample, cache_kv_latent, cache_k_rope, state_wkv, state_shift, page_table,
           ffn_norm, ffn_w_gate, ffn_w_up, ffn_w_down, mix_norm,
           mla_w_down, mla_g_q_lat, mla_g_kv_lat, mla_w_uq, mla_w_uk, mla_w_uv, mla_g_qn, mla_g_kn, mla_w_o,
           rwkv_mu, rwkv_w_r, rwkv_w_k, rwkv_w_v, rwkv_w_o, rwkv_w0, rwkv_w1, rwkv_w2,
           rwkv_a0, rwkv_a1, rwkv_a2, rwkv_v0, rwkv_v1, rwkv_v2, rwkv_g1, rwkv_g2,
           rwkv_k_k, rwkv_k_a, rwkv_r_k, rwkv_lnx_w, rwkv_lnx_b):
    nb, l, _ = x_prompt.shape
    ns, nq, _ = x_sample.shape
    n_p, n_s = nb * l, ns * nq
    depth = ffn_norm.shape[0]
    past = page_table.shape[1] * PAGE_SIZE
    x = (x_prompt.reshape(n_p, D_MODEL), x_sample.reshape(n_s, D_MODEL))

    tm = _pick(n_s, (512, 256, 128, 64, 32, 16, 8))
    assert l % tm == 0 and tm % nq == 0
    cos_p, sin_p = _rope_tables(jnp.arange(l))
    cos_s, sin_s = _rope_tables(past + jnp.arange(nq))
    ctab = jnp.concatenate([cos_p, jnp.tile(cos_s, (tm // nq, 1))], axis=0)
    stab = jnp.concatenate([sin_p, jnp.tile(sin_s, (tm // nq, 1))], axis=0)
    n_ptiles, tiles_per_seq = n_p // tm, l // tm
    tab_index = lambda i: jnp.where(i < n_ptiles, i % tiles_per_seq, tiles_per_seq)

    lat_rows, rope_rows, p_wkv, p_shift, s_wkv, s_shift = [], [], [], [], [], []
    vf_p = vf_s = None
    for i in range(depth):
        x = _ffn(x, ffn_norm[i, 0], ffn_w_gate[i, 0].astype(BF16), ffn_w_up[i, 0].astype(BF16),
                 ffn_w_down[i, 0].astype(BF16))
        if i % 2 == 0:
            m = i // 2
            w = _mla_weights(m, mla_w_down, mla_g_q_lat, mla_g_kv_lat, mla_w_uq, mla_w_uk, mla_w_uv,
                             mla_g_qn, mla_g_kn)
            q, c, kr, k, v = _mla_proj(x, mix_norm[i], w, ctab, stab, tm, tab_index)
            o_p = _prompt_attn(q, k, v, nb, l)
            q_s = jnp.transpose(q[:, n_p:, :].reshape(MLA_HEADS, ns, nq, QK_PAD), (1, 0, 2, 3))
            q_s = q_s.reshape(ns, MLA_HEADS * nq, QK_PAD)
            o_s = _sample_attn(q_s, c, kr, n_p, cache_kv_latent, cache_k_rope, m, page_table,
                               jnp.transpose(w["wuk"]), w["wuk"], w["wuv"], w["gkn"])
            proj = (o_p, o_s, mla_w_o[m].astype(BF16), None, None)
            lat_rows.append(c)
            rope_rows.append(kr)
        else:
            j = i // 2
            mu, w = _rwkv_weights(j, rwkv_mu, rwkv_w_r, rwkv_w_k, rwkv_w_v, rwkv_w0, rwkv_w1, rwkv_w2,
                                  rwkv_a0, rwkv_a1, rwkv_a2, rwkv_v0, rwkv_v1, rwkv_v2, rwkv_g1, rwkv_g2)
            params = [rwkv_k_k[j], rwkv_k_a[j], rwkv_r_k[j], rwkv_lnx_w[j], rwkv_lnx_b[j]]
            r, k, v, lw, alr, gate_p, hl = _rwkv_proj(x, 0, n_p, mix_norm[i], mu, w, vf_p, None, l)
            if j == 0:
                vf_p = v
            z_p, st_p = _wkv(r, k, v, lw, alr, params, None, nb, l)
            p_wkv.append(st_p)
            p_shift.append(hl.reshape(nb, -1, 8, D_MODEL)[:, -1, 7])
            shift_exp = jnp.zeros((ns, nq, D_MODEL), F32).at[:, 0, :].set(state_shift[j]).reshape(n_s, D_MODEL)
            r, k, v, lw, alr, gate_s, h_s = _rwkv_proj(x, n_p, n_s, mix_norm[i], mu, w, vf_s, shift_exp, nq)
            if j == 0:
                vf_s = v
            z_s, st_s = _wkv(r, k, v, lw, alr, params, state_wkv[j], ns, nq)
            s_wkv.append(st_s)
            s_shift.append(h_s.reshape(ns, nq, D_MODEL)[:, -1])
            proj = (z_p, z_s, rwkv_w_o[j].astype(BF16), gate_p, gate_s)
        x = _ffn(x, ffn_norm[i, 1], ffn_w_gate[i, 1].astype(BF16), ffn_w_up[i, 1].astype(BF16),
                 ffn_w_down[i, 1].astype(BF16), proj, split=i == depth - 1)

    lat = jnp.stack(lat_rows)
    rope = jnp.stack(rope_rows)
    return (x[0].reshape(nb, l, D_MODEL), x[1].reshape(ns, nq, D_MODEL),
            lat[:, :n_p].reshape(-1, nb, l, KV_RANK), rope[:, :n_p].reshape(-1, nb, l, ROPE_DIM),
            jnp.stack(p_wkv), jnp.stack(p_shift),
            lat[:, n_p:].reshape(-1, ns, nq, KV_RANK), rope[:, n_p:].reshape(-1, ns, nq, ROPE_DIM),
            jnp.stack(s_wkv), jnp.stack(s_shift))
```
